```python
import math
import jax, jax.numpy as jnp
from jax import lax
import numpy as np

D_MODEL = 2048
BATCH = 2
SEQ = 8192
DEPTH = 1

MIX_WIDTH = D_MODEL
CONV_WIDTH = MIX_WIDTH // 2
CONV_GROUPS = 8
MLA_HEADS = 8
QK_NOPE_DIM = 128
QK_ROPE_DIM = 64
V_HEAD_DIM = 128
Q_LORA_RANK = 768
KV_LORA_RANK = 512
ROPE_THETA = 10000.0
Q_BLOCK = 128
D_FF = 5632
CONV_K = 3
RMS_EPS = 1e-6
N_MOD = 6

IN_SPLITS = (Q_LORA_RANK, KV_LORA_RANK, QK_ROPE_DIM, CONV_WIDTH, CONV_WIDTH, CONV_WIDTH)
IN_COLS = sum(IN_SPLITS)

kernel_name = "hybrid_mla_shortconv_convffn_adaln"


def rms_norm(x, g):
    xf = x.astype(jnp.float32)
    y = xf * lax.rsqrt(jnp.mean(xf * xf, axis=-1, keepdims=True) + RMS_EPS)
    return (y * g.astype(jnp.float32)).astype(x.dtype)


def rope(x, cos, sin):
    x1, x2 = jnp.split(x, 2, axis=-1)
    return jnp.concatenate([x1 * cos - x2 * sin, x2 * cos + x1 * sin], axis=-1)


def causal_dwconv3(u, w, b):
    s = u.shape[1]
    up = jnp.pad(u, ((0, 0), (CONV_K - 1, 0), (0, 0)))
    return up[:, :s] * w[0] + up[:, 1:s + 1] * w[1] + u * w[2] + b


def mla_attention(q_nope, q_rope, k_nope, k_rope, v):
    b, s, h, _ = q_nope.shape
    nb = s // Q_BLOCK
    scale = 1.0 / math.sqrt(QK_NOPE_DIM + QK_ROPE_DIM)
    k_idx = jnp.arange(s)
    neg = jnp.finfo(jnp.float32).min

    def blockify(t):
        return t.reshape(b, nb, Q_BLOCK, *t.shape[2:]).swapaxes(0, 1)

    def one_block(args):
        qn, qr, i = args
        sc = (jnp.einsum('bqhd,bkhd->bhqk', qn, k_nope)
              + jnp.einsum('bqhd,bkd->bhqk', qr, k_rope)).astype(jnp.float32) * scale
        q_idx = i * Q_BLOCK + jnp.arange(Q_BLOCK)
        mask = k_idx[None, :] <= q_idx[:, None]
        sc = jnp.where(mask, sc, neg)
        p = jax.nn.softmax(sc, axis=-1).astype(v.dtype)
        return jnp.einsum('bhqk,bkhd->bqhd', p, v)

    out = lax.map(one_block, (blockify(q_nope), blockify(q_rope), jnp.arange(nb)))
    return out.swapaxes(0, 1).reshape(b, s, h * V_HEAD_DIM)


def setup_inputs(seed: int = 0) -> dict:
    key = jax.random.key(seed)
    ks = jax.random.split(key, 24)
    f32 = jnp.float32

    def nrm(k, shape, fan_in):
        return jax.random.normal(k, shape, f32) * (fan_in ** -0.5)

    def gain(k, n):
        return 1.0 + 0.02 * jax.random.normal(k, (DEPTH, n), f32)

    x = jax.random.normal(ks[0], (BATCH, SEQ, D_MODEL), f32)
    c = jax.random.normal(ks[1], (BATCH, D_MODEL), f32)
    offset = jax.random.randint(ks[2], (BATCH, 1), 0, 1024, dtype=jnp.int32)
    positions = offset + jnp.arange(SEQ, dtype=jnp.int32)[None, :]
    return {
        "x": x,
        "c": c,
        "positions": positions,
        "w_ada": nrm(ks[3], (DEPTH, D_MODEL, N_MOD * D_MODEL), D_MODEL),
        "b_ada": 0.02 * jax.random.normal(ks[4], (DEPTH, N_MOD * D_MODEL), f32),
        "g_pre_mix": gain(ks[5], D_MODEL),
        "g_post_mix": gain(ks[6], D_MODEL),
        "w_in": nrm(ks[7], (DEPTH, D_MODEL, IN_COLS), D_MODEL),
        "g_q": gain(ks[8], Q_LORA_RANK),
        "w_uq": nrm(ks[9], (DEPTH, Q_LORA_RANK, MLA_HEADS * (QK_NOPE_DIM + QK_ROPE_DIM)), Q_LORA_RANK),
        "g_kv": gain(ks[10], KV_LORA_RANK),
        "w_ukv": nrm(ks[11], (DEPTH, KV_LORA_RANK, MLA_HEADS * (QK_NOPE_DIM + V_HEAD_DIM)), KV_LORA_RANK),
        "conv_w_mix": nrm(ks[12], (DEPTH, CONV_K, CONV_WIDTH), CONV_K),
        "conv_b_mix": 0.02 * jax.random.normal(ks[13], (DEPTH, CONV_WIDTH), f32),
        "w_o": nrm(ks[14], (DEPTH, MIX_WIDTH, D_MODEL), MIX_WIDTH),
        "g_pre_ffn": gain(ks[15], D_MODEL),
        "g_post_ffn": gain(ks[16], D_MODEL),
        "w_up": nrm(ks[17], (DEPTH, D_MODEL, 2 * D_FF), D_MODEL),
        "conv_w_ffn": nrm(ks[18], (DEPTH, CONV_K, 2 * D_FF), CONV_K),
        "conv_b_ffn": 0.02 * jax.random.normal(ks[19], (DEPTH, 2 * D_FF), f32),
        "w_down": nrm(ks[20], (DEPTH, D_FF, D_MODEL), D_FF),
    }


def reference(x, c, positions, w_ada, b_ada, g_pre_mix, g_post_mix, w_in, g_q, w_uq,
              g_kv, w_ukv, conv_w_mix, conv_b_mix, w_o, g_pre_ffn, g_post_ffn,
              w_up, conv_w_ffn, conv_b_ffn, w_down):
    b, s, _ = x.shape
    inv_freq = 1.0 / (ROPE_THETA ** (jnp.arange(0, QK_ROPE_DIM, 2, dtype=jnp.float32) / QK_ROPE_DIM))
    ang = positions.astype(jnp.float32)[..., None] * inv_freq
    cos = jnp.cos(ang).astype(x.dtype)
    sin = jnp.sin(ang).astype(x.dtype)
    c_act = jax.nn.silu(c)
    cut = np.cumsum(IN_SPLITS)[:-1].tolist()

    for l in range(DEPTH):
        mod = c_act @ w_ada[l] + b_ada[l]
        sh_m, sc_m, gt_m, sh_f, sc_f, gt_f = [m[:, None, :] for m in jnp.split(mod, N_MOD, axis=-1)]

        h = rms_norm(x, g_pre_mix[l]) * (1.0 + sc_m) + sh_m
        proj = h @ w_in[l]
        q_lat, kv_lat, k_rope, gate_b, gate_c, conv_in = jnp.split(proj, cut, axis=-1)

        q = (rms_norm(q_lat, g_q[l]) @ w_uq[l]).reshape(b, s, MLA_HEADS, QK_NOPE_DIM + QK_ROPE_DIM)
        q_nope, q_rope = q[..., :QK_NOPE_DIM], q[..., QK_NOPE_DIM:]
        q_rope = rope(q_rope, cos[:, :, None, :], sin[:, :, None, :])
        k_rope = rope(k_rope, cos, sin)
        kv = (rms_norm(kv_lat, g_kv[l]) @ w_ukv[l]).reshape(b, s, MLA_HEADS, QK_NOPE_DIM + V_HEAD_DIM)
        k_nope, v = kv[..., :QK_NOPE_DIM], kv[..., QK_NOPE_DIM:]
        attn_out = mla_attention(q_nope, q_rope, k_nope, k_rope, v)

        conv_out = gate_b * causal_dwconv3(gate_c * conv_in, conv_w_mix[l], conv_b_mix[l])

        mix = jnp.concatenate([attn_out, conv_out], axis=-1) @ w_o[l]
        x = x + gt_m * rms_norm(mix, g_post_mix[l])

        h = rms_norm(x, g_pre_ffn[l]) * (1.0 + sc_f) + sh_f
        u = causal_dwconv3(h @ w_up[l], conv_w_ffn[l], conv_b_ffn[l])
        a, g = jnp.split(u, 2, axis=-1)
        y = (jax.nn.silu(g) * a) @ w_down[l]
        x = x + gt_f * rms_norm(y, g_post_ffn[l])
    return x
```

```python
import functools
import math

import jax
import jax.numpy as jnp
from jax import lax
from jax.experimental import pallas as pl
from jax.experimental.pallas import tpu as pltpu

D_MODEL = 2048
SEQ = 8192
CONV_WIDTH = 1024
MLA_HEADS = 8
QK_NOPE_DIM = 128
QK_ROPE_DIM = 64
V_HEAD_DIM = 128
Q_LORA_RANK = 768
KV_LORA_RANK = 512
ROPE_THETA = 10000.0
D_FF = 5632
RMS_EPS = 1e-6
N_MOD = 6

LANES = 128
HEAD_QK = 2 * LANES
PROJ_COLS = 4608
VMEM_LIMIT = 56 * 1024 * 1024

OFF_GATE_B = 0
OFF_GATE_C = 1024
OFF_CONV_IN = 2048
OFF_Q_LAT = 3072
OFF_K_ROPE = 3840
OFF_KV_LAT = 4096

BF16 = jnp.bfloat16
F32 = jnp.float32


def _params(*sem):
    return pltpu.CompilerParams(dimension_semantics=sem, vmem_limit_bytes=VMEM_LIMIT)


def _rms(x, g):
    return x * lax.rsqrt(jnp.mean(x * x, axis=-1, keepdims=True) + RMS_EPS) * g


def _mod_kernel(c_ref, w_ref, b_ref, o_ref):
    c = c_ref[...]
    c_act = c * (1.0 / (1.0 + jnp.exp(-c)))
    o_ref[...] = jnp.dot(c_act.astype(BF16), w_ref[...].astype(BF16),
                         preferred_element_type=F32) + b_ref[...]


def _adaln_mod(c_pad, w_ada, b_ada):
    rows, d = c_pad.shape
    n = w_ada.shape[1]
    tn = 1536
    return pl.pallas_call(
        _mod_kernel,
        grid=(n // tn,),
        in_specs=[pl.BlockSpec((rows, d), lambda j: (0, 0)),
                  pl.BlockSpec((d, tn), lambda j: (0, j)),
                  pl.BlockSpec((1, tn), lambda j: (0, j))],
        out_specs=pl.BlockSpec((rows, tn), lambda j: (0, j)),
        out_shape=jax.ShapeDtypeStruct((rows, n), F32),
        compiler_params=_params("arbitrary"),
        name="adaln_mod",
    )(c_pad, w_ada, b_ada)


def _inproj_kernel(x_ref, sc_ref, sh_ref, g_ref, w_ref, o_ref, h_scr):
    @pl.when(pl.program_id(1) == 0)
    def _():
        h = _rms(x_ref[...], g_ref[...]) * (1.0 + sc_ref[...]) + sh_ref[...]
        h_scr[...] = h.astype(BF16)

    o_ref[...] = jnp.dot(h_scr[...], w_ref[...],
                         preferred_element_type=F32).astype(o_ref.dtype)


def _in_proj(x2, mod3, g_pre, w_in_p, tm, tn):
    t, d = x2.shape
    n = w_in_p.shape[1]
    per_b = SEQ // tm
    return pl.pallas_call(
        _inproj_kernel,
        grid=(t // tm, n // tn),
        in_specs=[pl.BlockSpec((tm, d), lambda i, j: (i, 0)),
                  pl.BlockSpec((None, 1, d), lambda i, j: ((i // per_b) * N_MOD + 1, 0, 0)),
                  pl.BlockSpec((None, 1, d), lambda i, j: ((i // per_b) * N_MOD + 0, 0, 0)),
                  pl.BlockSpec((1, d), lambda i, j: (0, 0)),
                  pl.BlockSpec((d, tn), lambda i, j: (0, j))],
        out_specs=pl.BlockSpec((tm, tn), lambda i, j: (i, j)),
        out_shape=jax.ShapeDtypeStruct((t, n), BF16),
        scratch_shapes=[pltpu.VMEM((tm, d), BF16)],
        compiler_params=_params("arbitrary", "arbitrary"),
        name="in_proj",
    )(x2, mod3, mod3, g_pre, w_in_p)


def _qkv_kernel(ql_ref, kr_ref, kvl_ref, pos_ref, invf_ref, gq_ref, gkv_ref,
                wq_ref, wkv_ref, q_ref, k_ref, v_ref):
    tm = ql_ref.shape[0]
    ang = pos_ref[...].astype(F32) * invf_ref[...]
    lane = lax.broadcasted_iota(jnp.int32, (tm, LANES), 1)
    rope_lane = lane < QK_ROPE_DIM
    cos = jnp.where(rope_lane, jnp.cos(ang), 0.0)
    sin = jnp.where(rope_lane, jnp.sin(ang), 0.0)

    def rope(a):
        return a * cos + pltpu.roll(a, QK_ROPE_DIM, 1) * sin

    qn = _rms(ql_ref[...].astype(F32), gq_ref[...]).astype(BF16)
    q = jnp.dot(qn, wq_ref[...], preferred_element_type=F32)
    scale = 1.0 / math.sqrt(QK_NOPE_DIM + QK_ROPE_DIM)
    kvn = _rms(kvl_ref[...].astype(F32), gkv_ref[...]).astype(BF16)
    kv = jnp.dot(kvn, wkv_ref[...], preferred_element_type=F32)
    k_rope = rope(kr_ref[...].astype(F32)).astype(BF16)
    for h in range(MLA_HEADS):
        lo = h * HEAD_QK
        q_ref[:, lo:lo + LANES] = (q[:, lo:lo + LANES] * scale).astype(BF16)
        q_ref[:, lo + LANES:lo + HEAD_QK] = (rope(q[:, lo + LANES:lo + HEAD_QK]) * scale).astype(BF16)
        k_ref[:, lo:lo + LANES] = kv[:, h * LANES:(h + 1) * LANES].astype(BF16)
        k_ref[:, lo + LANES:lo + HEAD_QK] = k_rope
    v_ref[...] = kv[:, MLA_HEADS * QK_NOPE_DIM:].astype(BF16)


def _latent_qkv(proj, pos2, invf, g_q, g_kv, wq_p, wkv_p, tm):
    t = proj.shape[0]
    hq = MLA_HEADS * HEAD_QK
    hv = MLA_HEADS * V_HEAD_DIM
    const = lambda i: (0, 0)
    return pl.pallas_call(
        _qkv_kernel,
        grid=(t // tm,),
        in_specs=[pl.BlockSpec((tm, Q_LORA_RANK), lambda i: (i, OFF_Q_LAT // Q_LORA_RANK)),
                  pl.BlockSpec((tm, LANES), lambda i: (i, OFF_K_ROPE // LANES)),
                  pl.BlockSpec((tm, KV_LORA_RANK), lambda i: (i, OFF_KV_LAT // KV_LORA_RANK)),
                  pl.BlockSpec((tm, 1), lambda i: (i, 0)),
                  pl.BlockSpec((1, LANES), const),
                  pl.BlockSpec((1, Q_LORA_RANK), const),
                  pl.BlockSpec((1, KV_LORA_RANK), const),
                  pl.BlockSpec((Q_LORA_RANK, hq), const),
                  pl.BlockSpec((KV_LORA_RANK, 2 * hv), const)],
        out_specs=[pl.BlockSpec((tm, hq), lambda i: (i, 0)),
                   pl.BlockSpec((tm, hq), lambda i: (i, 0)),
                   pl.BlockSpec((tm, hv), lambda i: (i, 0))],
        out_shape=[jax.ShapeDtypeStruct((t, hq), BF16),
                   jax.ShapeDtypeStruct((t, hq), BF16),
                   jax.ShapeDtypeStruct((t, hv), BF16)],
        compiler_params=_params("arbitrary"),
        name="latent_qkv",
    )(proj, proj, proj, pos2, invf, g_q, g_kv, wq_p, wkv_p)


def _causal_conv3(u, halo, w_ref, b_ref):
    tm = u.shape[0]
    row = lax.broadcasted_iota(jnp.int32, u.shape, 0)
    h1 = halo[7:8, :]
    h2 = halo[6:7, :]
    p1 = jnp.where(row == 0, h1, pltpu.roll(u, 1, 0))
    p2 = jnp.where(row == 0, h2, jnp.where(row == 1, h1, pltpu.roll(u, 2, 0)))
    return p2 * w_ref[0:1, :] + p1 * w_ref[1:2, :] + u * w_ref[2:3, :] + b_ref[...]


def _convgate_kernel(gb_ref, gc_ref, ci_ref, gch_ref, cih_ref, w_ref, b_ref, o_ref, *, per_b):
    first = (pl.program_id(0) % per_b) == 0
    g = gc_ref[...].astype(F32) * ci_ref[...].astype(F32)
    gh = gch_ref[...].astype(F32) * cih_ref[...].astype(F32)
    halo = jnp.where(first, 0.0, gh[8:16, :])
    o_ref[...] = (gb_ref[...].astype(F32) * _causal_conv3(g, halo, w_ref, b_ref)).astype(o_ref.dtype)


def _conv_gate(proj, conv_w, conv_b, tm):
    t = proj.shape[0]
    c = CONV_WIDTH
    hb = 16
    per_b = SEQ // tm
    halo_idx = lambda col: (lambda i: (jnp.maximum(i * (tm // hb) - 1, 0), col))
    return pl.pallas_call(
        functools.partial(_convgate_kernel, per_b=per_b),
        grid=(t // tm,),
        in_specs=[pl.BlockSpec((tm, c), lambda i: (i, OFF_GATE_B // c)),
                  pl.BlockSpec((tm, c), lambda i: (i, OFF_GATE_C // c)),
                  pl.BlockSpec((tm, c), lambda i: (i, OFF_CONV_IN // c)),
                  pl.BlockSpec((hb, c), halo_idx(OFF_GATE_C // c)),
                  pl.BlockSpec((hb, c), halo_idx(OFF_CONV_IN // c)),
                  pl.BlockSpec((3, c), lambda i: (0, 0)),
                  pl.BlockSpec((1, c), lambda i: (0, 0))],
        out_specs=pl.BlockSpec((tm, c), lambda i: (i, 0)),
        out_shape=jax.ShapeDtypeStruct((t, c), BF16),
        compiler_params=_params("arbitrary"),
        name="conv_gate",
    )(proj, proj, proj, proj, proj, conv_w, conv_b)


def _attn_kernel(q_ref, k_ref, v_ref, o_ref, m_scr, l_scr, acc_scr, *, blk):
    qi = pl.program_id(2)
    q = q_ref[...]
    m_scr[...] = jnp.full(m_scr.shape, -1e30, F32)
    l_scr[...] = jnp.zeros(l_scr.shape, F32)
    acc_scr[...] = jnp.zeros(acc_scr.shape, F32)

    def step(j, masked):
        start = pl.multiple_of(j * blk, blk)
        k = k_ref[pl.ds(start, blk), :]
        v = v_ref[pl.ds(start, blk), :]
        s = lax.dot_general(q, k, (((1,), (1,)), ((), ())), preferred_element_type=F32)
        if masked:
            r = lax.broadcasted_iota(jnp.int32, s.shape, 0)
            c = lax.broadcasted_iota(jnp.int32, s.shape, 1)
            s = jnp.where(c <= r, s, -1e30)
        m_old = m_scr[...]
        m_new = jnp.maximum(m_old, jnp.max(s, axis=-1, keepdims=True))
        alpha = jnp.exp(m_old - m_new)
        p = jnp.exp(s - m_new)
        l_scr[...] = alpha * l_scr[...] + jnp.sum(p, axis=-1, keepdims=True)
        acc_scr[...] = alpha * acc_scr[...] + jnp.dot(p.astype(BF16), v, preferred_element_type=F32)
        m_scr[...] = m_new

    def body(j, carry):
        step(j, False)
        return carry

    lax.fori_loop(0, qi, body, 0)
    step(qi, True)
    o_ref[...] = (acc_scr[...] / l_scr[...]).astype(o_ref.dtype)


def _attention(q, k, v, blk):
    t = q.shape[0]
    b = t // SEQ
    nq = SEQ // blk
    return pl.pallas_call(
        functools.partial(_attn_kernel, blk=blk),
        grid=(b, MLA_HEADS, nq),
        in_specs=[pl.BlockSpec((blk, HEAD_QK), lambda bi, h, i: (bi * nq + i, h)),
                  pl.BlockSpec((SEQ, HEAD_QK), lambda bi, h, i: (bi, h)),
                  pl.BlockSpec((SEQ, V_HEAD_DIM), lambda bi, h, i: (bi, h))],
        out_specs=pl.BlockSpec((blk, V_HEAD_DIM), lambda bi, h, i: (bi * nq + i, h)),
        out_shape=jax.ShapeDtypeStruct((t, MLA_HEADS * V_HEAD_DIM), BF16),
        scratch_shapes=[pltpu.VMEM((blk, 1), F32),
                        pltpu.VMEM((blk, 1), F32),
                        pltpu.VMEM((blk, V_HEAD_DIM), F32)],
        compiler_params=_params("arbitrary", "arbitrary", "arbitrary"),
        name="attention",
    )(q, k, v)


def _outproj_kernel(a_ref, c_ref, x_ref, gt_ref, g_ref, wa_ref, wc_ref, o_ref):
    mix = (jnp.dot(a_ref[...], wa_ref[...], preferred_element_type=F32)
           + jnp.dot(c_ref[...], wc_ref[...], preferred_element_type=F32))
    o_ref[...] = x_ref[...] + gt_ref[...] * _rms(mix, g_ref[...])


def _out_proj(attn, conv, x2, mod3, g_post, w_o_bf, tm):
    t, d = x2.shape
    ka = attn.shape[1]
    kc = conv.shape[1]
    per_b = SEQ // tm
    return pl.pallas_call(
        _outproj_kernel,
        grid=(t // tm,),
        in_specs=[pl.BlockSpec((tm, ka), lambda i: (i, 0)),
                  pl.BlockSpec((tm, kc), lambda i: (i, 0)),
                  pl.BlockSpec((tm, d), lambda i: (i, 0)),
                  pl.BlockSpec((None, 1, d), lambda i: ((i // per_b) * N_MOD + 2, 0, 0)),
                  pl.BlockSpec((1, d), lambda i: (0, 0)),
                  pl.BlockSpec((ka, d), lambda i: (0, 0)),
                  pl.BlockSpec((kc, d), lambda i: (1, 0))],
        out_specs=pl.BlockSpec((tm, d), lambda i: (i, 0)),
        out_shape=jax.ShapeDtypeStruct((t, d), F32),
        compiler_params=_params("arbitrary"),
        name="out_proj",
    )(attn, conv, x2, mod3, g_post, w_o_bf, w_o_bf)


def _ffn_kernel(x_ref, sc_ref, sh_ref, gt_ref, gpre_ref, gpost_ref, wa_ref, wg_ref,
                cwa_ref, cwg_ref, cba_ref, cbg_ref, wd_ref, o_ref,
                h_scr, acc_scr, halo_a, halo_g, *, per_b, nf):
    i = pl.program_id(0)
    f = pl.program_id(1)
    tm = x_ref.shape[0]

    @pl.when(f == 0)
    def _():
        h = _rms(x_ref[...], gpre_ref[...]) * (1.0 + sc_ref[...]) + sh_ref[...]
        h_scr[...] = h.astype(BF16)
        acc_scr[...] = jnp.zeros(acc_scr.shape, F32)

    h = h_scr[...]
    first = (i % per_b) == 0

    def branch(w_ref, cw_ref, cb_ref, halo_ref):
        u = jnp.dot(h, w_ref[...], preferred_element_type=F32)
        halo = jnp.where(first, 0.0, halo_ref[f])
        halo_ref[f] = u[tm - 8:, :]
        return _causal_conv3(u, halo, cw_ref, cb_ref)

    a = branch(wa_ref, cwa_ref, cba_ref, halo_a)
    g = branch(wg_ref, cwg_ref, cbg_ref, halo_g)
    y = (g * (1.0 / (1.0 + jnp.exp(-g))) * a).astype(BF16)
    acc_scr[...] += jnp.dot(y, wd_ref[...], preferred_element_type=F32)

    @pl.when(f == nf - 1)
    def _():
        o_ref[...] = x_ref[...] + gt_ref[...] * _rms(acc_scr[...], gpost_ref[...])


def _conv_ffn(x1, mod3, g_pre, g_post, w_up_bf, conv_w, conv_b, w_down_bf, tm, tf):
    t, d = x1.shape
    nf = D_FF // tf
    per_b = SEQ // tm
    modspec = lambda k: pl.BlockSpec((None, 1, d), lambda i, f: ((i // per_b) * N_MOD + k, 0, 0))
    return pl.pallas_call(
        functools.partial(_ffn_kernel, per_b=per_b, nf=nf),
        grid=(t // tm, nf),
        in_specs=[pl.BlockSpec((tm, d), lambda i, f: (i, 0)),
                  modspec(4), modspec(3), modspec(5),
                  pl.BlockSpec((1, d), lambda i, f: (0, 0)),
                  pl.BlockSpec((1, d), lambda i, f: (0, 0)),
                  pl.BlockSpec((d, tf), lambda i, f: (0, f)),
                  pl.BlockSpec((d, tf), lambda i, f: (0, nf + f)),
                  pl.BlockSpec((3, tf), lambda i, f: (0, f)),
                  pl.BlockSpec((3, tf), lambda i, f: (0, nf + f)),
                  pl.BlockSpec((1, tf), lambda i, f: (0, f)),
                  pl.BlockSpec((1, tf), lambda i, f: (0, nf + f)),
                  pl.BlockSpec((tf, d), lambda i, f: (f, 0))],
        out_specs=pl.BlockSpec((tm, d), lambda i, f: (i, 0)),
        out_shape=jax.ShapeDtypeStruct((t, d), F32),
        scratch_shapes=[pltpu.VMEM((tm, d), BF16),
                        pltpu.VMEM((tm, d), F32),
                        pltpu.VMEM((nf, 8, tf), F32),
                        pltpu.VMEM((nf, 8, tf), F32)],
        compiler_params=_params("arbitrary", "arbitrary"),
        name="conv_ffn",
    )(x1, mod3, mod3, mod3, g_pre, g_post, w_up_bf, w_up_bf, conv_w, conv_w,
      conv_b, conv_b, w_down_bf)


def _rot_half_cols(w):
    half = w.shape[-1] // 2
    return jnp.concatenate([-w[..., half:], w[..., :half]], axis=-1)


def _regroup_w_in(w):
    d = w.shape[0]
    c0 = Q_LORA_RANK
    c1 = c0 + KV_LORA_RANK
    c2 = c1 + QK_ROPE_DIM
    c3 = c2 + CONV_WIDTH
    c4 = c3 + CONV_WIDTH
    q_lat, kv_lat, k_rope = w[:, :c0], w[:, c0:c1], w[:, c1:c2]
    gate_b, gate_c, conv_in = w[:, c2:c3], w[:, c3:c4], w[:, c4:]
    pad = jnp.zeros((d, OFF_KV_LAT - OFF_K_ROPE - 2 * QK_ROPE_DIM), w.dtype)
    return jnp.concatenate([gate_b, gate_c, conv_in, q_lat, k_rope, _rot_half_cols(k_rope),
                            pad, kv_lat], axis=1)


def _regroup_w_uq(w):
    r = w.shape[0]
    w = w.reshape(r, MLA_HEADS, QK_NOPE_DIM + QK_ROPE_DIM)
    nope, rope = w[..., :QK_NOPE_DIM], w[..., QK_NOPE_DIM:]
    return jnp.concatenate([nope, rope, _rot_half_cols(rope)], axis=-1).reshape(r, MLA_HEADS * HEAD_QK)


def _regroup_w_ukv(w):
    r = w.shape[0]
    w = w.reshape(r, MLA_HEADS, QK_NOPE_DIM + V_HEAD_DIM)
    k = w[..., :QK_NOPE_DIM].reshape(r, MLA_HEADS * QK_NOPE_DIM)
    v = w[..., QK_NOPE_DIM:].reshape(r, MLA_HEADS * V_HEAD_DIM)
    return jnp.concatenate([k, v], axis=1)


def kernel(x, c, positions, w_ada, b_ada, g_pre_mix, g_post_mix, w_in, g_q, w_uq, g_kv, w_ukv,
           conv_w_mix, conv_b_mix, w_o, g_pre_ffn, g_post_ffn, w_up, conv_w_ffn, conv_b_ffn, w_down):
    b, s, d = x.shape
    assert (s, d) == (SEQ, D_MODEL) and w_ada.shape[0] == 1
    t = b * s
    x2 = x.reshape(t, d)
    pos2 = positions.reshape(t, 1)
    half = jnp.arange(0, QK_ROPE_DIM, 2, dtype=F32) / QK_ROPE_DIM
    inv_freq = 1.0 / (ROPE_THETA ** half)
    invf = jnp.concatenate([inv_freq, inv_freq, jnp.zeros((LANES - QK_ROPE_DIM,), F32)]).reshape(1, LANES)

    rows = 8
    c_pad = jnp.zeros((rows, d), F32).at[:b].set(c)
    x_cur = x2
    for l in range(w_ada.shape[0]):
        mod = _adaln_mod(c_pad, w_ada[l], b_ada[l].reshape(1, -1))
        mod3 = mod[:b].reshape(b * N_MOD, 1, d)

        w_in_p = _regroup_w_in(w_in[l]).astype(BF16)
        wq_p = _regroup_w_uq(w_uq[l]).astype(BF16)
        wkv_p = _regroup_w_ukv(w_ukv[l]).astype(BF16)

        proj = _in_proj(x_cur, mod3, g_pre_mix[l].reshape(1, d), w_in_p, tm=1024, tn=1536)
        q, k, v = _latent_qkv(proj, pos2, invf, g_q[l].reshape(1, -1), g_kv[l].reshape(1, -1),
                              wq_p, wkv_p, tm=512)
        conv = _conv_gate(proj, conv_w_mix[l], conv_b_mix[l].reshape(1, -1), tm=512)
        attn = _attention(q, k, v, blk=512)
        x_cur = _out_proj(attn, conv, x_cur, mod3, g_post_mix[l].reshape(1, d),
                          w_o[l].astype(BF16), tm=512)
        x_cur = _conv_ffn(x_cur, mod3, g_pre_ffn[l].reshape(1, d), g_post_ffn[l].reshape(1, d),
                          w_up[l].astype(BF16), conv_w_ffn[l], conv_b_ffn[l].reshape(1, -1),
                          w_down[l].astype(BF16), tm=512, tf=512)
    return x_cur.reshape(b, s, d)
```

```python
import functools
import math

import jax
import jax.numpy as jnp
from jax import lax
from jax.experimental import pallas as pl
from jax.experimental.pallas import tpu as pltpu

D_MODEL = 2048
SEQ = 8192
CONV_WIDTH = 1024
MLA_HEADS = 8
QK_NOPE_DIM = 128
QK_ROPE_DIM = 64
V_HEAD_DIM = 128
Q_LORA_RANK = 768
KV_LORA_RANK = 512
ROPE_THETA = 10000.0
D_FF = 5632
RMS_EPS = 1e-6
N_MOD = 6

LANES = 128
HEAD_QK = 2 * LANES
PROJ_COLS = 4608
VMEM_LIMIT = 56 * 1024 * 1024
ATT_BLK = 512

OFF_GATE_B = 0
OFF_GATE_C = 1024
OFF_CONV_IN = 2048
OFF_Q_LAT = 3072
OFF_K_ROPE = 3840
OFF_KV_LAT = 4096

BF16 = jnp.bfloat16
F32 = jnp.float32


def _params(*sem):
    return pltpu.CompilerParams(dimension_semantics=sem, vmem_limit_bytes=VMEM_LIMIT)


def _rms(x, g):
    return x * lax.rsqrt(jnp.mean(x * x, axis=-1, keepdims=True) + RMS_EPS) * g


def _mod_kernel(c_ref, w_ref, b_ref, o_ref):
    c = c_ref[...]
    c_act = c * (1.0 / (1.0 + jnp.exp(-c)))
    o_ref[...] = jnp.dot(c_act.astype(BF16), w_ref[...].astype(BF16),
                         preferred_element_type=F32) + b_ref[...]


def _adaln_mod(c_pad, w_ada, b_ada):
    rows, d = c_pad.shape
    n = w_ada.shape[1]
    tn = 1536
    return pl.pallas_call(
        _mod_kernel,
        grid=(n // tn,),
        in_specs=[pl.BlockSpec((rows, d), lambda j: (0, 0)),
                  pl.BlockSpec((d, tn), lambda j: (0, j)),
                  pl.BlockSpec((1, tn), lambda j: (0, j))],
        out_specs=pl.BlockSpec((rows, tn), lambda j: (0, j)),
        out_shape=jax.ShapeDtypeStruct((rows, n), F32),
        compiler_params=_params("arbitrary"),
        name="adaln_mod",
    )(c_pad, w_ada, b_ada)


def _inproj_kernel(x_ref, sc_ref, sh_ref, g_ref, w_ref, o_ref, h_scr):
    @pl.when(pl.program_id(1) == 0)
    def _():
        h = _rms(x_ref[...], g_ref[...]) * (1.0 + sc_ref[...]) + sh_ref[...]
        h_scr[...] = h.astype(BF16)

    o_ref[...] = jnp.dot(h_scr[...], w_ref[...],
                         preferred_element_type=F32).astype(o_ref.dtype)


def _in_proj(x2, mod3, g_pre, w_in_p, tm, tn):
    t, d = x2.shape
    n = w_in_p.shape[1]
    per_b = SEQ // tm
    return pl.pallas_call(
        _inproj_kernel,
        grid=(t // tm, n // tn),
        in_specs=[pl.BlockSpec((tm, d), lambda i, j: (i, 0)),
                  pl.BlockSpec((None, 1, d), lambda i, j: ((i // per_b) * N_MOD + 1, 0, 0)),
                  pl.BlockSpec((None, 1, d), lambda i, j: ((i // per_b) * N_MOD + 0, 0, 0)),
                  pl.BlockSpec((1, d), lambda i, j: (0, 0)),
                  pl.BlockSpec((d, tn), lambda i, j: (0, j))],
        out_specs=pl.BlockSpec((tm, tn), lambda i, j: (i, j)),
        out_shape=jax.ShapeDtypeStruct((t, n), BF16),
        scratch_shapes=[pltpu.VMEM((tm, d), BF16)],
        compiler_params=_params("arbitrary", "arbitrary"),
        name="in_proj",
    )(x2, mod3, mod3, g_pre, w_in_p)


def _qkv_kernel(ql_ref, kr_ref, kvl_ref, pos_ref, invf_ref, gq_ref, gkv_ref,
                wq_ref, wkv_ref, qT_ref, k_ref, vT_ref):
    tm = ql_ref.shape[0]
    ang = pos_ref[...].astype(F32) * invf_ref[...]
    lane = lax.broadcasted_iota(jnp.int32, (tm, LANES), 1)
    rope_lane = lane < QK_ROPE_DIM
    cos = jnp.where(rope_lane, jnp.cos(ang), 0.0)
    sin = jnp.where(rope_lane, jnp.sin(ang), 0.0)

    def rope(a):
        return a * cos + pltpu.roll(a, QK_ROPE_DIM, 1) * sin

    qn = _rms(ql_ref[...].astype(F32), gq_ref[...]).astype(BF16)
    q = jnp.dot(qn, wq_ref[...], preferred_element_type=F32)
    scale = 1.0 / math.sqrt(QK_NOPE_DIM + QK_ROPE_DIM)
    kvn = _rms(kvl_ref[...].astype(F32), gkv_ref[...]).astype(BF16)
    kv = jnp.dot(kvn, wkv_ref[...], preferred_element_type=F32)
    k_rope = rope(kr_ref[...].astype(F32)).astype(BF16)
    v_off = MLA_HEADS * QK_NOPE_DIM
    for h in range(MLA_HEADS):
        lo = h * HEAD_QK
        qT_ref[lo:lo + LANES, :] = (q[:, lo:lo + LANES] * scale).T.astype(BF16)
        qT_ref[lo + LANES:lo + HEAD_QK, :] = (rope(q[:, lo + LANES:lo + HEAD_QK]) * scale).T.astype(BF16)
        k_ref[:, lo:lo + LANES] = kv[:, h * LANES:(h + 1) * LANES].astype(BF16)
        k_ref[:, lo + LANES:lo + HEAD_QK] = k_rope
        vT_ref[h * LANES:(h + 1) * LANES, :] = kv[:, v_off + h * LANES:v_off + (h + 1) * LANES].T.astype(BF16)


def _latent_qkv(proj, pos2, invf, g_q, g_kv, wq_p, wkv_p, tm):
    t = proj.shape[0]
    hq = MLA_HEADS * HEAD_QK
    hv = MLA_HEADS * V_HEAD_DIM
    const = lambda i: (0, 0)
    return pl.pallas_call(
        _qkv_kernel,
        grid=(t // tm,),
        in_specs=[pl.BlockSpec((tm, Q_LORA_RANK), lambda i: (i, OFF_Q_LAT // Q_LORA_RANK)),
                  pl.BlockSpec((tm, LANES), lambda i: (i, OFF_K_ROPE // LANES)),
                  pl.BlockSpec((tm, KV_LORA_RANK), lambda i: (i, OFF_KV_LAT // KV_LORA_RANK)),
                  pl.BlockSpec((tm, 1), lambda i: (i, 0)),
                  pl.BlockSpec((1, LANES), const),
                  pl.BlockSpec((1, Q_LORA_RANK), const),
                  pl.BlockSpec((1, KV_LORA_RANK), const),
                  pl.BlockSpec((Q_LORA_RANK, hq), const),
                  pl.BlockSpec((KV_LORA_RANK, 2 * hv), const)],
        out_specs=[pl.BlockSpec((None, hq, tm), lambda i: (i, 0, 0)),
                   pl.BlockSpec((tm, hq), lambda i: (i, 0)),
                   pl.BlockSpec((None, hv, tm), lambda i: (i, 0, 0))],
        out_shape=[jax.ShapeDtypeStruct((t // tm, hq, tm), BF16),
                   jax.ShapeDtypeStruct((t, hq), BF16),
                   jax.ShapeDtypeStruct((t // tm, hv, tm), BF16)],
        compiler_params=_params("arbitrary"),
        name="latent_qkv",
    )(proj, proj, proj, pos2, invf, g_q, g_kv, wq_p, wkv_p)


def _causal_conv3(u, halo, w_ref, b_ref):
    tm = u.shape[0]
    row = lax.broadcasted_iota(jnp.int32, u.shape, 0)
    h1 = halo[7:8, :]
    h2 = halo[6:7, :]
    p1 = jnp.where(row == 0, h1, pltpu.roll(u, 1, 0))
    p2 = jnp.where(row == 0, h2, jnp.where(row == 1, h1, pltpu.roll(u, 2, 0)))
    return p2 * w_ref[0:1, :] + p1 * w_ref[1:2, :] + u * w_ref[2:3, :] + b_ref[...]


def _convgate_kernel(gb_ref, gc_ref, ci_ref, gch_ref, cih_ref, w_ref, b_ref, o_ref, *, per_b):
    first = (pl.program_id(0) % per_b) == 0
    g = gc_ref[...].astype(F32) * ci_ref[...].astype(F32)
    gh = gch_ref[...].astype(F32) * cih_ref[...].astype(F32)
    halo = jnp.where(first, 0.0, gh[8:16, :])
    o_ref[...] = (gb_ref[...].astype(F32) * _causal_conv3(g, halo, w_ref, b_ref)).astype(o_ref.dtype)


def _conv_gate(proj, conv_w, conv_b, tm):
    t = proj.shape[0]
    c = CONV_WIDTH
    hb = 16
    per_b = SEQ // tm
    halo_idx = lambda col: (lambda i: (jnp.maximum(i * (tm // hb) - 1, 0), col))
    return pl.pallas_call(
        functools.partial(_convgate_kernel, per_b=per_b),
        grid=(t // tm,),
        in_specs=[pl.BlockSpec((tm, c), lambda i: (i, OFF_GATE_B // c)),
                  pl.BlockSpec((tm, c), lambda i: (i, OFF_GATE_C // c)),
                  pl.BlockSpec((tm, c), lambda i: (i, OFF_CONV_IN // c)),
                  pl.BlockSpec((hb, c), halo_idx(OFF_GATE_C // c)),
                  pl.BlockSpec((hb, c), halo_idx(OFF_CONV_IN // c)),
                  pl.BlockSpec((3, c), lambda i: (0, 0)),
                  pl.BlockSpec((1, c), lambda i: (0, 0))],
        out_specs=pl.BlockSpec((tm, c), lambda i: (i, 0)),
        out_shape=jax.ShapeDtypeStruct((t, c), BF16),
        compiler_params=_params("arbitrary"),
        name="conv_gate",
    )(proj, proj, proj, proj, proj, conv_w, conv_b)


def _attn_kernel(qT_ref, k_ref, vT_ref, o_ref, m_scr, l_scr, acc_scr, *, blk):
    qi = pl.program_id(2)
    qT = qT_ref[...]
    m_scr[...] = jnp.full(m_scr.shape, -1e30, F32)
    l_scr[...] = jnp.zeros(l_scr.shape, F32)
    acc_scr[...] = jnp.zeros(acc_scr.shape, F32)

    def step(j, masked):
        start = pl.multiple_of(j * blk, blk)
        k = k_ref[pl.ds(start, blk), :]
        sT = jnp.dot(k, qT, preferred_element_type=F32)
        if masked:
            r = lax.broadcasted_iota(jnp.int32, sT.shape, 0)
            c = lax.broadcasted_iota(jnp.int32, sT.shape, 1)
            sT = jnp.where(r <= c, sT, -1e30)
        m_old = m_scr[...]
        m_new = jnp.maximum(m_old, jnp.max(sT, axis=0, keepdims=True))
        alpha = jnp.exp(m_old - m_new)
        pT = jnp.exp(sT - m_new)
        l_scr[...] = alpha * l_scr[...] + jnp.sum(pT, axis=0, keepdims=True)
        acc_scr[...] = alpha * acc_scr[...] + jnp.dot(vT_ref[j], pT.astype(BF16),
                                                      preferred_element_type=F32)
        m_scr[...] = m_new

    def body(j, carry):
        step(j, False)
        return carry

    lax.fori_loop(0, qi, body, 0)
    step(qi, True)
    o_ref[...] = (acc_scr[...] / l_scr[...]).T.astype(o_ref.dtype)


def _attention(qT, k, vT, blk):
    t = k.shape[0]
    b = t // SEQ
    nq = SEQ // blk
    return pl.pallas_call(
        functools.partial(_attn_kernel, blk=blk),
        grid=(b, MLA_HEADS, nq),
        in_specs=[pl.BlockSpec((None, HEAD_QK, blk), lambda bi, h, i: (bi * nq + i, h, 0)),
                  pl.BlockSpec((SEQ, HEAD_QK), lambda bi, h, i: (bi, h)),
                  pl.BlockSpec((nq, V_HEAD_DIM, blk), lambda bi, h, i: (bi, h, 0))],
        out_specs=pl.BlockSpec((blk, V_HEAD_DIM), lambda bi, h, i: (bi * nq + i, h)),
        out_shape=jax.ShapeDtypeStruct((t, MLA_HEADS * V_HEAD_DIM), BF16),
        scratch_shapes=[pltpu.VMEM((1, blk), F32),
                        pltpu.VMEM((1, blk), F32),
                        pltpu.VMEM((V_HEAD_DIM, blk), F32)],
        compiler_params=_params("arbitrary", "arbitrary", "arbitrary"),
        name="attention",
    )(qT, k, vT)


def _outproj_kernel(a_ref, c_ref, x_ref, gt_ref, g_ref, wa_ref, wc_ref, o_ref):
    mix = (jnp.dot(a_ref[...], wa_ref[...], preferred_element_type=F32)
           + jnp.dot(c_ref[...], wc_ref[...], preferred_element_type=F32))
    o_ref[...] = x_ref[...] + gt_ref[...] * _rms(mix, g_ref[...])


def _out_proj(attn, conv, x2, mod3, g_post, w_o_bf, tm):
    t, d = x2.shape
    ka = attn.shape[1]
    kc = conv.shape[1]
    per_b = SEQ // tm
    return pl.pallas_call(
        _outproj_kernel,
        grid=(t // tm,),
        in_specs=[pl.BlockSpec((tm, ka), lambda i: (i, 0)),
                  pl.BlockSpec((tm, kc), lambda i: (i, 0)),
                  pl.BlockSpec((tm, d), lambda i: (i, 0)),
                  pl.BlockSpec((None, 1, d), lambda i: ((i // per_b) * N_MOD + 2, 0, 0)),
                  pl.BlockSpec((1, d), lambda i: (0, 0)),
                  pl.BlockSpec((ka, d), lambda i: (0, 0)),
                  pl.BlockSpec((kc, d), lambda i: (1, 0))],
        out_specs=pl.BlockSpec((tm, d), lambda i: (i, 0)),
        out_shape=jax.ShapeDtypeStruct((t, d), F32),
        compiler_params=_params("arbitrary"),
        name="out_proj",
    )(attn, conv, x2, mod3, g_post, w_o_bf, w_o_bf)


def _ffn_kernel(x_ref, sc_ref, sh_ref, gt_ref, gpre_ref, gpost_ref, wa_ref, wg_ref,
                cwa_ref, cwg_ref, cba_ref, cbg_ref, wd_ref, o_ref,
                h_scr, acc_scr, halo_a, halo_g, *, per_b, nf):
    i = pl.program_id(0)
    f = pl.program_id(1)
    tm = x_ref.shape[0]

    @pl.when(f == 0)
    def _():
        h = _rms(x_ref[...], gpre_ref[...]) * (1.0 + sc_ref[...]) + sh_ref[...]
        h_scr[...] = h.astype(BF16)
        acc_scr[...] = jnp.zeros(acc_scr.shape, F32)

    h = h_scr[...]
    first = (i % per_b) == 0

    def branch(w_ref, cw_ref, cb_ref, halo_ref):
        u = jnp.dot(h, w_ref[...], preferred_element_type=F32)
        halo = jnp.where(first, 0.0, halo_ref[f])
        halo_ref[f] = u[tm - 8:, :]
        return _causal_conv3(u, halo, cw_ref, cb_ref)

    a = branch(wa_ref, cwa_ref, cba_ref, halo_a)
    g = branch(wg_ref, cwg_ref, cbg_ref, halo_g)
    y = (g * (1.0 / (1.0 + jnp.exp(-g))) * a).astype(BF16)
    acc_scr[...] += jnp.dot(y, wd_ref[...], preferred_element_type=F32)

    @pl.when(f == nf - 1)
    def _():
        o_ref[...] = x_ref[...] + gt_ref[...] * _rms(acc_scr[...], gpost_ref[...])


def _conv_ffn(x1, mod3, g_pre, g_post, w_up_bf, conv_w, conv_b, w_down_bf, tm, tf):
    t, d = x1.shape
    nf = D_FF // tf
    per_b = SEQ // tm
    modspec = lambda k: pl.BlockSpec((None, 1, d), lambda i, f: ((i // per_b) * N_MOD + k, 0, 0))
    return pl.pallas_call(
        functools.partial(_ffn_kernel, per_b=per_b, nf=nf),
        grid=(t // tm, nf),
        in_specs=[pl.BlockSpec((tm, d), lambda i, f: (i, 0)),
                  modspec(4), modspec(3), modspec(5),
                  pl.BlockSpec((1, d), lambda i, f: (0, 0)),
                  pl.BlockSpec((1, d), lambda i, f: (0, 0)),
                  pl.BlockSpec((d, tf), lambda i, f: (0, f)),
                  pl.BlockSpec((d, tf), lambda i, f: (0, nf + f)),
                  pl.BlockSpec((3, tf), lambda i, f: (0, f)),
                  pl.BlockSpec((3, tf), lambda i, f: (0, nf + f)),
                  pl.BlockSpec((1, tf), lambda i, f: (0, f)),
                  pl.BlockSpec((1, tf), lambda i, f: (0, nf + f)),
                  pl.BlockSpec((tf, d), lambda i, f: (f, 0))],
        out_specs=pl.BlockSpec((tm, d), lambda i, f: (i, 0)),
        out_shape=jax.ShapeDtypeStruct((t, d), F32),
        scratch_shapes=[pltpu.VMEM((tm, d), BF16),
                        pltpu.VMEM((tm, d), F32),
                        pltpu.VMEM((nf, 8, tf), F32),
                        pltpu.VMEM((nf, 8, tf), F32)],
        compiler_params=_params("arbitrary", "arbitrary"),
        name="conv_ffn",
    )(x1, mod3, mod3, mod3, g_pre, g_post, w_up_bf, w_up_bf, conv_w, conv_w,
      conv_b, conv_b, w_down_bf)


def _rot_half_cols(w):
    half = w.shape[-1] // 2
    return jnp.concatenate([-w[..., half:], w[..., :half]], axis=-1)


def _regroup_w_in(w):
    d = w.shape[0]
    c0 = Q_LORA_RANK
    c1 = c0 + KV_LORA_RANK
    c2 = c1 + QK_ROPE_DIM
    c3 = c2 + CONV_WIDTH
    c4 = c3 + CONV_WIDTH
    q_lat, kv_lat, k_rope = w[:, :c0], w[:, c0:c1], w[:, c1:c2]
    gate_b, gate_c, conv_in = w[:, c2:c3], w[:, c3:c4], w[:, c4:]
    pad = jnp.zeros((d, OFF_KV_LAT - OFF_K_ROPE - 2 * QK_ROPE_DIM), w.dtype)
    return jnp.concatenate([gate_b, gate_c, conv_in, q_lat, k_rope, _rot_half_cols(k_rope),
                            pad, kv_lat], axis=1)


def _regroup_w_uq(w):
    r = w.shape[0]
    w = w.reshape(r, MLA_HEADS, QK_NOPE_DIM + QK_ROPE_DIM)
    nope, rope = w[..., :QK_NOPE_DIM], w[..., QK_NOPE_DIM:]
    return jnp.concatenate([nope, rope, _rot_half_cols(rope)], axis=-1).reshape(r, MLA_HEADS * HEAD_QK)


def _regroup_w_ukv(w):
    r = w.shape[0]
    w = w.reshape(r, MLA_HEADS, QK_NOPE_DIM + V_HEAD_DIM)
    k = w[..., :QK_NOPE_DIM].reshape(r, MLA_HEADS * QK_NOPE_DIM)
    v = w[..., QK_NOPE_DIM:].reshape(r, MLA_HEADS * V_HEAD_DIM)
    return jnp.concatenate([k, v], axis=1)


def kernel(x, c, positions, w_ada, b_ada, g_pre_mix, g_post_mix, w_in, g_q, w_uq, g_kv, w_ukv,
           conv_w_mix, conv_b_mix, w_o, g_pre_ffn, g_post_ffn, w_up, conv_w_ffn, conv_b_ffn, w_down):
    b, s, d = x.shape
    assert (s, d) == (SEQ, D_MODEL) and w_ada.shape[0] == 1
    t = b * s
    x2 = x.reshape(t, d)
    pos2 = positions.reshape(t, 1)
    half = jnp.arange(0, QK_ROPE_DIM, 2, dtype=F32) / QK_ROPE_DIM
    inv_freq = 1.0 / (ROPE_THETA ** half)
    invf = jnp.concatenate([inv_freq, inv_freq, jnp.zeros((LANES - QK_ROPE_DIM,), F32)]).reshape(1, LANES)

    rows = 8
    c_pad = jnp.zeros((rows, d), F32).at[:b].set(c)
    x_cur = x2
    for l in range(w_ada.shape[0]):
        mod = _adaln_mod(c_pad, w_ada[l], b_ada[l].reshape(1, -1))
        mod3 = mod[:b].reshape(b * N_MOD, 1, d)

        w_in_p = _regroup_w_in(w_in[l]).astype(BF16)
        wq_p = _regroup_w_uq(w_uq[l]).astype(BF16)
        wkv_p = _regroup_w_ukv(w_ukv[l]).astype(BF16)

        proj = _in_proj(x_cur, mod3, g_pre_mix[l].reshape(1, d), w_in_p, tm=1024, tn=1536)
        qT, k, vT = _latent_qkv(proj, pos2, invf, g_q[l].reshape(1, -1), g_kv[l].reshape(1, -1),
                                wq_p, wkv_p, tm=ATT_BLK)
        conv = _conv_gate(proj, conv_w_mix[l], conv_b_mix[l].reshape(1, -1), tm=512)
        attn = _attention(qT, k, vT, blk=ATT_BLK)
        x_cur = _out_proj(attn, conv, x_cur, mod3, g_post_mix[l].reshape(1, d),
                          w_o[l].astype(BF16), tm=512)
        x_cur = _conv_ffn(x_cur, mod3, g_pre_ffn[l].reshape(1, d), g_post_ffn[l].reshape(1, d),
                          w_up[l].astype(BF16), conv_w_ffn[l], conv_b_ffn[l].reshape(1, -1),
                          w_down[l].astype(BF16), tm=512, tf=512)
    return x_cur.reshape(b, s, d)
```

```python
import functools
import math

import jax
import jax.numpy as jnp
from jax import lax
from jax.experimental import pallas as pl
from jax.experimental.pallas import tpu as pltpu

D_MODEL = 2048
SEQ = 8192
CONV_WIDTH = 1024
MLA_HEADS = 8
QK_NOPE_DIM = 128
QK_ROPE_DIM = 64
V_HEAD_DIM = 128
Q_LORA_RANK = 768
KV_LORA_RANK = 512
ROPE_THETA = 10000.0
D_FF = 5632
RMS_EPS = 1e-6
N_MOD = 6

LANES = 128
HEAD_QK = 2 * LANES
PROJ_COLS = 4608
VMEM_LIMIT = 56 * 1024 * 1024
ATT_BLK = 512
VT_ROWS = V_HEAD_DIM + 16
LOG2_E = 1.4426950408889634

OFF_GATE_B = 0
OFF_GATE_C = 1024
OFF_CONV_IN = 2048
OFF_Q_LAT = 3072
OFF_K_ROPE = 3840
OFF_KV_LAT = 4096

BF16 = jnp.bfloat16
F32 = jnp.float32


def _params(*sem):
    return pltpu.CompilerParams(dimension_semantics=sem, vmem_limit_bytes=VMEM_LIMIT)


def _rms(x, g):
    return x * lax.rsqrt(jnp.mean(x * x, axis=-1, keepdims=True) + RMS_EPS) * g


def _mod_kernel(c_ref, w_ref, b_ref, o_ref):
    c = c_ref[...]
    c_act = c * (1.0 / (1.0 + jnp.exp(-c)))
    o_ref[...] = jnp.dot(c_act.astype(BF16), w_ref[...].astype(BF16),
                         preferred_element_type=F32) + b_ref[...]


def _adaln_mod(c_pad, w_ada, b_ada):
    rows, d = c_pad.shape
    n = w_ada.shape[1]
    tn = 1536
    return pl.pallas_call(
        _mod_kernel,
        grid=(n // tn,),
        in_specs=[pl.BlockSpec((rows, d), lambda j: (0, 0)),
                  pl.BlockSpec((d, tn), lambda j: (0, j)),
                  pl.BlockSpec((1, tn), lambda j: (0, j))],
        out_specs=pl.BlockSpec((rows, tn), lambda j: (0, j)),
        out_shape=jax.ShapeDtypeStruct((rows, n), F32),
        compiler_params=_params("arbitrary"),
        name="adaln_mod",
    )(c_pad, w_ada, b_ada)


def _inproj_kernel(x_ref, sc_ref, sh_ref, g_ref, w_ref, o_ref, h_scr):
    @pl.when(pl.program_id(1) == 0)
    def _():
        h = _rms(x_ref[...], g_ref[...]) * (1.0 + sc_ref[...]) + sh_ref[...]
        h_scr[...] = h.astype(BF16)

    o_ref[...] = jnp.dot(h_scr[...], w_ref[...],
                         preferred_element_type=F32).astype(o_ref.dtype)


def _in_proj(x2, mod3, g_pre, w_in_p, tm, tn):
    t, d = x2.shape
    n = w_in_p.shape[1]
    per_b = SEQ // tm
    return pl.pallas_call(
        _inproj_kernel,
        grid=(t // tm, n // tn),
        in_specs=[pl.BlockSpec((tm, d), lambda i, j: (i, 0)),
                  pl.BlockSpec((None, 1, d), lambda i, j: ((i // per_b) * N_MOD + 1, 0, 0)),
                  pl.BlockSpec((None, 1, d), lambda i, j: ((i // per_b) * N_MOD + 0, 0, 0)),
                  pl.BlockSpec((1, d), lambda i, j: (0, 0)),
                  pl.BlockSpec((d, tn), lambda i, j: (0, j))],
        out_specs=pl.BlockSpec((tm, tn), lambda i, j: (i, j)),
        out_shape=jax.ShapeDtypeStruct((t, n), BF16),
        scratch_shapes=[pltpu.VMEM((tm, d), BF16)],
        compiler_params=_params("arbitrary", "arbitrary"),
        name="in_proj",
    )(x2, mod3, mod3, g_pre, w_in_p)


def _qkv_kernel(ql_ref, kr_ref, kvl_ref, pos_ref, invf_ref, gq_ref, gkv_ref,
                wq_ref, wkv_ref, qT_ref, k_ref, vT_ref):
    tm = ql_ref.shape[0]
    ang = pos_ref[...].astype(F32) * invf_ref[...]
    lane = lax.broadcasted_iota(jnp.int32, (tm, LANES), 1)
    rope_lane = lane < QK_ROPE_DIM
    cos = jnp.where(rope_lane, jnp.cos(ang), 0.0)
    sin = jnp.where(rope_lane, jnp.sin(ang), 0.0)

    def rope(a):
        return a * cos + pltpu.roll(a, QK_ROPE_DIM, 1) * sin

    qn = _rms(ql_ref[...].astype(F32), gq_ref[...]).astype(BF16)
    q = jnp.dot(qn, wq_ref[...], preferred_element_type=F32)
    scale = LOG2_E / math.sqrt(QK_NOPE_DIM + QK_ROPE_DIM)
    kvn = _rms(kvl_ref[...].astype(F32), gkv_ref[...]).astype(BF16)
    kv = jnp.dot(kvn, wkv_ref[...], preferred_element_type=F32)
    k_rope = rope(kr_ref[...].astype(F32)).astype(BF16)
    v_off = MLA_HEADS * QK_NOPE_DIM
    ones = jnp.ones((VT_ROWS - V_HEAD_DIM, tm), BF16)
    for h in range(MLA_HEADS):
        lo = h * HEAD_QK
        qT_ref[lo:lo + LANES, :] = (q[:, lo:lo + LANES] * scale).T.astype(BF16)
        qT_ref[lo + LANES:lo + HEAD_QK, :] = (rope(q[:, lo + LANES:lo + HEAD_QK]) * scale).T.astype(BF16)
        k_ref[:, lo:lo + LANES] = kv[:, h * LANES:(h + 1) * LANES].astype(BF16)
        k_ref[:, lo + LANES:lo + HEAD_QK] = k_rope
        vlo = h * VT_ROWS
        vT_ref[vlo:vlo + V_HEAD_DIM, :] = kv[:, v_off + h * LANES:v_off + (h + 1) * LANES].T.astype(BF16)
        vT_ref[vlo + V_HEAD_DIM:vlo + VT_ROWS, :] = ones


def _latent_qkv(proj, pos2, invf, g_q, g_kv, wq_p, wkv_p, tm):
    t = proj.shape[0]
    hq = MLA_HEADS * HEAD_QK
    hv = MLA_HEADS * V_HEAD_DIM
    hvt = MLA_HEADS * VT_ROWS
    const = lambda i: (0, 0)
    return pl.pallas_call(
        _qkv_kernel,
        grid=(t // tm,),
        in_specs=[pl.BlockSpec((tm, Q_LORA_RANK), lambda i: (i, OFF_Q_LAT // Q_LORA_RANK)),
                  pl.BlockSpec((tm, LANES), lambda i: (i, OFF_K_ROPE // LANES)),
                  pl.BlockSpec((tm, KV_LORA_RANK), lambda i: (i, OFF_KV_LAT // KV_LORA_RANK)),
                  pl.BlockSpec((tm, 1), lambda i: (i, 0)),
                  pl.BlockSpec((1, LANES), const),
                  pl.BlockSpec((1, Q_LORA_RANK), const),
                  pl.BlockSpec((1, KV_LORA_RANK), const),
                  pl.BlockSpec((Q_LORA_RANK, hq), const),
                  pl.BlockSpec((KV_LORA_RANK, 2 * hv), const)],
        out_specs=[pl.BlockSpec((None, hq, tm), lambda i: (i, 0, 0)),
                   pl.BlockSpec((tm, hq), lambda i: (i, 0)),
                   pl.BlockSpec((None, hvt, tm), lambda i: (i, 0, 0))],
        out_shape=[jax.ShapeDtypeStruct((t // tm, hq, tm), BF16),
                   jax.ShapeDtypeStruct((t, hq), BF16),
                   jax.ShapeDtypeStruct((t // tm, hvt, tm), BF16)],
        compiler_params=_params("arbitrary"),
        name="latent_qkv",
    )(proj, proj, proj, pos2, invf, g_q, g_kv, wq_p, wkv_p)


def _causal_conv3(u, halo, w_ref, b_ref):
    tm = u.shape[0]
    row = lax.broadcasted_iota(jnp.int32, u.shape, 0)
    h1 = halo[7:8, :]
    h2 = halo[6:7, :]
    p1 = jnp.where(row == 0, h1, pltpu.roll(u, 1, 0))
    p2 = jnp.where(row == 0, h2, jnp.where(row == 1, h1, pltpu.roll(u, 2, 0)))
    return p2 * w_ref[0:1, :] + p1 * w_ref[1:2, :] + u * w_ref[2:3, :] + b_ref[...]


def _convgate_kernel(gb_ref, gc_ref, ci_ref, gch_ref, cih_ref, w_ref, b_ref, o_ref, *, per_b):
    first = (pl.program_id(0) % per_b) == 0
    g = gc_ref[...].astype(F32) * ci_ref[...].astype(F32)
    gh = gch_ref[...].astype(F32) * cih_ref[...].astype(F32)
    halo = jnp.where(first, 0.0, gh[8:16, :])
    o_ref[...] = (gb_ref[...].astype(F32) * _causal_conv3(g, halo, w_ref, b_ref)).astype(o_ref.dtype)


def _conv_gate(proj, conv_w, conv_b, tm):
    t = proj.shape[0]
    c = CONV_WIDTH
    hb = 16
    per_b = SEQ // tm
    halo_idx = lambda col: (lambda i: (jnp.maximum(i * (tm // hb) - 1, 0), col))
    return pl.pallas_call(
        functools.partial(_convgate_kernel, per_b=per_b),
        grid=(t // tm,),
        in_specs=[pl.BlockSpec((tm, c), lambda i: (i, OFF_GATE_B // c)),
                  pl.BlockSpec((tm, c), lambda i: (i, OFF_GATE_C // c)),
                  pl.BlockSpec((tm, c), lambda i: (i, OFF_CONV_IN // c)),
                  pl.BlockSpec((hb, c), halo_idx(OFF_GATE_C // c)),
                  pl.BlockSpec((hb, c), halo_idx(OFF_CONV_IN // c)),
                  pl.BlockSpec((3, c), lambda i: (0, 0)),
                  pl.BlockSpec((1, c), lambda i: (0, 0))],
        out_specs=pl.BlockSpec((tm, c), lambda i: (i, 0)),
        out_shape=jax.ShapeDtypeStruct((t, c), BF16),
        compiler_params=_params("arbitrary"),
        name="conv_gate",
    )(proj, proj, proj, proj, proj, conv_w, conv_b)


def _attn_kernel(qT_ref, k_ref, vT_ref, o_ref, sa_scr, sb_scr, m_scr, acc_scr, *, blk):
    qi = pl.program_id(2)
    qT = qT_ref[...]
    m_scr[...] = jnp.full(m_scr.shape, -1e30, F32)
    acc_scr[...] = jnp.zeros(acc_scr.shape, F32)

    def scores(j):
        start = pl.multiple_of(j * blk, blk)
        return jnp.dot(k_ref[pl.ds(start, blk), :], qT, preferred_element_type=F32)

    def consume(sT, j):
        m_old = m_scr[...]
        m_new = jnp.maximum(m_old, jnp.max(sT, axis=0, keepdims=True))
        alpha = jnp.exp2(m_old - m_new)
        pT = jnp.exp2(sT - m_new).astype(BF16)
        acc_scr[...] = alpha * acc_scr[...] + jnp.dot(vT_ref[j], pT, preferred_element_type=F32)
        m_scr[...] = m_new

    def consume_diag(s_ref):
        sT = s_ref[...]
        r = lax.broadcasted_iota(jnp.int32, sT.shape, 0)
        c = lax.broadcasted_iota(jnp.int32, sT.shape, 1)
        consume(jnp.where(r <= c, sT, -1e30), qi)

    def pair(j):
        sb_scr[...] = scores(j + 1)
        consume(sa_scr[...], j)
        sa_scr[...] = scores(j + 2)
        consume(sb_scr[...], j + 1)

    sa_scr[...] = scores(0)

    def body(t, carry):
        pair(4 * t)
        pair(4 * t + 2)
        return carry

    lax.fori_loop(0, qi // 4, body, 0)
    rem = qi % 4

    @pl.when(rem >= 2)
    def _():
        pair(qi - rem)

    @pl.when(rem % 2 == 0)
    def _():
        consume_diag(sa_scr)

    @pl.when(rem % 2 == 1)
    def _():
        sb_scr[...] = scores(qi)
        consume(sa_scr[...], qi - 1)
        consume_diag(sb_scr)

    acc = acc_scr[...]
    o_ref[...] = (acc[:V_HEAD_DIM] / acc[V_HEAD_DIM:V_HEAD_DIM + 1]).T.astype(o_ref.dtype)


def _attention(qT, k, vT, blk):
    t = k.shape[0]
    b = t // SEQ
    nq = SEQ // blk
    return pl.pallas_call(
        functools.partial(_attn_kernel, blk=blk),
        grid=(b, MLA_HEADS, nq),
        in_specs=[pl.BlockSpec((None, HEAD_QK, blk), lambda bi, h, i: (bi * nq + i, h, 0)),
                  pl.BlockSpec((SEQ, HEAD_QK), lambda bi, h, i: (bi, h)),
                  pl.BlockSpec((nq, VT_ROWS, blk), lambda bi, h, i: (bi, h, 0))],
        out_specs=pl.BlockSpec((blk, V_HEAD_DIM), lambda bi, h, i: (bi * nq + i, h)),
        out_shape=jax.ShapeDtypeStruct((t, MLA_HEADS * V_HEAD_DIM), BF16),
        scratch_shapes=[pltpu.VMEM((blk, blk), F32),
                        pltpu.VMEM((blk, blk), F32),
                        pltpu.VMEM((1, blk), F32),
                        pltpu.VMEM((VT_ROWS, blk), F32)],
        compiler_params=_params("arbitrary", "arbitrary", "arbitrary"),
        name="attention",
    )(qT, k, vT)


def _outproj_kernel(a_ref, c_ref, x_ref, gt_ref, g_ref, wa_ref, wc_ref, o_ref):
    mix = (jnp.dot(a_ref[...], wa_ref[...], preferred_element_type=F32)
           + jnp.dot(c_ref[...], wc_ref[...], preferred_element_type=F32))
    o_ref[...] = x_ref[...] + gt_ref[...] * _rms(mix, g_ref[...])


def _out_proj(attn, conv, x2, mod3, g_post, w_o_bf, tm):
    t, d = x2.shape
    ka = attn.shape[1]
    kc = conv.shape[1]
    per_b = SEQ // tm
    return pl.pallas_call(
        _outproj_kernel,
        grid=(t // tm,),
        in_specs=[pl.BlockSpec((tm, ka), lambda i: (i, 0)),
                  pl.BlockSpec((tm, kc), lambda i: (i, 0)),
                  pl.BlockSpec((tm, d), lambda i: (i, 0)),
                  pl.BlockSpec((None, 1, d), lambda i: ((i // per_b) * N_MOD + 2, 0, 0)),
                  pl.BlockSpec((1, d), lambda i: (0, 0)),
                  pl.BlockSpec((ka, d), lambda i: (0, 0)),
                  pl.BlockSpec((kc, d), lambda i: (1, 0))],
        out_specs=pl.BlockSpec((tm, d), lambda i: (i, 0)),
        out_shape=jax.ShapeDtypeStruct((t, d), F32),
        compiler_params=_params("arbitrary"),
        name="out_proj",
    )(attn, conv, x2, mod3, g_post, w_o_bf, w_o_bf)


def _ffn_kernel(x_ref, sc_ref, sh_ref, gt_ref, gpre_ref, gpost_ref, wa_ref, wg_ref,
                cwa_ref, cwg_ref, cba_ref, cbg_ref, wd_ref, o_ref,
                h_scr, acc_scr, halo_a, halo_g, y0_scr, y1_scr, *, per_b, nf, sub):
    i = pl.program_id(0)
    f = pl.program_id(1)
    tm = x_ref.shape[0]
    tf = wa_ref.shape[1]
    first = (i % per_b) == 0

    def branch(w_ref, cw_ref, cb_ref, halo_ref, cols):
        u = jnp.dot(h_scr[...], w_ref[:, cols], preferred_element_type=F32)
        halo = jnp.where(first, 0.0, halo_ref[f, :, cols])
        halo_ref[f, :, cols] = u[tm - 8:, :]
        return _causal_conv3(u, halo, cw_ref.at[:, cols], cb_ref.at[:, cols])

    def up(y_out):
        for s in range(0, tf, sub):
            cols = slice(s, s + sub)
            a = branch(wa_ref, cwa_ref, cba_ref, halo_a, cols)
            g = branch(wg_ref, cwg_ref, cbg_ref, halo_g, cols)
            y_out[:, cols] = (g * (1.0 / (1.0 + jnp.exp(-g))) * a).astype(BF16)

    def down(y_in):
        acc_scr[...] += jnp.dot(y_in[...], wd_ref[...], preferred_element_type=F32)

    @pl.when(f == 0)
    def _():
        h = _rms(x_ref[...], gpre_ref[...]) * (1.0 + sc_ref[...]) + sh_ref[...]
        h_scr[...] = h.astype(BF16)
        acc_scr[...] = jnp.zeros(acc_scr.shape, F32)
        up(y0_scr)

    interior = jnp.logical_and(f > 0, f < nf)

    @pl.when(jnp.logical_and(interior, f % 2 == 1))
    def _():
        up(y1_scr)
        down(y0_scr)

    @pl.when(jnp.logical_and(interior, f % 2 == 0))
    def _():
        up(y0_scr)
        down(y1_scr)

    @pl.when(f == nf)
    def _():
        down(y1_scr if (nf - 1) % 2 == 1 else y0_scr)
        o_ref[...] = x_ref[...] + gt_ref[...] * _rms(acc_scr[...], gpost_ref[...])


def _conv_ffn(x1, mod3, g_pre, g_post, w_up_bf, conv_w, conv_b, w_down_bf, tm, tf):
    t, d = x1.shape
    nf = D_FF // tf
    per_b = SEQ // tm
    modspec = lambda k: pl.BlockSpec((None, 1, d), lambda i, f: ((i // per_b) * N_MOD + k, 0, 0))
    up_f = lambda f: jnp.minimum(f, nf - 1)
    dn_f = lambda f: jnp.maximum(f - 1, 0)
    return pl.pallas_call(
        functools.partial(_ffn_kernel, per_b=per_b, nf=nf, sub=2 * LANES),
        grid=(t // tm, nf + 1),
        in_specs=[pl.BlockSpec((tm, d), lambda i, f: (i, 0)),
                  modspec(4), modspec(3), modspec(5),
                  pl.BlockSpec((1, d), lambda i, f: (0, 0)),
                  pl.BlockSpec((1, d), lambda i, f: (0, 0)),
                  pl.BlockSpec((d, tf), lambda i, f: (0, up_f(f))),
                  pl.BlockSpec((d, tf), lambda i, f: (0, nf + up_f(f))),
                  pl.BlockSpec((3, tf), lambda i, f: (0, up_f(f))),
                  pl.BlockSpec((3, tf), lambda i, f: (0, nf + up_f(f))),
                  pl.BlockSpec((1, tf), lambda i, f: (0, up_f(f))),
                  pl.BlockSpec((1, tf), lambda i, f: (0, nf + up_f(f))),
                  pl.BlockSpec((tf, d), lambda i, f: (dn_f(f), 0))],
        out_specs=pl.BlockSpec((tm, d), lambda i, f: (i, 0)),
        out_shape=jax.ShapeDtypeStruct((t, d), F32),
        scratch_shapes=[pltpu.VMEM((tm, d), BF16),
                        pltpu.VMEM((tm, d), F32),
                        pltpu.VMEM((nf, 8, tf), F32),
                        pltpu.VMEM((nf, 8, tf), F32),
                        pltpu.VMEM((tm, tf), BF16),
                        pltpu.VMEM((tm, tf), BF16)],
        compiler_params=_params("arbitrary", "arbitrary"),
        name="conv_ffn",
    )(x1, mod3, mod3, mod3, g_pre, g_post, w_up_bf, w_up_bf, conv_w, conv_w,
      conv_b, conv_b, w_down_bf)


def _rot_half_cols(w):
    half = w.shape[-1] // 2
    return jnp.concatenate([-w[..., half:], w[..., :half]], axis=-1)


def _regroup_w_in(w):
    d = w.shape[0]
    c0 = Q_LORA_RANK
    c1 = c0 + KV_LORA_RANK
    c2 = c1 + QK_ROPE_DIM
    c3 = c2 + CONV_WIDTH
    c4 = c3 + CONV_WIDTH
    q_lat, kv_lat, k_rope = w[:, :c0], w[:, c0:c1], w[:, c1:c2]
    gate_b, gate_c, conv_in = w[:, c2:c3], w[:, c3:c4], w[:, c4:]
    pad = jnp.zeros((d, OFF_KV_LAT - OFF_K_ROPE - 2 * QK_ROPE_DIM), w.dtype)
    return jnp.concatenate([gate_b, gate_c, conv_in, q_lat, k_rope, _rot_half_cols(k_rope),
                            pad, kv_lat], axis=1)


def _regroup_w_uq(w):
    r = w.shape[0]
    w = w.reshape(r, MLA_HEADS, QK_NOPE_DIM + QK_ROPE_DIM)
    nope, rope = w[..., :QK_NOPE_DIM], w[..., QK_NOPE_DIM:]
    return jnp.concatenate([nope, rope, _rot_half_cols(rope)], axis=-1).reshape(r, MLA_HEADS * HEAD_QK)


def _regroup_w_ukv(w):
    r = w.shape[0]
    w = w.reshape(r, MLA_HEADS, QK_NOPE_DIM + V_HEAD_DIM)
    k = w[..., :QK_NOPE_DIM].reshape(r, MLA_HEADS * QK_NOPE_DIM)
    v = w[..., QK_NOPE_DIM:].reshape(r, MLA_HEADS * V_HEAD_DIM)
    return jnp.concatenate([k, v], axis=1)


def kernel(x, c, positions, w_ada, b_ada, g_pre_mix, g_post_mix, w_in, g_q, w_uq, g_kv, w_ukv,
           conv_w_mix, conv_b_mix, w_o, g_pre_ffn, g_post_ffn, w_up, conv_w_ffn, conv_b_ffn, w_down):
    b, s, d = x.shape
    assert (s, d) == (SEQ, D_MODEL) and w_ada.shape[0] == 1
    t = b * s
    x2 = x.reshape(t, d)
    pos2 = positions.reshape(t, 1)
    half = jnp.arange(0, QK_ROPE_DIM, 2, dtype=F32) / QK_ROPE_DIM
    inv_freq = 1.0 / (ROPE_THETA ** half)
    invf = jnp.concatenate([inv_freq, inv_freq, jnp.zeros((LANES - QK_ROPE_DIM,), F32)]).reshape(1, LANES)

    rows = 8
    c_pad = jnp.zeros((rows, d), F32).at[:b].set(c)
    x_cur = x2
    for l in range(w_ada.shape[0]):
        mod = _adaln_mod(c_pad, w_ada[l], b_ada[l].reshape(1, -1))
        mod3 = mod[:b].reshape(b * N_MOD, 1, d)

        w_in_p = _regroup_w_in(w_in[l]).astype(BF16)
        wq_p = _regroup_w_uq(w_uq[l]).astype(BF16)
        wkv_p = _regroup_w_ukv(w_ukv[l]).astype(BF16)

        proj = _in_proj(x_cur, mod3, g_pre_mix[l].reshape(1, d), w_in_p, tm=1024, tn=1536)
        qT, k, vT = _latent_qkv(proj, pos2, invf, g_q[l].reshape(1, -1), g_kv[l].reshape(1, -1),
                                wq_p, wkv_p, tm=ATT_BLK)
        conv = _conv_gate(proj, conv_w_mix[l], conv_b_mix[l].reshape(1, -1), tm=512)
        attn = _attention(qT, k, vT, blk=ATT_BLK)
        x_cur = _out_proj(attn, conv, x_cur, mod3, g_post_mix[l].reshape(1, d),
                          w_o[l].astype(BF16), tm=512)
        x_cur = _conv_ffn(x_cur, mod3, g_pre_ffn[l].reshape(1, d), g_post_ffn[l].reshape(1, d),
                          w_up[l].astype(BF16), conv_w_ffn[l], conv_b_ffn[l].reshape(1, -1),
                          w_down[l].astype(BF16), tm=512, tf=512)
    return x_cur.reshape(b, s, d)
```

```python
import functools
import math

import jax
import jax.numpy as jnp
from jax import lax
from jax.experimental import pallas as pl
from jax.experimental.pallas import tpu as pltpu

D_MODEL = 2048
SEQ = 8192
CONV_WIDTH = 1024
MLA_HEADS = 8
QK_NOPE_DIM = 128
QK_ROPE_DIM = 64
V_HEAD_DIM = 128
Q_LORA_RANK = 768
KV_LORA_RANK = 512
ROPE_THETA = 10000.0
D_FF = 5632
RMS_EPS = 1e-6
N_MOD = 6

LANES = 128
HEAD_QK = 2 * LANES
PROJ_COLS = 4608
VMEM_LIMIT = 56 * 1024 * 1024
ATT_BLK = 512
VT_ROWS = V_HEAD_DIM + 16
LOG2_E = 1.4426950408889634
IN_PROJ_TN = 1536
FFN_TF = 512

OFF_GATE_B = 0
OFF_GATE_C = 1024
OFF_CONV_IN = 2048
OFF_Q_LAT = 3072
OFF_K_ROPE = 3840
OFF_KV_LAT = 4096

BF16 = jnp.bfloat16
F32 = jnp.float32


def _params(*sem):
    return pltpu.CompilerParams(dimension_semantics=sem, vmem_limit_bytes=VMEM_LIMIT)


def _rms(x, g):
    return x * lax.rsqrt(jnp.mean(x * x, axis=-1, keepdims=True) + RMS_EPS) * g


def _mod_kernel(c_ref, w_ref, b_ref, o_ref):
    c = c_ref[...]
    c_act = c * (1.0 / (1.0 + jnp.exp(-c)))
    o_ref[...] = jnp.dot(c_act.astype(BF16), w_ref[...].astype(BF16),
                         preferred_element_type=F32) + b_ref[...]


def _adaln_mod(c_pad, w_ada, b_ada):
    rows, d = c_pad.shape
    n = w_ada.shape[1]
    tn = 1536
    return pl.pallas_call(
        _mod_kernel,
        grid=(n // tn,),
        in_specs=[pl.BlockSpec((rows, d), lambda j: (0, 0)),
                  pl.BlockSpec((d, tn), lambda j: (0, j)),
                  pl.BlockSpec((1, tn), lambda j: (0, j))],
        out_specs=pl.BlockSpec((rows, tn), lambda j: (0, j)),
        out_shape=jax.ShapeDtypeStruct((rows, n), F32),
        compiler_params=_params("arbitrary"),
        name="adaln_mod",
    )(c_pad, w_ada, b_ada)


def _inproj_kernel(x_ref, sc_ref, sh_ref, g_ref, w_ref, o_ref, h_scr):
    @pl.when(pl.program_id(1) == 0)
    def _():
        h = _rms(x_ref[...], g_ref[...]) * (1.0 + sc_ref[...]) + sh_ref[...]
        h_scr[...] = h.astype(BF16)

    o_ref[...] = jnp.dot(h_scr[...], w_ref[...],
                         preferred_element_type=F32).astype(o_ref.dtype)


def _in_proj(x2, mod3, g_pre, w_blk, tm):
    t, d = x2.shape
    nb, _, tn = w_blk.shape
    n = nb * tn
    per_b = SEQ // tm
    return pl.pallas_call(
        _inproj_kernel,
        grid=(t // tm, n // tn),
        in_specs=[pl.BlockSpec((tm, d), lambda i, j: (i, 0)),
                  pl.BlockSpec((None, 1, d), lambda i, j: ((i // per_b) * N_MOD + 1, 0, 0)),
                  pl.BlockSpec((None, 1, d), lambda i, j: ((i // per_b) * N_MOD + 0, 0, 0)),
                  pl.BlockSpec((1, d), lambda i, j: (0, 0)),
                  pl.BlockSpec((None, d, tn), lambda i, j: (j, 0, 0))],
        out_specs=pl.BlockSpec((tm, tn), lambda i, j: (i, j)),
        out_shape=jax.ShapeDtypeStruct((t, n), BF16),
        scratch_shapes=[pltpu.VMEM((tm, d), BF16)],
        compiler_params=_params("arbitrary", "arbitrary"),
        name="in_proj",
    )(x2, mod3, mod3, g_pre, w_blk)


def _qkv_kernel(ql_ref, kr_ref, kvl_ref, pos_ref, invf_ref, gq_ref, gkv_ref,
                wq_ref, wkv_ref, qT_ref, k_ref, vT_ref):
    tm = ql_ref.shape[0]
    ang = pos_ref[...].astype(F32) * invf_ref[...]
    lane = lax.broadcasted_iota(jnp.int32, (tm, LANES), 1)
    rope_lane = lane < QK_ROPE_DIM
    cos = jnp.where(rope_lane, jnp.cos(ang), 0.0)
    sin = jnp.where(rope_lane, jnp.sin(ang), 0.0)

    def rope(a):
        return a * cos + pltpu.roll(a, QK_ROPE_DIM, 1) * sin

    qn = _rms(ql_ref[...].astype(F32), gq_ref[...]).astype(BF16)
    q = jnp.dot(qn, wq_ref[...], preferred_element_type=F32)
    scale = LOG2_E / math.sqrt(QK_NOPE_DIM + QK_ROPE_DIM)
    kvn = _rms(kvl_ref[...].astype(F32), gkv_ref[...]).astype(BF16)
    kv = jnp.dot(kvn, wkv_ref[...], preferred_element_type=F32)
    k_rope = rope(kr_ref[...].astype(F32)).astype(BF16)
    v_off = MLA_HEADS * QK_NOPE_DIM
    ones = jnp.ones((VT_ROWS - V_HEAD_DIM, tm), BF16)
    for h in range(MLA_HEADS):
        lo = h * HEAD_QK
        qT_ref[lo:lo + LANES, :] = (q[:, lo:lo + LANES] * scale).T.astype(BF16)
        qT_ref[lo + LANES:lo + HEAD_QK, :] = (rope(q[:, lo + LANES:lo + HEAD_QK]) * scale).T.astype(BF16)
        k_ref[:, lo:lo + LANES] = kv[:, h * LANES:(h + 1) * LANES].astype(BF16)
        k_ref[:, lo + LANES:lo + HEAD_QK] = k_rope
        vlo = h * VT_ROWS
        vT_ref[vlo:vlo + V_HEAD_DIM, :] = kv[:, v_off + h * LANES:v_off + (h + 1) * LANES].T.astype(BF16)
        vT_ref[vlo + V_HEAD_DIM:vlo + VT_ROWS, :] = ones


def _latent_qkv(proj, pos2, invf, g_q, g_kv, wq_p, wkv_p, tm):
    t = proj.shape[0]
    hq = MLA_HEADS * HEAD_QK
    hv = MLA_HEADS * V_HEAD_DIM
    hvt = MLA_HEADS * VT_ROWS
    const = lambda i: (0, 0)
    return pl.pallas_call(
        _qkv_kernel,
        grid=(t // tm,),
        in_specs=[pl.BlockSpec((tm, Q_LORA_RANK), lambda i: (i, OFF_Q_LAT // Q_LORA_RANK)),
                  pl.BlockSpec((tm, LANES), lambda i: (i, OFF_K_ROPE // LANES)),
                  pl.BlockSpec((tm, KV_LORA_RANK), lambda i: (i, OFF_KV_LAT // KV_LORA_RANK)),
                  pl.BlockSpec((tm, 1), lambda i: (i, 0)),
                  pl.BlockSpec((1, LANES), const),
                  pl.BlockSpec((1, Q_LORA_RANK), const),
                  pl.BlockSpec((1, KV_LORA_RANK), const),
                  pl.BlockSpec((Q_LORA_RANK, hq), const),
                  pl.BlockSpec((KV_LORA_RANK, 2 * hv), const)],
        out_specs=[pl.BlockSpec((None, hq, tm), lambda i: (i, 0, 0)),
                   pl.BlockSpec((tm, hq), lambda i: (i, 0)),
                   pl.BlockSpec((None, hvt, tm), lambda i: (i, 0, 0))],
        out_shape=[jax.ShapeDtypeStruct((t // tm, hq, tm), BF16),
                   jax.ShapeDtypeStruct((t, hq), BF16),
                   jax.ShapeDtypeStruct((t // tm, hvt, tm), BF16)],
        compiler_params=_params("arbitrary"),
        name="latent_qkv",
    )(proj, proj, proj, pos2, invf, g_q, g_kv, wq_p, wkv_p)


def _causal_conv3(u, halo, w_ref, b_ref):
    w0, w1, w2, b = w_ref[0:1, :], w_ref[1:2, :], w_ref[2:3, :], b_ref[...]

    def taps(p2, p1, p0):
        return p2 * w0 + p1 * w1 + p0 * w2 + b

    body = taps(pltpu.roll(u, 2, 0), pltpu.roll(u, 1, 0), u)
    head = jnp.concatenate([halo, u[0:8, :]], axis=0)
    first = taps(pltpu.roll(head, 2, 0), pltpu.roll(head, 1, 0), head)[8:16, :]
    return jnp.concatenate([first, body[8:, :]], axis=0)


def _convgate_kernel(gb_ref, gc_ref, ci_ref, gch_ref, cih_ref, w_ref, b_ref, o_ref, *, per_b):
    first = (pl.program_id(0) % per_b) == 0
    g = gc_ref[...].astype(F32) * ci_ref[...].astype(F32)
    gh = gch_ref[...].astype(F32) * cih_ref[...].astype(F32)
    halo = jnp.where(first, 0.0, gh[8:16, :])
    o_ref[...] = (gb_ref[...].astype(F32) * _causal_conv3(g, halo, w_ref, b_ref)).astype(o_ref.dtype)


def _conv_gate(proj, conv_w, conv_b, tm):
    t = proj.shape[0]
    c = CONV_WIDTH
    hb = 16
    per_b = SEQ // tm
    halo_idx = lambda col: (lambda i: (jnp.maximum(i * (tm // hb) - 1, 0), col))
    return pl.pallas_call(
        functools.partial(_convgate_kernel, per_b=per_b),
        grid=(t // tm,),
        in_specs=[pl.BlockSpec((tm, c), lambda i: (i, OFF_GATE_B // c)),
                  pl.BlockSpec((tm, c), lambda i: (i, OFF_GATE_C // c)),
                  pl.BlockSpec((tm, c), lambda i: (i, OFF_CONV_IN // c)),
                  pl.BlockSpec((hb, c), halo_idx(OFF_GATE_C // c)),
                  pl.BlockSpec((hb, c), halo_idx(OFF_CONV_IN // c)),
                  pl.BlockSpec((3, c), lambda i: (0, 0)),
                  pl.BlockSpec((1, c), lambda i: (0, 0))],
        out_specs=pl.BlockSpec((tm, c), lambda i: (i, 0)),
        out_shape=jax.ShapeDtypeStruct((t, c), BF16),
        compiler_params=_params("arbitrary"),
        name="conv_gate",
    )(proj, proj, proj, proj, proj, conv_w, conv_b)


def _attn_kernel(qT_ref, k_ref, vT_ref, o_ref, sa_scr, sb_scr, m_scr, acc_scr, *, blk):
    qi = pl.program_id(2)
    qT = qT_ref[...]
    m_scr[...] = jnp.full(m_scr.shape, -1e30, F32)
    acc_scr[...] = jnp.zeros(acc_scr.shape, F32)

    def scores(j):
        start = pl.multiple_of(j * blk, blk)
        return jnp.dot(k_ref[pl.ds(start, blk), :], qT, preferred_element_type=F32)

    def consume(sT, j):
        m_old = m_scr[...]
        m_new = jnp.maximum(m_old, jnp.max(sT, axis=0, keepdims=True))
        alpha = jnp.exp2(m_old - m_new)
        pT = jnp.exp2(sT - m_new).astype(BF16)
        acc_scr[...] = alpha * acc_scr[...] + jnp.dot(vT_ref[j], pT, preferred_element_type=F32)
        m_scr[...] = m_new

    def consume_diag(s_ref):
        sT = s_ref[...]
        r = lax.broadcasted_iota(jnp.int32, sT.shape, 0)
        c = lax.broadcasted_iota(jnp.int32, sT.shape, 1)
        consume(jnp.where(r <= c, sT, -1e30), qi)

    def pair(j):
        sb_scr[...] = scores(j + 1)
        consume(sa_scr[...], j)
        sa_scr[...] = scores(j + 2)
        consume(sb_scr[...], j + 1)

    sa_scr[...] = scores(0)

    def body(t, carry):
        pair(4 * t)
        pair(4 * t + 2)
        return carry

    lax.fori_loop(0, qi // 4, body, 0)
    rem = qi % 4

    @pl.when(rem >= 2)
    def _():
        pair(qi - rem)

    @pl.when(rem % 2 == 0)
    def _():
        consume_diag(sa_scr)

    @pl.when(rem % 2 == 1)
    def _():
        sb_scr[...] = scores(qi)
        consume(sa_scr[...], qi - 1)
        consume_diag(sb_scr)

    acc = acc_scr[...]
    o_ref[...] = (acc[:V_HEAD_DIM] / acc[V_HEAD_DIM:V_HEAD_DIM + 1]).T.astype(o_ref.dtype)


def _attention(qT, k, vT, blk):
    t = k.shape[0]
    b = t // SEQ
    nq = SEQ // blk
    return pl.pallas_call(
        functools.partial(_attn_kernel, blk=blk),
        grid=(b, MLA_HEADS, nq),
        in_specs=[pl.BlockSpec((None, HEAD_QK, blk), lambda bi, h, i: (bi * nq + i, h, 0)),
                  pl.BlockSpec((SEQ, HEAD_QK), lambda bi, h, i: (bi, h)),
                  pl.BlockSpec((nq, VT_ROWS, blk), lambda bi, h, i: (bi, h, 0))],
        out_specs=pl.BlockSpec((blk, V_HEAD_DIM), lambda bi, h, i: (bi * nq + i, h)),
        out_shape=jax.ShapeDtypeStruct((t, MLA_HEADS * V_HEAD_DIM), BF16),
        scratch_shapes=[pltpu.VMEM((blk, blk), F32),
                        pltpu.VMEM((blk, blk), F32),
                        pltpu.VMEM((1, blk), F32),
                        pltpu.VMEM((VT_ROWS, blk), F32)],
        compiler_params=_params("arbitrary", "arbitrary", "arbitrary"),
        name="attention",
    )(qT, k, vT)


def _outproj_kernel(a_ref, c_ref, x_ref, gt_ref, g_ref, wa_ref, wc_ref, o_ref):
    mix = (jnp.dot(a_ref[...], wa_ref[...], preferred_element_type=F32)
           + jnp.dot(c_ref[...], wc_ref[...], preferred_element_type=F32))
    o_ref[...] = x_ref[...] + gt_ref[...] * _rms(mix, g_ref[...])


def _out_proj(attn, conv, x2, mod3, g_post, w_o_bf, tm):
    t, d = x2.shape
    ka = attn.shape[1]
    kc = conv.shape[1]
    per_b = SEQ // tm
    return pl.pallas_call(
        _outproj_kernel,
        grid=(t // tm,),
        in_specs=[pl.BlockSpec((tm, ka), lambda i: (i, 0)),
                  pl.BlockSpec((tm, kc), lambda i: (i, 0)),
                  pl.BlockSpec((tm, d), lambda i: (i, 0)),
                  pl.BlockSpec((None, 1, d), lambda i: ((i // per_b) * N_MOD + 2, 0, 0)),
                  pl.BlockSpec((1, d), lambda i: (0, 0)),
                  pl.BlockSpec((ka, d), lambda i: (0, 0)),
                  pl.BlockSpec((kc, d), lambda i: (1, 0))],
        out_specs=pl.BlockSpec((tm, d), lambda i: (i, 0)),
        out_shape=jax.ShapeDtypeStruct((t, d), F32),
        compiler_params=_params("arbitrary"),
        name="out_proj",
    )(attn, conv, x2, mod3, g_post, w_o_bf, w_o_bf)


def _ffn_kernel(x_ref, sc_ref, sh_ref, gt_ref, gpre_ref, gpost_ref, wa_ref, wg_ref,
                cwa_ref, cwg_ref, cba_ref, cbg_ref, wd_ref, o_ref,
                h_scr, acc_scr, halo_a, halo_g, y0_scr, y1_scr, *, per_b, nf, sub):
    i = pl.program_id(0)
    f = pl.program_id(1)
    tm = x_ref.shape[0]
    tf = wa_ref.shape[1]
    first = (i % per_b) == 0

    def branch(w_ref, cw_ref, cb_ref, halo_ref, cols):
        u = jnp.dot(h_scr[...], w_ref[:, cols], preferred_element_type=F32)
        halo = jnp.where(first, 0.0, halo_ref[f, :, cols])
        halo_ref[f, :, cols] = u[tm - 8:, :]
        return _causal_conv3(u, halo, cw_ref.at[:, cols], cb_ref.at[:, cols])

    def up(y_out):
        for s in range(0, tf, sub):
            cols = slice(s, s + sub)
            a = branch(wa_ref, cwa_ref, cba_ref, halo_a, cols)
            g = branch(wg_ref, cwg_ref, cbg_ref, halo_g, cols)
            y_out[:, cols] = (g * (1.0 / (1.0 + jnp.exp(-g))) * a).astype(BF16)

    def down(y_in):
        acc_scr[...] += jnp.dot(y_in[...], wd_ref[...], preferred_element_type=F32)

    @pl.when(f == 0)
    def _():
        h = _rms(x_ref[...], gpre_ref[...]) * (1.0 + sc_ref[...]) + sh_ref[...]
        h_scr[...] = h.astype(BF16)
        acc_scr[...] = jnp.zeros(acc_scr.shape, F32)
        up(y0_scr)

    interior = jnp.logical_and(f > 0, f < nf)

    @pl.when(jnp.logical_and(interior, f % 2 == 1))
    def _():
        up(y1_scr)
        down(y0_scr)

    @pl.when(jnp.logical_and(interior, f % 2 == 0))
    def _():
        up(y0_scr)
        down(y1_scr)

    @pl.when(f == nf)
    def _():
        down(y1_scr if (nf - 1) % 2 == 1 else y0_scr)
        o_ref[...] = x_ref[...] + gt_ref[...] * _rms(acc_scr[...], gpost_ref[...])


def _conv_ffn(x1, mod3, g_pre, g_post, w_up_blk, conv_w, conv_b, w_down_bf, tm):
    t, d = x1.shape
    tf = w_up_blk.shape[2]
    nf = D_FF // tf
    per_b = SEQ // tm
    modspec = lambda k: pl.BlockSpec((None, 1, d), lambda i, f: ((i // per_b) * N_MOD + k, 0, 0))
    up_f = lambda f: jnp.minimum(f, nf - 1)
    dn_f = lambda f: jnp.maximum(f - 1, 0)
    return pl.pallas_call(
        functools.partial(_ffn_kernel, per_b=per_b, nf=nf, sub=2 * LANES),
        grid=(t // tm, nf + 1),
        in_specs=[pl.BlockSpec((tm, d), lambda i, f: (i, 0)),
                  modspec(4), modspec(3), modspec(5),
                  pl.BlockSpec((1, d), lambda i, f: (0, 0)),
                  pl.BlockSpec((1, d), lambda i, f: (0, 0)),
                  pl.BlockSpec((None, d, tf), lambda i, f: (up_f(f), 0, 0)),
                  pl.BlockSpec((None, d, tf), lambda i, f: (nf + up_f(f), 0, 0)),
                  pl.BlockSpec((3, tf), lambda i, f: (0, up_f(f))),
                  pl.BlockSpec((3, tf), lambda i, f: (0, nf + up_f(f))),
                  pl.BlockSpec((1, tf), lambda i, f: (0, up_f(f))),
                  pl.BlockSpec((1, tf), lambda i, f: (0, nf + up_f(f))),
                  pl.BlockSpec((tf, d), lambda i, f: (dn_f(f), 0))],
        out_specs=pl.BlockSpec((tm, d), lambda i, f: (i, 0)),
        out_shape=jax.ShapeDtypeStruct((t, d), F32),
        scratch_shapes=[pltpu.VMEM((tm, d), BF16),
                        pltpu.VMEM((tm, d), F32),
                        pltpu.VMEM((nf, 8, tf), F32),
                        pltpu.VMEM((nf, 8, tf), F32),
                        pltpu.VMEM((tm, tf), BF16),
                        pltpu.VMEM((tm, tf), BF16)],
        compiler_params=_params("arbitrary", "arbitrary"),
        name="conv_ffn",
    )(x1, mod3, mod3, mod3, g_pre, g_post, w_up_blk, w_up_blk, conv_w, conv_w,
      conv_b, conv_b, w_down_bf)


def _rot_half_cols(w):
    half = w.shape[-1] // 2
    return jnp.concatenate([-w[..., half:], w[..., :half]], axis=-1)


def _regroup_w_in(w):
    d = w.shape[0]
    c0 = Q_LORA_RANK
    c1 = c0 + KV_LORA_RANK
    c2 = c1 + QK_ROPE_DIM
    c3 = c2 + CONV_WIDTH
    c4 = c3 + CONV_WIDTH
    q_lat, kv_lat, k_rope = w[:, :c0], w[:, c0:c1], w[:, c1:c2]
    gate_b, gate_c, conv_in = w[:, c2:c3], w[:, c3:c4], w[:, c4:]
    pad = jnp.zeros((d, OFF_KV_LAT - OFF_K_ROPE - 2 * QK_ROPE_DIM), w.dtype)
    return jnp.concatenate([gate_b, gate_c, conv_in, q_lat, k_rope, _rot_half_cols(k_rope),
                            pad, kv_lat], axis=1)


def _block_cols(w, tn):
    k, n = w.shape
    return w.reshape(k, n // tn, tn).transpose(1, 0, 2)


def _regroup_w_uq(w):
    r = w.shape[0]
    w = w.reshape(r, MLA_HEADS, QK_NOPE_DIM + QK_ROPE_DIM)
    nope, rope = w[..., :QK_NOPE_DIM], w[..., QK_NOPE_DIM:]
    return jnp.concatenate([nope, rope, _rot_half_cols(rope)], axis=-1).reshape(r, MLA_HEADS * HEAD_QK)


def _regroup_w_ukv(w):
    r = w.shape[0]
    w = w.reshape(r, MLA_HEADS, QK_NOPE_DIM + V_HEAD_DIM)
    k = w[..., :QK_NOPE_DIM].reshape(r, MLA_HEADS * QK_NOPE_DIM)
    v = w[..., QK_NOPE_DIM:].reshape(r, MLA_HEADS * V_HEAD_DIM)
    return jnp.concatenate([k, v], axis=1)


def kernel(x, c, positions, w_ada, b_ada, g_pre_mix, g_post_mix, w_in, g_q, w_uq, g_kv, w_ukv,
           conv_w_mix, conv_b_mix, w_o, g_pre_ffn, g_post_ffn, w_up, conv_w_ffn, conv_b_ffn, w_down):
    b, s, d = x.shape
    assert (s, d) == (SEQ, D_MODEL) and w_ada.shape[0] == 1
    t = b * s
    x2 = x.reshape(t, d)
    pos2 = positions.reshape(t, 1)
    half = jnp.arange(0, QK_ROPE_DIM, 2, dtype=F32) / QK_ROPE_DIM
    inv_freq = 1.0 / (ROPE_THETA ** half)
    invf = jnp.concatenate([inv_freq, inv_freq, jnp.zeros((LANES - QK_ROPE_DIM,), F32)]).reshape(1, LANES)

    rows = 8
    c_pad = jnp.zeros((rows, d), F32).at[:b].set(c)
    x_cur = x2
    for l in range(w_ada.shape[0]):
        mod = _adaln_mod(c_pad, w_ada[l], b_ada[l].reshape(1, -1))
        mod3 = mod[:b].reshape(b * N_MOD, 1, d)

        w_in_blk = _block_cols(_regroup_w_in(w_in[l].astype(BF16)), IN_PROJ_TN)
        wq_p = _regroup_w_uq(w_uq[l]).astype(BF16)
        wkv_p = _regroup_w_ukv(w_ukv[l]).astype(BF16)

        proj = _in_proj(x_cur, mod3, g_pre_mix[l].reshape(1, d), w_in_blk, tm=1024)
        qT, k, vT = _latent_qkv(proj, pos2, invf, g_q[l].reshape(1, -1), g_kv[l].reshape(1, -1),
                                wq_p, wkv_p, tm=ATT_BLK)
        conv = _conv_gate(proj, conv_w_mix[l], conv_b_mix[l].reshape(1, -1), tm=512)
        attn = _attention(qT, k, vT, blk=ATT_BLK)
        x_cur = _out_proj(attn, conv, x_cur, mod3, g_post_mix[l].reshape(1, d),
                          w_o[l].astype(BF16), tm=512)
        x_cur = _conv_ffn(x_cur, mod3, g_pre_ffn[l].reshape(1, d), g_post_ffn[l].reshape(1, d),
                          _block_cols(w_up[l], FFN_TF).astype(BF16), conv_w_ffn[l],
                          conv_b_ffn[l].reshape(1, -1), w_down[l].astype(BF16), tm=512)
    return x_cur.reshape(b, s, d)
```

```python
import functools
import math

import jax
import jax.numpy as jnp
from jax import lax
from jax.experimental import pallas as pl
from jax.experimental.pallas import tpu as pltpu

D_MODEL = 2048
SEQ = 8192
CONV_WIDTH = 1024
MLA_HEADS = 8
QK_NOPE_DIM = 128
QK_ROPE_DIM = 64
V_HEAD_DIM = 128
Q_LORA_RANK = 768
KV_LORA_RANK = 512
ROPE_THETA = 10000.0
D_FF = 5632
RMS_EPS = 1e-6
N_MOD = 6

LANES = 128
HEAD_QK = 2 * LANES
PROJ_COLS = 4608
VMEM_LIMIT = 56 * 1024 * 1024
ATT_BLK = 512
VT_ROWS = V_HEAD_DIM + 16
LOG2_E = 1.4426950408889634
IN_PROJ_TN = 1536
FFN_TF = 512

OFF_GATE_B = 0
OFF_GATE_C = 1024
OFF_CONV_IN = 2048
OFF_Q_LAT = 3072
OFF_K_ROPE = 3840
OFF_KV_LAT = 4096

BF16 = jnp.bfloat16
F32 = jnp.float32


def _params(*sem):
    return pltpu.CompilerParams(dimension_semantics=sem, vmem_limit_bytes=VMEM_LIMIT)


def _rms(x, g):
    return x * lax.rsqrt(jnp.mean(x * x, axis=-1, keepdims=True) + RMS_EPS) * g


def _mod_kernel(c_ref, w_ref, b_ref, o_ref):
    c = c_ref[...]
    c_act = c * (1.0 / (1.0 + jnp.exp(-c)))
    o_ref[...] = jnp.dot(c_act.astype(BF16), w_ref[...].astype(BF16),
                         preferred_element_type=F32) + b_ref[...]


def _adaln_mod(c_pad, w_ada, b_ada):
    rows, d = c_pad.shape
    n = w_ada.shape[1]
    tn = 1536
    return pl.pallas_call(
        _mod_kernel,
        grid=(n // tn,),
        in_specs=[pl.BlockSpec((rows, d), lambda j: (0, 0)),
                  pl.BlockSpec((d, tn), lambda j: (0, j)),
                  pl.BlockSpec((1, tn), lambda j: (0, j))],
        out_specs=pl.BlockSpec((rows, tn), lambda j: (0, j)),
        out_shape=jax.ShapeDtypeStruct((rows, n), F32),
        compiler_params=_params("arbitrary"),
        name="adaln_mod",
    )(c_pad, w_ada, b_ada)


def _inproj_kernel(x_ref, sc_ref, sh_ref, g_ref, w_ref, o_ref, h_scr):
    tm = x_ref.shape[0]

    def project(rows):
        for r in range(0, tm, rows):
            o_ref[r:r + rows, :] = jnp.dot(h_scr[r:r + rows, :], w_ref[...],
                                           preferred_element_type=F32).astype(o_ref.dtype)

    @pl.when(pl.program_id(1) == 0)
    def _():
        rows = tm // 4
        for r in range(0, tm, rows):
            h = _rms(x_ref[r:r + rows, :], g_ref[...]) * (1.0 + sc_ref[...]) + sh_ref[...]
            h_scr[r:r + rows, :] = h.astype(BF16)
        project(rows)

    @pl.when(pl.program_id(1) != 0)
    def _():
        project(tm)


def _in_proj(x2, mod3, g_pre, w_blk, tm):
    t, d = x2.shape
    nb, _, tn = w_blk.shape
    n = nb * tn
    per_b = SEQ // tm
    return pl.pallas_call(
        _inproj_kernel,
        grid=(t // tm, n // tn),
        in_specs=[pl.BlockSpec((tm, d), lambda i, j: (i, 0)),
                  pl.BlockSpec((None, 1, d), lambda i, j: ((i // per_b) * N_MOD + 1, 0, 0)),
                  pl.BlockSpec((None, 1, d), lambda i, j: ((i // per_b) * N_MOD + 0, 0, 0)),
                  pl.BlockSpec((1, d), lambda i, j: (0, 0)),
                  pl.BlockSpec((None, d, tn), lambda i, j: (j, 0, 0))],
        out_specs=pl.BlockSpec((tm, tn), lambda i, j: (i, j)),
        out_shape=jax.ShapeDtypeStruct((t, n), BF16),
        scratch_shapes=[pltpu.VMEM((tm, d), BF16)],
        compiler_params=_params("arbitrary", "arbitrary"),
        name="in_proj",
    )(x2, mod3, mod3, g_pre, w_blk)


def _qkv_kernel(ql_ref, kr_ref, kvl_ref, pos_ref, invf_ref, gq_ref, gkv_ref,
                wq_ref, wkv_ref, qT_ref, k_ref, vT_ref):
    tm = ql_ref.shape[0]
    ang = pos_ref[...].astype(F32) * invf_ref[...]
    lane = lax.broadcasted_iota(jnp.int32, (tm, LANES), 1)
    rope_lane = lane < QK_ROPE_DIM
    cos = jnp.where(rope_lane, jnp.cos(ang), 0.0)
    sin = jnp.where(rope_lane, jnp.sin(ang), 0.0)

    def rope(a):
        return a * cos + pltpu.roll(a, QK_ROPE_DIM, 1) * sin

    qn = _rms(ql_ref[...].astype(F32), gq_ref[...]).astype(BF16)
    q = jnp.dot(qn, wq_ref[...], preferred_element_type=F32)
    scale = LOG2_E / math.sqrt(QK_NOPE_DIM + QK_ROPE_DIM)
    kvn = _rms(kvl_ref[...].astype(F32), gkv_ref[...]).astype(BF16)
    kv = jnp.dot(kvn, wkv_ref[...], preferred_element_type=F32)
    k_rope = rope(kr_ref[...].astype(F32)).astype(BF16)
    v_off = MLA_HEADS * QK_NOPE_DIM
    ones = jnp.ones((VT_ROWS - V_HEAD_DIM, tm), BF16)
    for h in range(MLA_HEADS):
        lo = h * HEAD_QK
        qT_ref[lo:lo + LANES, :] = (q[:, lo:lo + LANES] * scale).T.astype(BF16)
        qT_ref[lo + LANES:lo + HEAD_QK, :] = (rope(q[:, lo + LANES:lo + HEAD_QK]) * scale).T.astype(BF16)
        k_ref[:, lo:lo + LANES] = kv[:, h * LANES:(h + 1) * LANES].astype(BF16)
        k_ref[:, lo + LANES:lo + HEAD_QK] = k_rope
        vlo = h * VT_ROWS
        vT_ref[vlo:vlo + V_HEAD_DIM, :] = kv[:, v_off + h * LANES:v_off + (h + 1) * LANES].T.astype(BF16)
        vT_ref[vlo + V_HEAD_DIM:vlo + VT_ROWS, :] = ones


def _latent_qkv(proj, pos2, invf, g_q, g_kv, wq_p, wkv_p, tm):
    t = proj.shape[0]
    hq = MLA_HEADS * HEAD_QK
    hv = MLA_HEADS * V_HEAD_DIM
    hvt = MLA_HEADS * VT_ROWS
    const = lambda i: (0, 0)
    return pl.pallas_call(
        _qkv_kernel,
        grid=(t // tm,),
        in_specs=[pl.BlockSpec((tm, Q_LORA_RANK), lambda i: (i, OFF_Q_LAT // Q_LORA_RANK)),
                  pl.BlockSpec((tm, LANES), lambda i: (i, OFF_K_ROPE // LANES)),
                  pl.BlockSpec((tm, KV_LORA_RANK), lambda i: (i, OFF_KV_LAT // KV_LORA_RANK)),
                  pl.BlockSpec((tm, 1), lambda i: (i, 0)),
                  pl.BlockSpec((1, LANES), const),
                  pl.BlockSpec((1, Q_LORA_RANK), const),
                  pl.BlockSpec((1, KV_LORA_RANK), const),
                  pl.BlockSpec((Q_LORA_RANK, hq), const),
                  pl.BlockSpec((KV_LORA_RANK, 2 * hv), const)],
        out_specs=[pl.BlockSpec((None, hq, tm), lambda i: (i, 0, 0)),
                   pl.BlockSpec((tm, hq), lambda i: (i, 0)),
                   pl.BlockSpec((None, hvt, tm), lambda i: (i, 0, 0))],
        out_shape=[jax.ShapeDtypeStruct((t // tm, hq, tm), BF16),
                   jax.ShapeDtypeStruct((t, hq), BF16),
                   jax.ShapeDtypeStruct((t // tm, hvt, tm), BF16)],
        compiler_params=_params("arbitrary"),
        name="latent_qkv",
    )(proj, proj, proj, pos2, invf, g_q, g_kv, wq_p, wkv_p)


def _causal_conv3(u, halo, w_ref, b_ref):
    w0, w1, w2, b = w_ref[0:1, :], w_ref[1:2, :], w_ref[2:3, :], b_ref[...]

    def taps(p2, p1, p0):
        return p2 * w0 + p1 * w1 + p0 * w2 + b

    body = taps(pltpu.roll(u, 2, 0), pltpu.roll(u, 1, 0), u)
    head = jnp.concatenate([halo, u[0:8, :]], axis=0)
    first = taps(pltpu.roll(head, 2, 0), pltpu.roll(head, 1, 0), head)[8:16, :]
    return jnp.concatenate([first, body[8:, :]], axis=0)


def _convgate_kernel(gb_ref, gc_ref, ci_ref, gch_ref, cih_ref, w_ref, b_ref, o_ref, *, per_b):
    first = (pl.program_id(0) % per_b) == 0
    g = gc_ref[...].astype(F32) * ci_ref[...].astype(F32)
    gh = gch_ref[...].astype(F32) * cih_ref[...].astype(F32)
    halo = jnp.where(first, 0.0, gh[8:16, :])
    o_ref[...] = (gb_ref[...].astype(F32) * _causal_conv3(g, halo, w_ref, b_ref)).astype(o_ref.dtype)


def _conv_gate(proj, conv_w, conv_b, tm):
    t = proj.shape[0]
    c = CONV_WIDTH
    hb = 16
    per_b = SEQ // tm
    halo_idx = lambda col: (lambda i: (jnp.maximum(i * (tm // hb) - 1, 0), col))
    return pl.pallas_call(
        functools.partial(_convgate_kernel, per_b=per_b),
        grid=(t // tm,),
        in_specs=[pl.BlockSpec((tm, c), lambda i: (i, OFF_GATE_B // c)),
                  pl.BlockSpec((tm, c), lambda i: (i, OFF_GATE_C // c)),
                  pl.BlockSpec((tm, c), lambda i: (i, OFF_CONV_IN // c)),
                  pl.BlockSpec((hb, c), halo_idx(OFF_GATE_C // c)),
                  pl.BlockSpec((hb, c), halo_idx(OFF_CONV_IN // c)),
                  pl.BlockSpec((3, c), lambda i: (0, 0)),
                  pl.BlockSpec((1, c), lambda i: (0, 0))],
        out_specs=pl.BlockSpec((tm, c), lambda i: (i, 0)),
        out_shape=jax.ShapeDtypeStruct((t, c), BF16),
        compiler_params=_params("arbitrary"),
        name="conv_gate",
    )(proj, proj, proj, proj, proj, conv_w, conv_b)


def _attn_kernel(qT_ref, k_ref, vT_ref, o_ref, sa_scr, sb_scr, m_scr, acc_scr, *, blk):
    qi = pl.program_id(2)
    qT = qT_ref[...]
    m_scr[...] = jnp.full(m_scr.shape, -1e30, F32)
    acc_scr[...] = jnp.zeros(acc_scr.shape, F32)

    def scores(j):
        start = pl.multiple_of(j * blk, blk)
        return jnp.dot(k_ref[pl.ds(start, blk), :], qT, preferred_element_type=F32)

    def consume(sT, j):
        m_old = m_scr[...]
        m_new = jnp.maximum(m_old, jnp.max(sT, axis=0, keepdims=True))
        alpha = jnp.exp2(m_old - m_new)
        pT = jnp.exp2(sT - m_new).astype(BF16)
        acc_scr[...] = alpha * acc_scr[...] + jnp.dot(vT_ref[j], pT, preferred_element_type=F32)
        m_scr[...] = m_new

    def consume_diag(s_ref):
        sT = s_ref[...]
        r = lax.broadcasted_iota(jnp.int32, sT.shape, 0)
        c = lax.broadcasted_iota(jnp.int32, sT.shape, 1)
        consume(jnp.where(r <= c, sT, -1e30), qi)

    def pair(j):
        sb_scr[...] = scores(j + 1)
        consume(sa_scr[...], j)
        sa_scr[...] = scores(j + 2)
        consume(sb_scr[...], j + 1)

    sa_scr[...] = scores(0)

    def body(t, carry):
        pair(4 * t)
        pair(4 * t + 2)
        return carry

    lax.fori_loop(0, qi // 4, body, 0)
    rem = qi % 4

    @pl.when(rem >= 2)
    def _():
        pair(qi - rem)

    @pl.when(rem % 2 == 0)
    def _():
        consume_diag(sa_scr)

    @pl.when(rem % 2 == 1)
    def _():
        sb_scr[...] = scores(qi)
        consume(sa_scr[...], qi - 1)
        consume_diag(sb_scr)

    acc = acc_scr[...]
    o_ref[...] = (acc[:V_HEAD_DIM] / acc[V_HEAD_DIM:V_HEAD_DIM + 1]).T.astype(o_ref.dtype)


def _attention(qT, k, vT, blk):
    t = k.shape[0]
    b = t // SEQ
    nq = SEQ // blk
    return pl.pallas_call(
        functools.partial(_attn_kernel, blk=blk),
        grid=(b, MLA_HEADS, nq),
        in_specs=[pl.BlockSpec((None, HEAD_QK, blk), lambda bi, h, i: (bi * nq + i, h, 0)),
                  pl.BlockSpec((SEQ, HEAD_QK), lambda bi, h, i: (bi, h)),
                  pl.BlockSpec((nq, VT_ROWS, blk), lambda bi, h, i: (bi, h, 0))],
        out_specs=pl.BlockSpec((blk, V_HEAD_DIM), lambda bi, h, i: (bi * nq + i, h)),
        out_shape=jax.ShapeDtypeStruct((t, MLA_HEADS * V_HEAD_DIM), BF16),
        scratch_shapes=[pltpu.VMEM((blk, blk), F32),
                        pltpu.VMEM((blk, blk), F32),
                        pltpu.VMEM((1, blk), F32),
                        pltpu.VMEM((VT_ROWS, blk), F32)],
        compiler_params=_params("arbitrary", "arbitrary", "arbitrary"),
        name="attention",
    )(qT, k, vT)


def _outproj_kernel(a_ref, c_ref, x_ref, gt_ref, g_ref, wa_ref, wc_ref, o_ref):
    tm = x_ref.shape[0]
    rows = tm // 2
    for r in range(0, tm, rows):
        mix = (jnp.dot(a_ref[r:r + rows, :], wa_ref[...], preferred_element_type=F32)
               + jnp.dot(c_ref[r:r + rows, :], wc_ref[...], preferred_element_type=F32))
        o_ref[r:r + rows, :] = x_ref[r:r + rows, :] + gt_ref[...] * _rms(mix, g_ref[...])


def _out_proj(attn, conv, x2, mod3, g_post, w_o_bf, tm):
    t, d = x2.shape
    ka = attn.shape[1]
    kc = conv.shape[1]
    per_b = SEQ // tm
    return pl.pallas_call(
        _outproj_kernel,
        grid=(t // tm,),
        in_specs=[pl.BlockSpec((tm, ka), lambda i: (i, 0)),
                  pl.BlockSpec((tm, kc), lambda i: (i, 0)),
                  pl.BlockSpec((tm, d), lambda i: (i, 0)),
                  pl.BlockSpec((None, 1, d), lambda i: ((i // per_b) * N_MOD + 2, 0, 0)),
                  pl.BlockSpec((1, d), lambda i: (0, 0)),
                  pl.BlockSpec((ka, d), lambda i: (0, 0)),
                  pl.BlockSpec((kc, d), lambda i: (1, 0))],
        out_specs=pl.BlockSpec((tm, d), lambda i: (i, 0)),
        out_shape=jax.ShapeDtypeStruct((t, d), F32),
        compiler_params=_params("arbitrary"),
        name="out_proj",
    )(attn, conv, x2, mod3, g_post, w_o_bf, w_o_bf)


def _ffn_kernel(x_ref, sc_ref, sh_ref, gt_ref, gpre_ref, gpost_ref, wa_ref, wg_ref,
                cwa_ref, cwg_ref, cba_ref, cbg_ref, wd_ref, o_ref,
                h_scr, acc_scr, halo_a, halo_g, y0_scr, y1_scr, *, per_b, nf, sub):
    i = pl.program_id(0)
    f = pl.program_id(1)
    tm = x_ref.shape[0]
    tf = wa_ref.shape[1]
    first = (i % per_b) == 0

    def branch(w_ref, cw_ref, cb_ref, halo_ref, cols):
        w = w_ref[:, cols]
        rows = tm // 4
        u = jnp.concatenate([jnp.dot(h_scr[r:r + rows, :], w, preferred_element_type=F32)
                             for r in range(0, tm, rows)], axis=0)
        halo = jnp.where(first, 0.0, halo_ref[f, :, cols])
        halo_ref[f, :, cols] = u[tm - 8:, :]
        return _causal_conv3(u, halo, cw_ref.at[:, cols], cb_ref.at[:, cols])

    def up(y_out):
        for s in range(0, tf, sub):
            cols = slice(s, s + sub)
            a = branch(wa_ref, cwa_ref, cba_ref, halo_a, cols)
            g = branch(wg_ref, cwg_ref, cbg_ref, halo_g, cols)
            y_out[:, cols] = (g * (1.0 / (1.0 + jnp.exp(-g))) * a).astype(BF16)

    def down(y_in):
        acc_scr[...] += jnp.dot(y_in[...], wd_ref[...], preferred_element_type=F32)

    @pl.when(f == 0)
    def _():
        h = _rms(x_ref[...], gpre_ref[...]) * (1.0 + sc_ref[...]) + sh_ref[...]
        h_scr[...] = h.astype(BF16)
        acc_scr[...] = jnp.zeros(acc_scr.shape, F32)
        up(y0_scr)

    interior = jnp.logical_and(f > 0, f < nf)

    @pl.when(jnp.logical_and(interior, f % 2 == 1))
    def _():
        up(y1_scr)
        down(y0_scr)

    @pl.when(jnp.logical_and(interior, f % 2 == 0))
    def _():
        up(y0_scr)
        down(y1_scr)

    @pl.when(f == nf)
    def _():
        down(y1_scr if (nf - 1) % 2 == 1 else y0_scr)
        o_ref[...] = x_ref[...] + gt_ref[...] * _rms(acc_scr[...], gpost_ref[...])


def _conv_ffn(x1, mod3, g_pre, g_post, w_up_bf, conv_w, conv_b, w_down_bf, tm):
    t, d = x1.shape
    tf = FFN_TF
    nf = D_FF // tf
    per_b = SEQ // tm
    modspec = lambda k: pl.BlockSpec((None, 1, d), lambda i, f: ((i // per_b) * N_MOD + k, 0, 0))
    up_f = lambda f: jnp.minimum(f, nf - 1)
    dn_f = lambda f: jnp.maximum(f - 1, 0)
    return pl.pallas_call(
        functools.partial(_ffn_kernel, per_b=per_b, nf=nf, sub=2 * LANES),
        grid=(t // tm, nf + 1),
        in_specs=[pl.BlockSpec((tm, d), lambda i, f: (i, 0)),
                  modspec(4), modspec(3), modspec(5),
                  pl.BlockSpec((1, d), lambda i, f: (0, 0)),
                  pl.BlockSpec((1, d), lambda i, f: (0, 0)),
                  pl.BlockSpec((d, tf), lambda i, f: (0, up_f(f))),
                  pl.BlockSpec((d, tf), lambda i, f: (0, nf + up_f(f))),
                  pl.BlockSpec((3, tf), lambda i, f: (0, up_f(f))),
                  pl.BlockSpec((3, tf), lambda i, f: (0, nf + up_f(f))),
                  pl.BlockSpec((1, tf), lambda i, f: (0, up_f(f))),
                  pl.BlockSpec((1, tf), lambda i, f: (0, nf + up_f(f))),
                  pl.BlockSpec((tf, d), lambda i, f: (dn_f(f), 0))],
        out_specs=pl.BlockSpec((tm, d), lambda i, f: (i, 0)),
        out_shape=jax.ShapeDtypeStruct((t, d), F32),
        scratch_shapes=[pltpu.VMEM((tm, d), BF16),
                        pltpu.VMEM((tm, d), F32),
                        pltpu.VMEM((nf, 8, tf), F32),
                        pltpu.VMEM((nf, 8, tf), F32),
                        pltpu.VMEM((tm, tf), BF16),
                        pltpu.VMEM((tm, tf), BF16)],
        compiler_params=_params("arbitrary", "arbitrary"),
        name="conv_ffn",
    )(x1, mod3, mod3, mod3, g_pre, g_post, w_up_bf, w_up_bf, conv_w, conv_w,
      conv_b, conv_b, w_down_bf)


def _rot_half_cols(w):
    half = w.shape[-1] // 2
    return jnp.concatenate([-w[..., half:], w[..., :half]], axis=-1)


def _regroup_w_in(w):
    d = w.shape[0]
    c0 = Q_LORA_RANK
    c1 = c0 + KV_LORA_RANK
    c2 = c1 + QK_ROPE_DIM
    c3 = c2 + CONV_WIDTH
    c4 = c3 + CONV_WIDTH
    q_lat, kv_lat, k_rope = w[:, :c0], w[:, c0:c1], w[:, c1:c2]
    gate_b, gate_c, conv_in = w[:, c2:c3], w[:, c3:c4], w[:, c4:]
    pad = jnp.zeros((d, OFF_KV_LAT - OFF_K_ROPE - 2 * QK_ROPE_DIM), w.dtype)
    return jnp.concatenate([gate_b, gate_c, conv_in, q_lat, k_rope, _rot_half_cols(k_rope),
                            pad, kv_lat], axis=1)


def _block_cols(w, tn):
    k, n = w.shape
    return w.reshape(k, n // tn, tn).transpose(1, 0, 2)


def _regroup_w_uq(w):
    r = w.shape[0]
    w = w.reshape(r, MLA_HEADS, QK_NOPE_DIM + QK_ROPE_DIM)
    nope, rope = w[..., :QK_NOPE_DIM], w[..., QK_NOPE_DIM:]
    return jnp.concatenate([nope, rope, _rot_half_cols(rope)], axis=-1).reshape(r, MLA_HEADS * HEAD_QK)


def _regroup_w_ukv(w):
    r = w.shape[0]
    w = w.reshape(r, MLA_HEADS, QK_NOPE_DIM + V_HEAD_DIM)
    k = w[..., :QK_NOPE_DIM].reshape(r, MLA_HEADS * QK_NOPE_DIM)
    v = w[..., QK_NOPE_DIM:].reshape(r, MLA_HEADS * V_HEAD_DIM)
    return jnp.concatenate([k, v], axis=1)


def kernel(x, c, positions, w_ada, b_ada, g_pre_mix, g_post_mix, w_in, g_q, w_uq, g_kv, w_ukv,
           conv_w_mix, conv_b_mix, w_o, g_pre_ffn, g_post_ffn, w_up, conv_w_ffn, conv_b_ffn, w_down):
    b, s, d = x.shape
    assert (s, d) == (SEQ, D_MODEL) and w_ada.shape[0] == 1
    t = b * s
    x2 = x.reshape(t, d)
    pos2 = positions.reshape(t, 1)
    half = jnp.arange(0, QK_ROPE_DIM, 2, dtype=F32) / QK_ROPE_DIM
    inv_freq = 1.0 / (ROPE_THETA ** half)
    invf = jnp.concatenate([inv_freq, inv_freq, jnp.zeros((LANES - QK_ROPE_DIM,), F32)]).reshape(1, LANES)

    rows = 8
    c_pad = jnp.zeros((rows, d), F32).at[:b].set(c)
    x_cur = x2
    for l in range(w_ada.shape[0]):
        mod = _adaln_mod(c_pad, w_ada[l], b_ada[l].reshape(1, -1))
        mod3 = mod[:b].reshape(b * N_MOD, 1, d)

        w_in_blk = _block_cols(_regroup_w_in(w_in[l].astype(BF16)), IN_PROJ_TN)
        wq_p = _regroup_w_uq(w_uq[l]).astype(BF16)
        wkv_p = _regroup_w_ukv(w_ukv[l]).astype(BF16)

        proj = _in_proj(x_cur, mod3, g_pre_mix[l].reshape(1, d), w_in_blk, tm=1024)
        qT, k, vT = _latent_qkv(proj, pos2, invf, g_q[l].reshape(1, -1), g_kv[l].reshape(1, -1),
                                wq_p, wkv_p, tm=ATT_BLK)
        conv = _conv_gate(proj, conv_w_mix[l], conv_b_mix[l].reshape(1, -1), tm=512)
        attn = _attention(qT, k, vT, blk=ATT_BLK)
        x_cur = _out_proj(attn, conv, x_cur, mod3, g_post_mix[l].reshape(1, d),
                          w_o[l].astype(BF16), tm=512)
        x_cur = _conv_ffn(x_cur, mod3, g_pre_ffn[l].reshape(1, d), g_post_ffn[l].reshape(1, d),
                          w_up[l].astype(BF16), conv_w_ffn[l],
                          conv_b_ffn[l].reshape(1, -1), w_down[l].astype(BF16), tm=512)
    return x_cur.reshape(b, s, d)
```

```python
import functools
import math

import jax
import jax.numpy as jnp
from jax import lax
from jax.experimental import pallas as pl
from jax.experimental.pallas import tpu as pltpu

D_MODEL = 2048
SEQ = 8192
CONV_WIDTH = 1024
MLA_HEADS = 8
QK_NOPE_DIM = 128
QK_ROPE_DIM = 64
V_HEAD_DIM = 128
Q_LORA_RANK = 768
KV_LORA_RANK = 512
ROPE_THETA = 10000.0
D_FF = 5632
RMS_EPS = 1e-6
N_MOD = 6

LANES = 128
HEAD_QK = 2 * LANES
PROJ_COLS = 4608
VMEM_LIMIT = 56 * 1024 * 1024
ATT_BLK = 512
VT_ROWS = V_HEAD_DIM + 16
LOG2_E = 1.4426950408889634
IN_PROJ_TN = 1536
FFN_TF = 512

OFF_GATE_B = 0
OFF_GATE_C = 1024
OFF_CONV_IN = 2048
OFF_Q_LAT = 3072
OFF_K_ROPE = 3840
OFF_KV_LAT = 4096

BF16 = jnp.bfloat16
F32 = jnp.float32


def _params(*sem):
    return pltpu.CompilerParams(dimension_semantics=sem, vmem_limit_bytes=VMEM_LIMIT)


def _rms(x, g):
    return x * lax.rsqrt(jnp.mean(x * x, axis=-1, keepdims=True) + RMS_EPS) * g


def _mod_kernel(c_ref, w_ref, b_ref, o_ref):
    c = c_ref[...]
    c_act = c * (1.0 / (1.0 + jnp.exp(-c)))
    o_ref[...] = jnp.dot(c_act.astype(BF16), w_ref[...].astype(BF16),
                         preferred_element_type=F32) + b_ref[...]


def _adaln_mod(c_pad, w_ada, b_ada):
    rows, d = c_pad.shape
    n = w_ada.shape[1]
    tn = 1536
    return pl.pallas_call(
        _mod_kernel,
        grid=(n // tn,),
        in_specs=[pl.BlockSpec((rows, d), lambda j: (0, 0)),
                  pl.BlockSpec((d, tn), lambda j: (0, j)),
                  pl.BlockSpec((1, tn), lambda j: (0, j))],
        out_specs=pl.BlockSpec((rows, tn), lambda j: (0, j)),
        out_shape=jax.ShapeDtypeStruct((rows, n), F32),
        compiler_params=_params("arbitrary"),
        name="adaln_mod",
    )(c_pad, w_ada, b_ada)


def _inproj_kernel(x_ref, sc_ref, sh_ref, g_ref, w_ref, o_ref, h_scr):
    tm = x_ref.shape[0]

    def project(rows):
        for r in range(0, tm, rows):
            o_ref[r:r + rows, :] = jnp.dot(h_scr[r:r + rows, :], w_ref[...],
                                           preferred_element_type=F32).astype(o_ref.dtype)

    @pl.when(pl.program_id(1) == 0)
    def _():
        rows = tm // 4
        for r in range(0, tm, rows):
            h = _rms(x_ref[r:r + rows, :], g_ref[...]) * (1.0 + sc_ref[...]) + sh_ref[...]
            h_scr[r:r + rows, :] = h.astype(BF16)
        project(rows)

    @pl.when(pl.program_id(1) != 0)
    def _():
        project(tm)


def _in_proj(x2, mod3, g_pre, w_blk, tm):
    t, d = x2.shape
    nb, _, tn = w_blk.shape
    n = nb * tn
    per_b = SEQ // tm
    return pl.pallas_call(
        _inproj_kernel,
        grid=(t // tm, n // tn),
        in_specs=[pl.BlockSpec((tm, d), lambda i, j: (i, 0)),
                  pl.BlockSpec((None, 1, d), lambda i, j: ((i // per_b) * N_MOD + 1, 0, 0)),
                  pl.BlockSpec((None, 1, d), lambda i, j: ((i // per_b) * N_MOD + 0, 0, 0)),
                  pl.BlockSpec((1, d), lambda i, j: (0, 0)),
                  pl.BlockSpec((None, d, tn), lambda i, j: (j, 0, 0))],
        out_specs=pl.BlockSpec((tm, tn), lambda i, j: (i, j)),
        out_shape=jax.ShapeDtypeStruct((t, n), BF16),
        scratch_shapes=[pltpu.VMEM((tm, d), BF16)],
        compiler_params=_params("arbitrary", "arbitrary"),
        name="in_proj",
    )(x2, mod3, mod3, g_pre, w_blk)


def _qkv_kernel(ql_ref, kr_ref, kvl_ref, pos_ref, invf_ref, gq_ref, gkv_ref,
                wq_ref, wkv_ref, qT_ref, k_ref, vT_ref):
    tm = ql_ref.shape[0]
    ang = pos_ref[...].astype(F32) * invf_ref[...]
    lane = lax.broadcasted_iota(jnp.int32, (tm, LANES), 1)
    rope_lane = lane < QK_ROPE_DIM
    cos = jnp.where(rope_lane, jnp.cos(ang), 0.0)
    sin = jnp.where(rope_lane, jnp.sin(ang), 0.0)

    def rope(a):
        return a * cos + pltpu.roll(a, QK_ROPE_DIM, 1) * sin

    qn = _rms(ql_ref[...].astype(F32), gq_ref[...]).astype(BF16)
    q = jnp.dot(qn, wq_ref[...], preferred_element_type=F32)
    scale = LOG2_E / math.sqrt(QK_NOPE_DIM + QK_ROPE_DIM)
    kvn = _rms(kvl_ref[...].astype(F32), gkv_ref[...]).astype(BF16)
    kv = jnp.dot(kvn, wkv_ref[...], preferred_element_type=F32)
    k_rope = rope(kr_ref[...].astype(F32)).astype(BF16)
    v_off = MLA_HEADS * QK_NOPE_DIM
    ones = jnp.ones((VT_ROWS - V_HEAD_DIM, tm), BF16)
    for h in range(MLA_HEADS):
        lo = h * HEAD_QK
        qT_ref[lo:lo + LANES, :] = (q[:, lo:lo + LANES] * scale).T.astype(BF16)
        qT_ref[lo + LANES:lo + HEAD_QK, :] = (rope(q[:, lo + LANES:lo + HEAD_QK]) * scale).T.astype(BF16)
        k_ref[:, lo:lo + LANES] = kv[:, h * LANES:(h + 1) * LANES].astype(BF16)
        k_ref[:, lo + LANES:lo + HEAD_QK] = k_rope
        vlo = h * VT_ROWS
        vT_ref[vlo:vlo + V_HEAD_DIM, :] = kv[:, v_off + h * LANES:v_off + (h + 1) * LANES].T.astype(BF16)
        vT_ref[vlo + V_HEAD_DIM:vlo + VT_ROWS, :] = ones


def _latent_qkv(proj, pos2, invf, g_q, g_kv, wq_p, wkv_p, tm):
    t = proj.shape[0]
    hq = MLA_HEADS * HEAD_QK
    hv = MLA_HEADS * V_HEAD_DIM
    hvt = MLA_HEADS * VT_ROWS
    const = lambda i: (0, 0)
    return pl.pallas_call(
        _qkv_kernel,
        grid=(t // tm,),
        in_specs=[pl.BlockSpec((tm, Q_LORA_RANK), lambda i: (i, OFF_Q_LAT // Q_LORA_RANK)),
                  pl.BlockSpec((tm, LANES), lambda i: (i, OFF_K_ROPE // LANES)),
                  pl.BlockSpec((tm, KV_LORA_RANK), lambda i: (i, OFF_KV_LAT // KV_LORA_RANK)),
                  pl.BlockSpec((tm, 1), lambda i: (i, 0)),
                  pl.BlockSpec((1, LANES), const),
                  pl.BlockSpec((1, Q_LORA_RANK), const),
                  pl.BlockSpec((1, KV_LORA_RANK), const),
                  pl.BlockSpec((Q_LORA_RANK, hq), const),
                  pl.BlockSpec((KV_LORA_RANK, 2 * hv), const)],
        out_specs=[pl.BlockSpec((None, hq, tm), lambda i: (i, 0, 0)),
                   pl.BlockSpec((tm, hq), lambda i: (i, 0)),
                   pl.BlockSpec((None, hvt, tm), lambda i: (i, 0, 0))],
        out_shape=[jax.ShapeDtypeStruct((t // tm, hq, tm), BF16),
                   jax.ShapeDtypeStruct((t, hq), BF16),
                   jax.ShapeDtypeStruct((t // tm, hvt, tm), BF16)],
        compiler_params=_params("arbitrary"),
        name="latent_qkv",
    )(proj, proj, proj, pos2, invf, g_q, g_kv, wq_p, wkv_p)


def _causal_conv3(u, halo, w_ref, b_ref):
    w0, w1, w2, b = w_ref[0:1, :], w_ref[1:2, :], w_ref[2:3, :], b_ref[...]

    def taps(p2, p1, p0):
        return p2 * w0 + p1 * w1 + p0 * w2 + b

    body = taps(pltpu.roll(u, 2, 0), pltpu.roll(u, 1, 0), u)
    head = jnp.concatenate([halo, u[0:8, :]], axis=0)
    first = taps(pltpu.roll(head, 2, 0), pltpu.roll(head, 1, 0), head)[8:16, :]
    return jnp.concatenate([first, body[8:, :]], axis=0)


def _convgate_kernel(gb_ref, gc_ref, ci_ref, gch_ref, cih_ref, w_ref, b_ref, o_ref, *, per_b):
    first = (pl.program_id(0) % per_b) == 0
    g = gc_ref[...].astype(F32) * ci_ref[...].astype(F32)
    gh = gch_ref[...].astype(F32) * cih_ref[...].astype(F32)
    halo = jnp.where(first, 0.0, gh[8:16, :])
    o_ref[...] = (gb_ref[...].astype(F32) * _causal_conv3(g, halo, w_ref, b_ref)).astype(o_ref.dtype)


def _conv_gate(proj, conv_w, conv_b, tm):
    t = proj.shape[0]
    c = CONV_WIDTH
    hb = 16
    per_b = SEQ // tm
    halo_idx = lambda col: (lambda i: (jnp.maximum(i * (tm // hb) - 1, 0), col))
    return pl.pallas_call(
        functools.partial(_convgate_kernel, per_b=per_b),
        grid=(t // tm,),
        in_specs=[pl.BlockSpec((tm, c), lambda i: (i, OFF_GATE_B // c)),
                  pl.BlockSpec((tm, c), lambda i: (i, OFF_GATE_C // c)),
                  pl.BlockSpec((tm, c), lambda i: (i, OFF_CONV_IN // c)),
                  pl.BlockSpec((hb, c), halo_idx(OFF_GATE_C // c)),
                  pl.BlockSpec((hb, c), halo_idx(OFF_CONV_IN // c)),
                  pl.BlockSpec((3, c), lambda i: (0, 0)),
                  pl.BlockSpec((1, c), lambda i: (0, 0))],
        out_specs=pl.BlockSpec((tm, c), lambda i: (i, 0)),
        out_shape=jax.ShapeDtypeStruct((t, c), BF16),
        compiler_params=_params("arbitrary"),
        name="conv_gate",
    )(proj, proj, proj, proj, proj, conv_w, conv_b)


def _attn_kernel(qT_ref, k_ref, vT_ref, o_ref, sa_scr, sb_scr, m_scr, acc_scr, *, blk):
    qi = pl.program_id(2)
    qT = qT_ref[...]
    m_scr[...] = jnp.full(m_scr.shape, -1e30, F32)
    acc_scr[...] = jnp.zeros(acc_scr.shape, F32)

    def scores(j):
        start = pl.multiple_of(j * blk, blk)
        return jnp.dot(k_ref[pl.ds(start, blk), :], qT, preferred_element_type=F32)

    def consume(sT, j):
        m_old = m_scr[...]
        m_new = jnp.maximum(m_old, jnp.max(sT, axis=0, keepdims=True))
        alpha = jnp.exp2(m_old - m_new)
        pT = jnp.exp2(sT - m_new).astype(BF16)
        acc_scr[...] = alpha * acc_scr[...] + jnp.dot(vT_ref[j], pT, preferred_element_type=F32)
        m_scr[...] = m_new

    def consume_diag(s_ref):
        sT = s_ref[...]
        r = lax.broadcasted_iota(jnp.int32, sT.shape, 0)
        c = lax.broadcasted_iota(jnp.int32, sT.shape, 1)
        consume(jnp.where(r <= c, sT, -1e30), qi)

    def pair(j):
        sb_scr[...] = scores(j + 1)
        consume(sa_scr[...], j)
        sa_scr[...] = scores(j + 2)
        consume(sb_scr[...], j + 1)

    sa_scr[...] = scores(0)

    def body(t, carry):
        pair(4 * t)
        pair(4 * t + 2)
        return carry

    lax.fori_loop(0, qi // 4, body, 0)
    rem = qi % 4

    @pl.when(rem >= 2)
    def _():
        pair(qi - rem)

    @pl.when(rem % 2 == 0)
    def _():
        consume_diag(sa_scr)

    @pl.when(rem % 2 == 1)
    def _():
        sb_scr[...] = scores(qi)
        consume(sa_scr[...], qi - 1)
        consume_diag(sb_scr)

    acc = acc_scr[...]
    o_ref[...] = (acc[:V_HEAD_DIM] / acc[V_HEAD_DIM:V_HEAD_DIM + 1]).T.astype(o_ref.dtype)


def _attention(qT, k, vT, blk):
    t = k.shape[0]
    b = t // SEQ
    nq = SEQ // blk
    return pl.pallas_call(
        functools.partial(_attn_kernel, blk=blk),
        grid=(b, MLA_HEADS, nq),
        in_specs=[pl.BlockSpec((None, HEAD_QK, blk), lambda bi, h, i: (bi * nq + i, h, 0)),
                  pl.BlockSpec((SEQ, HEAD_QK), lambda bi, h, i: (bi, h)),
                  pl.BlockSpec((nq, VT_ROWS, blk), lambda bi, h, i: (bi, h, 0))],
        out_specs=pl.BlockSpec((blk, V_HEAD_DIM), lambda bi, h, i: (bi * nq + i, h)),
        out_shape=jax.ShapeDtypeStruct((t, MLA_HEADS * V_HEAD_DIM), BF16),
        scratch_shapes=[pltpu.VMEM((blk, blk), F32),
                        pltpu.VMEM((blk, blk), F32),
                        pltpu.VMEM((1, blk), F32),
                        pltpu.VMEM((VT_ROWS, blk), F32)],
        compiler_params=_params("arbitrary", "arbitrary", "arbitrary"),
        name="attention",
    )(qT, k, vT)


def _outproj_kernel(a_ref, c_ref, x_ref, gt_ref, g_ref, wa_ref, wc_ref, o_ref):
    mix = (jnp.dot(a_ref[...], wa_ref[...], preferred_element_type=F32)
           + jnp.dot(c_ref[...], wc_ref[...], preferred_element_type=F32))
    o_ref[...] = x_ref[...] + gt_ref[...] * _rms(mix, g_ref[...])


def _out_proj(attn, conv, x2, mod3, g_post, w_o_bf, tm):
    t, d = x2.shape
    ka = attn.shape[1]
    kc = conv.shape[1]
    per_b = SEQ // tm
    return pl.pallas_call(
        _outproj_kernel,
        grid=(t // tm,),
        in_specs=[pl.BlockSpec((tm, ka), lambda i: (i, 0)),
                  pl.BlockSpec((tm, kc), lambda i: (i, 0)),
                  pl.BlockSpec((tm, d), lambda i: (i, 0)),
                  pl.BlockSpec((None, 1, d), lambda i: ((i // per_b) * N_MOD + 2, 0, 0)),
                  pl.BlockSpec((1, d), lambda i: (0, 0)),
                  pl.BlockSpec((ka, d), lambda i: (0, 0)),
                  pl.BlockSpec((kc, d), lambda i: (1, 0))],
        out_specs=pl.BlockSpec((tm, d), lambda i: (i, 0)),
        out_shape=jax.ShapeDtypeStruct((t, d), F32),
        compiler_params=_params("arbitrary"),
        name="out_proj",
    )(attn, conv, x2, mod3, g_post, w_o_bf, w_o_bf)


def _ffn_kernel(x_ref, sc_ref, sh_ref, gt_ref, gpre_ref, gpost_ref, wa_ref, wg_ref,
                cwa_ref, cwg_ref, cba_ref, cbg_ref, wd_ref, o_ref,
                h_scr, acc_scr, halo_a, halo_g, y0_scr, y1_scr, *, per_b, nf, sub):
    i = pl.program_id(0)
    f = pl.program_id(1)
    tm = x_ref.shape[0]
    tf = wa_ref.shape[1]
    first = (i % per_b) == 0

    def branch(w_ref, cw_ref, cb_ref, halo_ref, cols):
        w = w_ref[:, cols]
        rows = tm // 2
        u = jnp.concatenate([jnp.dot(h_scr[r:r + rows, :], w, preferred_element_type=F32)
                             for r in range(0, tm, rows)], axis=0)
        halo = jnp.where(first, 0.0, halo_ref[f, :, cols])
        halo_ref[f, :, cols] = u[tm - 8:, :]
        return _causal_conv3(u, halo, cw_ref.at[:, cols], cb_ref.at[:, cols])

    def up(y_out):
        for s in range(0, tf, sub):
            cols = slice(s, s + sub)
            a = branch(wa_ref, cwa_ref, cba_ref, halo_a, cols)
            g = branch(wg_ref, cwg_ref, cbg_ref, halo_g, cols)
            y_out[:, cols] = (g * (1.0 / (1.0 + jnp.exp(-g))) * a).astype(BF16)

    def down(y_in):
        acc_scr[...] += jnp.dot(y_in[...], wd_ref[...], preferred_element_type=F32)

    @pl.when(f == 0)
    def _():
        h = _rms(x_ref[...], gpre_ref[...]) * (1.0 + sc_ref[...]) + sh_ref[...]
        h_scr[...] = h.astype(BF16)
        acc_scr[...] = jnp.zeros(acc_scr.shape, F32)
        up(y0_scr)

    interior = jnp.logical_and(f > 0, f < nf)

    @pl.when(jnp.logical_and(interior, f % 2 == 1))
    def _():
        up(y1_scr)
        down(y0_scr)

    @pl.when(jnp.logical_and(interior, f % 2 == 0))
    def _():
        up(y0_scr)
        down(y1_scr)

    @pl.when(f == nf)
    def _():
        down(y1_scr if (nf - 1) % 2 == 1 else y0_scr)
        o_ref[...] = x_ref[...] + gt_ref[...] * _rms(acc_scr[...], gpost_ref[...])


def _conv_ffn(x1, mod3, g_pre, g_post, w_up_bf, conv_w, conv_b, w_down_bf, tm):
    t, d = x1.shape
    tf = FFN_TF
    nf = D_FF // tf
    per_b = SEQ // tm
    modspec = lambda k: pl.BlockSpec((None, 1, d), lambda i, f: ((i // per_b) * N_MOD + k, 0, 0))
    up_f = lambda f: jnp.minimum(f, nf - 1)
    dn_f = lambda f: jnp.maximum(f - 1, 0)
    return pl.pallas_call(
        functools.partial(_ffn_kernel, per_b=per_b, nf=nf, sub=2 * LANES),
        grid=(t // tm, nf + 1),
        in_specs=[pl.BlockSpec((tm, d), lambda i, f: (i, 0)),
                  modspec(4), modspec(3), modspec(5),
                  pl.BlockSpec((1, d), lambda i, f: (0, 0)),
                  pl.BlockSpec((1, d), lambda i, f: (0, 0)),
                  pl.BlockSpec((d, tf), lambda i, f: (0, up_f(f))),
                  pl.BlockSpec((d, tf), lambda i, f: (0, nf + up_f(f))),
                  pl.BlockSpec((3, tf), lambda i, f: (0, up_f(f))),
                  pl.BlockSpec((3, tf), lambda i, f: (0, nf + up_f(f))),
                  pl.BlockSpec((1, tf), lambda i, f: (0, up_f(f))),
                  pl.BlockSpec((1, tf), lambda i, f: (0, nf + up_f(f))),
                  pl.BlockSpec((tf, d), lambda i, f: (dn_f(f), 0))],
        out_specs=pl.BlockSpec((tm, d), lambda i, f: (i, 0)),
        out_shape=jax.ShapeDtypeStruct((t, d), F32),
        scratch_shapes=[pltpu.VMEM((tm, d), BF16),
                        pltpu.VMEM((tm, d), F32),
                        pltpu.VMEM((nf, 8, tf), F32),
                        pltpu.VMEM((nf, 8, tf), F32),
                        pltpu.VMEM((tm, tf), BF16),
                        pltpu.VMEM((tm, tf), BF16)],
        compiler_params=_params("arbitrary", "arbitrary"),
        name="conv_ffn",
    )(x1, mod3, mod3, mod3, g_pre, g_post, w_up_bf, w_up_bf, conv_w, conv_w,
      conv_b, conv_b, w_down_bf)


def _rot_half_cols(w):
    half = w.shape[-1] // 2
    return jnp.concatenate([-w[..., half:], w[..., :half]], axis=-1)


def _regroup_w_in(w):
    d = w.shape[0]
    c0 = Q_LORA_RANK
    c1 = c0 + KV_LORA_RANK
    c2 = c1 + QK_ROPE_DIM
    c3 = c2 + CONV_WIDTH
    c4 = c3 + CONV_WIDTH
    q_lat, kv_lat, k_rope = w[:, :c0], w[:, c0:c1], w[:, c1:c2]
    gate_b, gate_c, conv_in = w[:, c2:c3], w[:, c3:c4], w[:, c4:]
    pad = jnp.zeros((d, OFF_KV_LAT - OFF_K_ROPE - 2 * QK_ROPE_DIM), w.dtype)
    return jnp.concatenate([gate_b, gate_c, conv_in, q_lat, k_rope, _rot_half_cols(k_rope),
                            pad, kv_lat], axis=1)


def _block_cols(w, tn):
    k, n = w.shape
    return w.reshape(k, n // tn, tn).transpose(1, 0, 2)


def _regroup_w_uq(w):
    r = w.shape[0]
    w = w.reshape(r, MLA_HEADS, QK_NOPE_DIM + QK_ROPE_DIM)
    nope, rope = w[..., :QK_NOPE_DIM], w[..., QK_NOPE_DIM:]
    return jnp.concatenate([nope, rope, _rot_half_cols(rope)], axis=-1).reshape(r, MLA_HEADS * HEAD_QK)


def _regroup_w_ukv(w):
    r = w.shape[0]
    w = w.reshape(r, MLA_HEADS, QK_NOPE_DIM + V_HEAD_DIM)
    k = w[..., :QK_NOPE_DIM].reshape(r, MLA_HEADS * QK_NOPE_DIM)
    v = w[..., QK_NOPE_DIM:].reshape(r, MLA_HEADS * V_HEAD_DIM)
    return jnp.concatenate([k, v], axis=1)


def kernel(x, c, positions, w_ada, b_ada, g_pre_mix, g_post_mix, w_in, g_q, w_uq, g_kv, w_ukv,
           conv_w_mix, conv_b_mix, w_o, g_pre_ffn, g_post_ffn, w_up, conv_w_ffn, conv_b_ffn, w_down):
    b, s, d = x.shape
    assert (s, d) == (SEQ, D_MODEL) and w_ada.shape[0] == 1
    t = b * s
    x2 = x.reshape(t, d)
    pos2 = positions.reshape(t, 1)
    half = jnp.arange(0, QK_ROPE_DIM, 2, dtype=F32) / QK_ROPE_DIM
    inv_freq = 1.0 / (ROPE_THETA ** half)
    invf = jnp.concatenate([inv_freq, inv_freq, jnp.zeros((LANES - QK_ROPE_DIM,), F32)]).reshape(1, LANES)

    rows = 8
    c_pad = jnp.zeros((rows, d), F32).at[:b].set(c)
    x_cur = x2
    for l in range(w_ada.shape[0]):
        mod = _adaln_mod(c_pad, w_ada[l], b_ada[l].reshape(1, -1))
        mod3 = mod[:b].reshape(b * N_MOD, 1, d)

        w_in_blk = _block_cols(_regroup_w_in(w_in[l].astype(BF16)), IN_PROJ_TN)
        wq_p = _regroup_w_uq(w_uq[l]).astype(BF16)
        wkv_p = _regroup_w_ukv(w_ukv[l]).astype(BF16)

        proj = _in_proj(x_cur, mod3, g_pre_mix[l].reshape(1, d), w_in_blk, tm=1024)
        qT, k, vT = _latent_qkv(proj, pos2, invf, g_q[l].reshape(1, -1), g_kv[l].reshape(1, -1),
                                wq_p, wkv_p, tm=ATT_BLK)
        conv = _conv_gate(proj, conv_w_mix[l], conv_b_mix[l].reshape(1, -1), tm=512)
        attn = _attention(qT, k, vT, blk=ATT_BLK)
        x_cur = _out_proj(attn, conv, x_cur, mod3, g_post_mix[l].reshape(1, d),
                          w_o[l].astype(BF16), tm=512)
        x_cur = _conv_ffn(x_cur, mod3, g_pre_ffn[l].reshape(1, d), g_post_ffn[l].reshape(1, d),
                          w_up[l].astype(BF16), conv_w_ffn[l],
                          conv_b_ffn[l].reshape(1, -1), w_down[l].astype(BF16), tm=512)
    return x_cur.reshape(b, s, d)
```

```python
import functools
import math

import jax
import jax.numpy as jnp
from jax import lax
from jax.experimental import pallas as pl
from jax.experimental.pallas import tpu as pltpu

D_MODEL = 2048
SEQ = 8192
CONV_WIDTH = 1024
MLA_HEADS = 8
QK_NOPE_DIM = 128
QK_ROPE_DIM = 64
V_HEAD_DIM = 128
Q_LORA_RANK = 768
KV_LORA_RANK = 512
ROPE_THETA = 10000.0
D_FF = 5632
RMS_EPS = 1e-6
N_MOD = 6

LANES = 128
HEAD_QK = 2 * LANES
PROJ_COLS = 4608
VMEM_LIMIT = 56 * 1024 * 1024
ATT_BLK = 512
ATT_HEADS_PER_STEP = 2
VT_ROWS = V_HEAD_DIM + 16
LOG2_E = 1.4426950408889634
IN_PROJ_TN = 1536
FFN_TF = 512

OFF_GATE_B = 0
OFF_GATE_C = 1024
OFF_CONV_IN = 2048
OFF_Q_LAT = 3072
OFF_K_ROPE = 3840
OFF_KV_LAT = 4096

BF16 = jnp.bfloat16
F32 = jnp.float32


def _params(*sem):
    return pltpu.CompilerParams(dimension_semantics=sem, vmem_limit_bytes=VMEM_LIMIT)


def _rms(x, g):
    return x * lax.rsqrt(jnp.mean(x * x, axis=-1, keepdims=True) + RMS_EPS) * g


def _mod_kernel(c_ref, w_ref, b_ref, o_ref):
    c = c_ref[...]
    c_act = c * (1.0 / (1.0 + jnp.exp(-c)))
    o_ref[...] = jnp.dot(c_act.astype(BF16), w_ref[...].astype(BF16),
                         preferred_element_type=F32) + b_ref[...]


def _adaln_mod(c_pad, w_ada, b_ada):
    rows, d = c_pad.shape
    n = w_ada.shape[1]
    tn = 1536
    return pl.pallas_call(
        _mod_kernel,
        grid=(n // tn,),
        in_specs=[pl.BlockSpec((rows, d), lambda j: (0, 0)),
                  pl.BlockSpec((d, tn), lambda j: (0, j)),
                  pl.BlockSpec((1, tn), lambda j: (0, j))],
        out_specs=pl.BlockSpec((rows, tn), lambda j: (0, j)),
        out_shape=jax.ShapeDtypeStruct((rows, n), F32),
        compiler_params=_params("arbitrary"),
        name="adaln_mod",
    )(c_pad, w_ada, b_ada)


def _inproj_kernel(x_ref, sc_ref, sh_ref, g_ref, w_ref, o_ref, h_scr):
    tm = x_ref.shape[0]

    def project(rows):
        for r in range(0, tm, rows):
            o_ref[r:r + rows, :] = jnp.dot(h_scr[r:r + rows, :], w_ref[...],
                                           preferred_element_type=F32).astype(o_ref.dtype)

    @pl.when(pl.program_id(1) == 0)
    def _():
        rows = tm // 4
        for r in range(0, tm, rows):
            h = _rms(x_ref[r:r + rows, :], g_ref[...]) * (1.0 + sc_ref[...]) + sh_ref[...]
            h_scr[r:r + rows, :] = h.astype(BF16)
        project(rows)

    @pl.when(pl.program_id(1) != 0)
    def _():
        project(tm)


def _in_proj(x2, mod3, g_pre, w_blk, tm):
    t, d = x2.shape
    nb, _, tn = w_blk.shape
    n = nb * tn
    per_b = SEQ // tm
    return pl.pallas_call(
        _inproj_kernel,
        grid=(t // tm, n // tn),
        in_specs=[pl.BlockSpec((tm, d), lambda i, j: (i, 0)),
                  pl.BlockSpec((None, 1, d), lambda i, j: ((i // per_b) * N_MOD + 1, 0, 0)),
                  pl.BlockSpec((None, 1, d), lambda i, j: ((i // per_b) * N_MOD + 0, 0, 0)),
                  pl.BlockSpec((1, d), lambda i, j: (0, 0)),
                  pl.BlockSpec((None, d, tn), lambda i, j: (j, 0, 0))],
        out_specs=pl.BlockSpec((tm, tn), lambda i, j: (i, j)),
        out_shape=jax.ShapeDtypeStruct((t, n), BF16),
        scratch_shapes=[pltpu.VMEM((tm, d), BF16)],
        compiler_params=_params("arbitrary", "arbitrary"),
        name="in_proj",
    )(x2, mod3, mod3, g_pre, w_blk)


def _qkv_kernel(ql_ref, kr_ref, kvl_ref, pos_ref, invf_ref, gq_ref, gkv_ref,
                wq_ref, wkv_ref, qT_ref, k_ref, vT_ref):
    tm = ql_ref.shape[0]
    ang = pos_ref[...].astype(F32) * invf_ref[...]
    lane = lax.broadcasted_iota(jnp.int32, (tm, LANES), 1)
    rope_lane = lane < QK_ROPE_DIM
    cos = jnp.where(rope_lane, jnp.cos(ang), 0.0)
    sin = jnp.where(rope_lane, jnp.sin(ang), 0.0)

    def rope(a):
        return a * cos + pltpu.roll(a, QK_ROPE_DIM, 1) * sin

    qn = _rms(ql_ref[...].astype(F32), gq_ref[...]).astype(BF16)
    q = jnp.dot(qn, wq_ref[...], preferred_element_type=F32)
    scale = LOG2_E / math.sqrt(QK_NOPE_DIM + QK_ROPE_DIM)
    kvn = _rms(kvl_ref[...].astype(F32), gkv_ref[...]).astype(BF16)
    kv = jnp.dot(kvn, wkv_ref[...], preferred_element_type=F32)
    k_rope = rope(kr_ref[...].astype(F32)).astype(BF16)
    v_off = MLA_HEADS * QK_NOPE_DIM
    ones = jnp.ones((VT_ROWS - V_HEAD_DIM, tm), BF16)
    for h in range(MLA_HEADS):
        lo = h * HEAD_QK
        qT_ref[lo:lo + LANES, :] = (q[:, lo:lo + LANES] * scale).T.astype(BF16)
        qT_ref[lo + LANES:lo + HEAD_QK, :] = (rope(q[:, lo + LANES:lo + HEAD_QK]) * scale).T.astype(BF16)
        k_ref[:, lo:lo + LANES] = kv[:, h * LANES:(h + 1) * LANES].astype(BF16)
        k_ref[:, lo + LANES:lo + HEAD_QK] = k_rope
        vlo = h * VT_ROWS
        vT_ref[vlo:vlo + V_HEAD_DIM, :] = kv[:, v_off + h * LANES:v_off + (h + 1) * LANES].T.astype(BF16)
        vT_ref[vlo + V_HEAD_DIM:vlo + VT_ROWS, :] = ones


def _latent_qkv(proj, pos2, invf, g_q, g_kv, wq_p, wkv_p, tm):
    t = proj.shape[0]
    hq = MLA_HEADS * HEAD_QK
    hv = MLA_HEADS * V_HEAD_DIM
    hvt = MLA_HEADS * VT_ROWS
    const = lambda i: (0, 0)
    return pl.pallas_call(
        _qkv_kernel,
        grid=(t // tm,),
        in_specs=[pl.BlockSpec((tm, Q_LORA_RANK), lambda i: (i, OFF_Q_LAT // Q_LORA_RANK)),
                  pl.BlockSpec((tm, LANES), lambda i: (i, OFF_K_ROPE // LANES)),
                  pl.BlockSpec((tm, KV_LORA_RANK), lambda i: (i, OFF_KV_LAT // KV_LORA_RANK)),
                  pl.BlockSpec((tm, 1), lambda i: (i, 0)),
                  pl.BlockSpec((1, LANES), const),
                  pl.BlockSpec((1, Q_LORA_RANK), const),
                  pl.BlockSpec((1, KV_LORA_RANK), const),
                  pl.BlockSpec((Q_LORA_RANK, hq), const),
                  pl.BlockSpec((KV_LORA_RANK, 2 * hv), const)],
        out_specs=[pl.BlockSpec((None, hq, tm), lambda i: (i, 0, 0)),
                   pl.BlockSpec((tm, hq), lambda i: (i, 0)),
                   pl.BlockSpec((None, hvt, tm), lambda i: (i, 0, 0))],
        out_shape=[jax.ShapeDtypeStruct((t // tm, hq, tm), BF16),
                   jax.ShapeDtypeStruct((t, hq), BF16),
                   jax.ShapeDtypeStruct((t // tm, hvt, tm), BF16)],
        compiler_params=_params("arbitrary"),
        name="latent_qkv",
    )(proj, proj, proj, pos2, invf, g_q, g_kv, wq_p, wkv_p)


def _causal_conv3(u, halo, w_ref, b_ref):
    w0, w1, w2, b = w_ref[0:1, :], w_ref[1:2, :], w_ref[2:3, :], b_ref[...]

    def taps(p2, p1, p0):
        return p2 * w0 + p1 * w1 + p0 * w2 + b

    body = taps(pltpu.roll(u, 2, 0), pltpu.roll(u, 1, 0), u)
    head = jnp.concatenate([halo, u[0:8, :]], axis=0)
    first = taps(pltpu.roll(head, 2, 0), pltpu.roll(head, 1, 0), head)[8:16, :]
    return jnp.concatenate([first, body[8:, :]], axis=0)


def _convgate_kernel(gb_ref, gc_ref, ci_ref, gch_ref, cih_ref, w_ref, b_ref, o_ref, *, per_b):
    first = (pl.program_id(0) % per_b) == 0
    g = gc_ref[...].astype(F32) * ci_ref[...].astype(F32)
    gh = gch_ref[...].astype(F32) * cih_ref[...].astype(F32)
    halo = jnp.where(first, 0.0, gh[8:16, :])
    o_ref[...] = (gb_ref[...].astype(F32) * _causal_conv3(g, halo, w_ref, b_ref)).astype(o_ref.dtype)


def _conv_gate(proj, conv_w, conv_b, tm):
    t = proj.shape[0]
    c = CONV_WIDTH
    hb = 16
    per_b = SEQ // tm
    halo_idx = lambda col: (lambda i: (jnp.maximum(i * (tm // hb) - 1, 0), col))
    return pl.pallas_call(
        functools.partial(_convgate_kernel, per_b=per_b),
        grid=(t // tm,),
        in_specs=[pl.BlockSpec((tm, c), lambda i: (i, OFF_GATE_B // c)),
                  pl.BlockSpec((tm, c), lambda i: (i, OFF_GATE_C // c)),
                  pl.BlockSpec((tm, c), lambda i: (i, OFF_CONV_IN // c)),
                  pl.BlockSpec((hb, c), halo_idx(OFF_GATE_C // c)),
                  pl.BlockSpec((hb, c), halo_idx(OFF_CONV_IN // c)),
                  pl.BlockSpec((3, c), lambda i: (0, 0)),
                  pl.BlockSpec((1, c), lambda i: (0, 0))],
        out_specs=pl.BlockSpec((tm, c), lambda i: (i, 0)),
        out_shape=jax.ShapeDtypeStruct((t, c), BF16),
        compiler_params=_params("arbitrary"),
        name="conv_gate",
    )(proj, proj, proj, proj, proj, conv_w, conv_b)


def _attn_kernel(qT_ref, k_ref, vT_ref, o_ref, *scratch, blk, heads):
    qi = pl.program_id(2)
    per_head = 4
    state = [scratch[per_head * hh:per_head * (hh + 1)] for hh in range(heads)]

    def q_rows(hh):
        return slice(hh * HEAD_QK, (hh + 1) * HEAD_QK)

    def scores(hh, j):
        start = pl.multiple_of(j * blk, blk)
        return jnp.dot(k_ref[pl.ds(start, blk), q_rows(hh)], qT_ref[q_rows(hh), :],
                       preferred_element_type=F32)

    def consume(hh, sT, j):
        _, _, m_scr, acc_scr = state[hh]
        m_old = m_scr[...]
        m_new = jnp.maximum(m_old, jnp.max(sT, axis=0, keepdims=True))
        alpha = jnp.exp2(m_old - m_new)
        pT = jnp.exp2(sT - m_new).astype(BF16)
        vT = vT_ref[j, hh * VT_ROWS:(hh + 1) * VT_ROWS, :]
        acc_scr[...] = alpha * acc_scr[...] + jnp.dot(vT, pT, preferred_element_type=F32)
        m_scr[...] = m_new

    def consume_diag(hh, s_ref):
        sT = s_ref[...]
        r = lax.broadcasted_iota(jnp.int32, sT.shape, 0)
        c = lax.broadcasted_iota(jnp.int32, sT.shape, 1)
        consume(hh, jnp.where(r <= c, sT, -1e30), qi)

    def pair(j):
        for hh in range(heads):
            sa, sb, _, _ = state[hh]
            sb[...] = scores(hh, j + 1)
            consume(hh, sa[...], j)
        for hh in range(heads):
            sa, sb, _, _ = state[hh]
            sa[...] = scores(hh, j + 2)
            consume(hh, sb[...], j + 1)

    for hh in range(heads):
        sa, _, m_scr, acc_scr = state[hh]
        m_scr[...] = jnp.full(m_scr.shape, -1e30, F32)
        acc_scr[...] = jnp.zeros(acc_scr.shape, F32)
        sa[...] = scores(hh, 0)

    def body(t, carry):
        pair(4 * t)
        pair(4 * t + 2)
        return carry

    lax.fori_loop(0, qi // 4, body, 0)
    rem = qi % 4

    @pl.when(rem >= 2)
    def _():
        pair(qi - rem)

    @pl.when(rem % 2 == 0)
    def _():
        for hh in range(heads):
            consume_diag(hh, state[hh][0])

    @pl.when(rem % 2 == 1)
    def _():
        for hh in range(heads):
            sa, sb, _, _ = state[hh]
            sb[...] = scores(hh, qi)
            consume(hh, sa[...], qi - 1)
        for hh in range(heads):
            consume_diag(hh, state[hh][1])

    for hh in range(heads):
        acc = state[hh][3][...]
        o_ref[:, hh * V_HEAD_DIM:(hh + 1) * V_HEAD_DIM] = (
            acc[:V_HEAD_DIM] / acc[V_HEAD_DIM:V_HEAD_DIM + 1]).T.astype(o_ref.dtype)


def _attention(qT, k, vT, blk, heads):
    t = k.shape[0]
    b = t // SEQ
    nq = SEQ // blk
    per_head_scratch = [pltpu.VMEM((blk, blk), F32),
                        pltpu.VMEM((blk, blk), F32),
                        pltpu.VMEM((1, blk), F32),
                        pltpu.VMEM((VT_ROWS, blk), F32)]
    return pl.pallas_call(
        functools.partial(_attn_kernel, blk=blk, heads=heads),
        grid=(b, MLA_HEADS // heads, nq),
        in_specs=[pl.BlockSpec((None, heads * HEAD_QK, blk), lambda bi, g, i: (bi * nq + i, g, 0)),
                  pl.BlockSpec((SEQ, heads * HEAD_QK), lambda bi, g, i: (bi, g)),
                  pl.BlockSpec((nq, heads * VT_ROWS, blk), lambda bi, g, i: (bi, g, 0))],
        out_specs=pl.BlockSpec((blk, heads * V_HEAD_DIM), lambda bi, g, i: (bi * nq + i, g)),
        out_shape=jax.ShapeDtypeStruct((t, MLA_HEADS * V_HEAD_DIM), BF16),
        scratch_shapes=per_head_scratch * heads,
        compiler_params=_params("arbitrary", "arbitrary", "arbitrary"),
        name="attention",
    )(qT, k, vT)


def _outproj_kernel(a_ref, c_ref, x_ref, gt_ref, g_ref, wa_ref, wc_ref, o_ref):
    mix = (jnp.dot(a_ref[...], wa_ref[...], preferred_element_type=F32)
           + jnp.dot(c_ref[...], wc_ref[...], preferred_element_type=F32))
    o_ref[...] = x_ref[...] + gt_ref[...] * _rms(mix, g_ref[...])


def _out_proj(attn, conv, x2, mod3, g_post, w_o_bf, tm):
    t, d = x2.shape
    ka = attn.shape[1]
    kc = conv.shape[1]
    per_b = SEQ // tm
    return pl.pallas_call(
        _outproj_kernel,
        grid=(t // tm,),
        in_specs=[pl.BlockSpec((tm, ka), lambda i: (i, 0)),
                  pl.BlockSpec((tm, kc), lambda i: (i, 0)),
                  pl.BlockSpec((tm, d), lambda i: (i, 0)),
                  pl.BlockSpec((None, 1, d), lambda i: ((i // per_b) * N_MOD + 2, 0, 0)),
                  pl.BlockSpec((1, d), lambda i: (0, 0)),
                  pl.BlockSpec((ka, d), lambda i: (0, 0)),
                  pl.BlockSpec((kc, d), lambda i: (1, 0))],
        out_specs=pl.BlockSpec((tm, d), lambda i: (i, 0)),
        out_shape=jax.ShapeDtypeStruct((t, d), F32),
        compiler_params=_params("arbitrary"),
        name="out_proj",
    )(attn, conv, x2, mod3, g_post, w_o_bf, w_o_bf)


def _ffn_kernel(x_ref, sc_ref, sh_ref, gt_ref, gpre_ref, gpost_ref, wa_ref, wg_ref,
                cwa_ref, cwg_ref, cba_ref, cbg_ref, wd_ref, o_ref,
                h_scr, halo_a, halo_g, y0_scr, y1_scr, *, per_b, nf, sub):
    i = pl.program_id(0)
    f = pl.program_id(1)
    tm = x_ref.shape[0]
    tf = wa_ref.shape[1]
    first = (i % per_b) == 0

    def branch(w_ref, cw_ref, cb_ref, halo_ref, cols):
        u = jnp.dot(h_scr[...], w_ref[:, cols], preferred_element_type=F32)
        halo = jnp.where(first, 0.0, halo_ref[f, :, cols])
        halo_ref[f, :, cols] = u[tm - 8:, :]
        return _causal_conv3(u, halo, cw_ref.at[:, cols], cb_ref.at[:, cols])

    def up(y_out):
        for s in range(0, tf, sub):
            cols = slice(s, s + sub)
            a = branch(wa_ref, cwa_ref, cba_ref, halo_a, cols)
            g = branch(wg_ref, cwg_ref, cbg_ref, halo_g, cols)
            y_out[:, cols] = (g * (1.0 / (1.0 + jnp.exp(-g))) * a).astype(BF16)

    def down(y_in):
        o_ref[...] += jnp.dot(y_in[...], wd_ref[...], preferred_element_type=F32)

    @pl.when(f == 0)
    def _():
        h = _rms(x_ref[...], gpre_ref[...]) * (1.0 + sc_ref[...]) + sh_ref[...]
        h_scr[...] = h.astype(BF16)
        o_ref[...] = jnp.zeros(o_ref.shape, F32)
        up(y0_scr)

    interior = jnp.logical_and(f > 0, f < nf)

    @pl.when(jnp.logical_and(interior, f % 2 == 1))
    def _():
        up(y1_scr)
        down(y0_scr)

    @pl.when(jnp.logical_and(interior, f % 2 == 0))
    def _():
        up(y0_scr)
        down(y1_scr)

    @pl.when(f == nf)
    def _():
        down(y1_scr if (nf - 1) % 2 == 1 else y0_scr)
        o_ref[...] = x_ref[...] + gt_ref[...] * _rms(o_ref[...], gpost_ref[...])


def _conv_ffn(x1, mod3, g_pre, g_post, w_up_bf, conv_w, conv_b, w_down_bf, tm):
    t, d = x1.shape
    tf = FFN_TF
    nf = D_FF // tf
    per_b = SEQ // tm
    modspec = lambda k: pl.BlockSpec((None, 1, d), lambda i, f: ((i // per_b) * N_MOD + k, 0, 0))
    up_f = lambda f: jnp.minimum(f, nf - 1)
    dn_f = lambda f: jnp.maximum(f - 1, 0)
    return pl.pallas_call(
        functools.partial(_ffn_kernel, per_b=per_b, nf=nf, sub=2 * LANES),
        grid=(t // tm, nf + 1),
        in_specs=[pl.BlockSpec((tm, d), lambda i, f: (i, 0)),
                  modspec(4), modspec(3), modspec(5),
                  pl.BlockSpec((1, d), lambda i, f: (0, 0)),
                  pl.BlockSpec((1, d), lambda i, f: (0, 0)),
                  pl.BlockSpec((d, tf), lambda i, f: (0, up_f(f))),
                  pl.BlockSpec((d, tf), lambda i, f: (0, nf + up_f(f))),
                  pl.BlockSpec((3, tf), lambda i, f: (0, up_f(f))),
                  pl.BlockSpec((3, tf), lambda i, f: (0, nf + up_f(f))),
                  pl.BlockSpec((1, tf), lambda i, f: (0, up_f(f))),
                  pl.BlockSpec((1, tf), lambda i, f: (0, nf + up_f(f))),
                  pl.BlockSpec((tf, d), lambda i, f: (dn_f(f), 0))],
        out_specs=pl.BlockSpec((tm, d), lambda i, f: (i, 0)),
        out_shape=jax.ShapeDtypeStruct((t, d), F32),
        scratch_shapes=[pltpu.VMEM((tm, d), BF16),
                        pltpu.VMEM((nf, 8, tf), F32),
                        pltpu.VMEM((nf, 8, tf), F32),
                        pltpu.VMEM((tm, tf), BF16),
                        pltpu.VMEM((tm, tf), BF16)],
        compiler_params=_params("arbitrary", "arbitrary"),
        name="conv_ffn",
    )(x1, mod3, mod3, mod3, g_pre, g_post, w_up_bf, w_up_bf, conv_w, conv_w,
      conv_b, conv_b, w_down_bf)


def _rot_half_cols(w):
    half = w.shape[-1] // 2
    return jnp.concatenate([-w[..., half:], w[..., :half]], axis=-1)


def _regroup_w_in(w):
    d = w.shape[0]
    c0 = Q_LORA_RANK
    c1 = c0 + KV_LORA_RANK
    c2 = c1 + QK_ROPE_DIM
    c3 = c2 + CONV_WIDTH
    c4 = c3 + CONV_WIDTH
    q_lat, kv_lat, k_rope = w[:, :c0], w[:, c0:c1], w[:, c1:c2]
    gate_b, gate_c, conv_in = w[:, c2:c3], w[:, c3:c4], w[:, c4:]
    pad = jnp.zeros((d, OFF_KV_LAT - OFF_K_ROPE - 2 * QK_ROPE_DIM), w.dtype)
    return jnp.concatenate([gate_b, gate_c, conv_in, q_lat, k_rope, _rot_half_cols(k_rope),
                            pad, kv_lat], axis=1)


def _block_cols(w, tn):
    k, n = w.shape
    return w.reshape(k, n // tn, tn).transpose(1, 0, 2)


def _regroup_w_uq(w):
    r = w.shape[0]
    w = w.reshape(r, MLA_HEADS, QK_NOPE_DIM + QK_ROPE_DIM)
    nope, rope = w[..., :QK_NOPE_DIM], w[..., QK_NOPE_DIM:]
    return jnp.concatenate([nope, rope, _rot_half_cols(rope)], axis=-1).reshape(r, MLA_HEADS * HEAD_QK)


def _regroup_w_ukv(w):
    r = w.shape[0]
    w = w.reshape(r, MLA_HEADS, QK_NOPE_DIM + V_HEAD_DIM)
    k = w[..., :QK_NOPE_DIM].reshape(r, MLA_HEADS * QK_NOPE_DIM)
    v = w[..., QK_NOPE_DIM:].reshape(r, MLA_HEADS * V_HEAD_DIM)
    return jnp.concatenate([k, v], axis=1)


def kernel(x, c, positions, w_ada, b_ada, g_pre_mix, g_post_mix, w_in, g_q, w_uq, g_kv, w_ukv,
           conv_w_mix, conv_b_mix, w_o, g_pre_ffn, g_post_ffn, w_up, conv_w_ffn, conv_b_ffn, w_down):
    b, s, d = x.shape
    assert (s, d) == (SEQ, D_MODEL) and w_ada.shape[0] == 1
    t = b * s
    x2 = x.reshape(t, d)
    pos2 = positions.reshape(t, 1)
    half = jnp.arange(0, QK_ROPE_DIM, 2, dtype=F32) / QK_ROPE_DIM
    inv_freq = 1.0 / (ROPE_THETA ** half)
    invf = jnp.concatenate([inv_freq, inv_freq, jnp.zeros((LANES - QK_ROPE_DIM,), F32)]).reshape(1, LANES)

    rows = 8
    c_pad = jnp.zeros((rows, d), F32).at[:b].set(c)
    x_cur = x2
    for l in range(w_ada.shape[0]):
        mod = _adaln_mod(c_pad, w_ada[l], b_ada[l].reshape(1, -1))
        mod3 = mod[:b].reshape(b * N_MOD, 1, d)

        w_in_blk = _block_cols(_regroup_w_in(w_in[l].astype(BF16)), IN_PROJ_TN)
        wq_p = _regroup_w_uq(w_uq[l]).astype(BF16)
        wkv_p = _regroup_w_ukv(w_ukv[l]).astype(BF16)

        proj = _in_proj(x_cur, mod3, g_pre_mix[l].reshape(1, d), w_in_blk, tm=1024)
        qT, k, vT = _latent_qkv(proj, pos2, invf, g_q[l].reshape(1, -1), g_kv[l].reshape(1, -1),
                                wq_p, wkv_p, tm=ATT_BLK)
        conv = _conv_gate(proj, conv_w_mix[l], conv_b_mix[l].reshape(1, -1), tm=512)
        attn = _attention(qT, k, vT, blk=ATT_BLK, heads=ATT_HEADS_PER_STEP)
        x_cur = _out_proj(attn, conv, x_cur, mod3, g_post_mix[l].reshape(1, d),
                          w_o[l].astype(BF16), tm=512)
        x_cur = _conv_ffn(x_cur, mod3, g_pre_ffn[l].reshape(1, d), g_post_ffn[l].reshape(1, d),
                          w_up[l].astype(BF16), conv_w_ffn[l],
                          conv_b_ffn[l].reshape(1, -1), w_down[l].astype(BF16), tm=1024)
    return x_cur.reshape(b, s, d)
```

```python
import functools
import math

import jax
import jax.numpy as jnp
from jax import lax
from jax.experimental import pallas as pl
from jax.experimental.pallas import tpu as pltpu

D_MODEL = 2048
SEQ = 8192
CONV_WIDTH = 1024
MLA_HEADS = 8
QK_NOPE_DIM = 128
QK_ROPE_DIM = 64
V_HEAD_DIM = 128
Q_LORA_RANK = 768
KV_LORA_RANK = 512
ROPE_THETA = 10000.0
D_FF = 5632
RMS_EPS = 1e-6
N_MOD = 6

LANES = 128
HEAD_QK = 2 * LANES
PROJ_COLS = 4608
VMEM_LIMIT = 56 * 1024 * 1024
ATT_BLK = 512
ATT_HEADS_PER_STEP = 2
VT_ROWS = V_HEAD_DIM + 16
LOG2_E = 1.4426950408889634
IN_PROJ_TN = 1536
FFN_TF = 512

OFF_GATE_B = 0
OFF_GATE_C = 1024
OFF_CONV_IN = 2048
OFF_Q_LAT = 3072
OFF_K_ROPE = 3840
OFF_KV_LAT = 4096

BF16 = jnp.bfloat16
F32 = jnp.float32


def _params(*sem):
    return pltpu.CompilerParams(dimension_semantics=sem, vmem_limit_bytes=VMEM_LIMIT)


def _rms(x, g):
    return x * lax.rsqrt(jnp.mean(x * x, axis=-1, keepdims=True) + RMS_EPS) * g


def _mod_kernel(c_ref, w_ref, b_ref, o_ref):
    c = c_ref[...]
    c_act = c * (1.0 / (1.0 + jnp.exp(-c)))
    o_ref[...] = jnp.dot(c_act.astype(BF16), w_ref[...].astype(BF16),
                         preferred_element_type=F32) + b_ref[...]


def _adaln_mod(c_pad, w_ada, b_ada):
    rows, d = c_pad.shape
    n = w_ada.shape[1]
    tn = 1536
    return pl.pallas_call(
        _mod_kernel,
        grid=(n // tn,),
        in_specs=[pl.BlockSpec((rows, d), lambda j: (0, 0)),
                  pl.BlockSpec((d, tn), lambda j: (0, j)),
                  pl.BlockSpec((1, tn), lambda j: (0, j))],
        out_specs=pl.BlockSpec((rows, tn), lambda j: (0, j)),
        out_shape=jax.ShapeDtypeStruct((rows, n), F32),
        compiler_params=_params("arbitrary"),
        name="adaln_mod",
    )(c_pad, w_ada, b_ada)


def _inproj_kernel(x_ref, sc_ref, sh_ref, g_ref, w_ref, o_ref, h_scr):
    tm = x_ref.shape[0]

    def project(rows):
        for r in range(0, tm, rows):
            o_ref[r:r + rows, :] = jnp.dot(h_scr[r:r + rows, :], w_ref[...],
                                           preferred_element_type=F32).astype(o_ref.dtype)

    @pl.when(pl.program_id(1) == 0)
    def _():
        rows = tm // 4
        for r in range(0, tm, rows):
            h = _rms(x_ref[r:r + rows, :], g_ref[...] * (1.0 + sc_ref[...])) + sh_ref[...]
            h_scr[r:r + rows, :] = h.astype(BF16)
        project(rows)

    @pl.when(pl.program_id(1) != 0)
    def _():
        project(tm)


def _in_proj(x2, mod3, g_pre, w_blk, tm):
    t, d = x2.shape
    nb, _, tn = w_blk.shape
    n = nb * tn
    per_b = SEQ // tm
    return pl.pallas_call(
        _inproj_kernel,
        grid=(t // tm, n // tn),
        in_specs=[pl.BlockSpec((tm, d), lambda i, j: (i, 0)),
                  pl.BlockSpec((None, 1, d), lambda i, j: ((i // per_b) * N_MOD + 1, 0, 0)),
                  pl.BlockSpec((None, 1, d), lambda i, j: ((i // per_b) * N_MOD + 0, 0, 0)),
                  pl.BlockSpec((1, d), lambda i, j: (0, 0)),
                  pl.BlockSpec((None, d, tn), lambda i, j: (j, 0, 0))],
        out_specs=pl.BlockSpec((tm, tn), lambda i, j: (i, j)),
        out_shape=jax.ShapeDtypeStruct((t, n), BF16),
        scratch_shapes=[pltpu.VMEM((tm, d), BF16)],
        compiler_params=_params("arbitrary", "arbitrary"),
        name="in_proj",
    )(x2, mod3, mod3, g_pre, w_blk)


def _qkv_kernel(ql_ref, kr_ref, kvl_ref, pos_ref, invf_ref, gq_ref, gkv_ref,
                wq_ref, wkv_ref, qT_ref, k_ref, vT_ref):
    tm = ql_ref.shape[0]
    ang = pos_ref[...].astype(F32) * invf_ref[...]
    lane = lax.broadcasted_iota(jnp.int32, (tm, LANES), 1)
    rope_lane = lane < QK_ROPE_DIM
    cos = jnp.where(rope_lane, jnp.cos(ang), 0.0)
    sin = jnp.where(rope_lane, jnp.sin(ang), 0.0)

    def rope(a):
        return a * cos + pltpu.roll(a, QK_ROPE_DIM, 1) * sin

    qn = _rms(ql_ref[...].astype(F32), gq_ref[...]).astype(BF16)
    q = jnp.dot(qn, wq_ref[...], preferred_element_type=F32)
    scale = LOG2_E / math.sqrt(QK_NOPE_DIM + QK_ROPE_DIM)
    kvn = _rms(kvl_ref[...].astype(F32), gkv_ref[...]).astype(BF16)
    kv = jnp.dot(kvn, wkv_ref[...], preferred_element_type=F32)
    k_rope = rope(kr_ref[...].astype(F32)).astype(BF16)
    v_off = MLA_HEADS * QK_NOPE_DIM
    ones = jnp.ones((VT_ROWS - V_HEAD_DIM, tm), BF16)
    for h in range(MLA_HEADS):
        lo = h * HEAD_QK
        qT_ref[lo:lo + LANES, :] = (q[:, lo:lo + LANES] * scale).T.astype(BF16)
        qT_ref[lo + LANES:lo + HEAD_QK, :] = (rope(q[:, lo + LANES:lo + HEAD_QK]) * scale).T.astype(BF16)
        k_ref[:, lo:lo + LANES] = kv[:, h * LANES:(h + 1) * LANES].astype(BF16)
        k_ref[:, lo + LANES:lo + HEAD_QK] = k_rope
        vlo = h * VT_ROWS
        vT_ref[vlo:vlo + V_HEAD_DIM, :] = kv[:, v_off + h * LANES:v_off + (h + 1) * LANES].T.astype(BF16)
        vT_ref[vlo + V_HEAD_DIM:vlo + VT_ROWS, :] = ones


def _latent_qkv(proj, pos2, invf, g_q, g_kv, wq_p, wkv_p, tm):
    t = proj.shape[0]
    hq = MLA_HEADS * HEAD_QK
    hv = MLA_HEADS * V_HEAD_DIM
    hvt = MLA_HEADS * VT_ROWS
    const = lambda i: (0, 0)
    return pl.pallas_call(
        _qkv_kernel,
        grid=(t // tm,),
        in_specs=[pl.BlockSpec((tm, Q_LORA_RANK), lambda i: (i, OFF_Q_LAT // Q_LORA_RANK)),
                  pl.BlockSpec((tm, LANES), lambda i: (i, OFF_K_ROPE // LANES)),
                  pl.BlockSpec((tm, KV_LORA_RANK), lambda i: (i, OFF_KV_LAT // KV_LORA_RANK)),
                  pl.BlockSpec((tm, 1), lambda i: (i, 0)),
                  pl.BlockSpec((1, LANES), const),
                  pl.BlockSpec((1, Q_LORA_RANK), const),
                  pl.BlockSpec((1, KV_LORA_RANK), const),
                  pl.BlockSpec((Q_LORA_RANK, hq), const),
                  pl.BlockSpec((KV_LORA_RANK, 2 * hv), const)],
        out_specs=[pl.BlockSpec((None, hq, tm), lambda i: (i, 0, 0)),
                   pl.BlockSpec((tm, hq), lambda i: (i, 0)),
                   pl.BlockSpec((None, hvt, tm), lambda i: (i, 0, 0))],
        out_shape=[jax.ShapeDtypeStruct((t // tm, hq, tm), BF16),
                   jax.ShapeDtypeStruct((t, hq), BF16),
                   jax.ShapeDtypeStruct((t // tm, hvt, tm), BF16)],
        compiler_params=_params("arbitrary"),
        name="latent_qkv",
    )(proj, proj, proj, pos2, invf, g_q, g_kv, wq_p, wkv_p)


def _causal_conv3(u, halo, w_ref, b_ref):
    w0, w1, w2, b = w_ref[0:1, :], w_ref[1:2, :], w_ref[2:3, :], b_ref[...]

    def taps(p2, p1, p0):
        return p2 * w0 + p1 * w1 + p0 * w2 + b

    body = taps(pltpu.roll(u, 2, 0), pltpu.roll(u, 1, 0), u)
    head = jnp.concatenate([halo, u[0:8, :]], axis=0)
    first = taps(pltpu.roll(head, 2, 0), pltpu.roll(head, 1, 0), head)[8:16, :]
    return jnp.concatenate([first, body[8:, :]], axis=0)


def _attn_kernel(qT_ref, k_ref, vT_ref, o_ref, *scratch, blk, heads):
    qi = pl.program_id(2)
    per_head = 4
    state = [scratch[per_head * hh:per_head * (hh + 1)] for hh in range(heads)]

    def q_rows(hh):
        return slice(hh * HEAD_QK, (hh + 1) * HEAD_QK)

    def scores(hh, j):
        start = pl.multiple_of(j * blk, blk)
        return jnp.dot(k_ref[pl.ds(start, blk), q_rows(hh)], qT_ref[q_rows(hh), :],
                       preferred_element_type=F32)

    def consume(hh, sT, j):
        _, _, m_scr, acc_scr = state[hh]
        m_old = m_scr[...]
        m_new = jnp.maximum(m_old, jnp.max(sT, axis=0, keepdims=True))
        alpha = jnp.exp2(m_old - m_new)
        pT = jnp.exp2(sT - m_new).astype(BF16)
        vT = vT_ref[j, hh * VT_ROWS:(hh + 1) * VT_ROWS, :]
        acc_scr[...] = alpha * acc_scr[...] + jnp.dot(vT, pT, preferred_element_type=F32)
        m_scr[...] = m_new

    def consume_diag(hh, s_ref):
        sT = s_ref[...]
        r = lax.broadcasted_iota(jnp.int32, sT.shape, 0)
        c = lax.broadcasted_iota(jnp.int32, sT.shape, 1)
        consume(hh, jnp.where(r <= c, sT, -1e30), qi)

    def pair(j):
        for hh in range(heads):
            sa, sb, _, _ = state[hh]
            sb[...] = scores(hh, j + 1)
            consume(hh, sa[...], j)
        for hh in range(heads):
            sa, sb, _, _ = state[hh]
            sa[...] = scores(hh, j + 2)
            consume(hh, sb[...], j + 1)

    for hh in range(heads):
        sa, _, m_scr, acc_scr = state[hh]
        m_scr[...] = jnp.full(m_scr.shape, -1e30, F32)
        acc_scr[...] = jnp.zeros(acc_scr.shape, F32)
        sa[...] = scores(hh, 0)

    def body(t, carry):
        pair(4 * t)
        pair(4 * t + 2)
        return carry

    lax.fori_loop(0, qi // 4, body, 0)
    rem = qi % 4

    @pl.when(rem >= 2)
    def _():
        pair(qi - rem)

    @pl.when(rem % 2 == 0)
    def _():
        for hh in range(heads):
            consume_diag(hh, state[hh][0])

    @pl.when(rem % 2 == 1)
    def _():
        for hh in range(heads):
            sa, sb, _, _ = state[hh]
            sb[...] = scores(hh, qi)
            consume(hh, sa[...], qi - 1)
        for hh in range(heads):
            consume_diag(hh, state[hh][1])

    for hh in range(heads):
        acc = state[hh][3][...]
        o_ref[:, hh * V_HEAD_DIM:(hh + 1) * V_HEAD_DIM] = (
            acc[:V_HEAD_DIM] / acc[V_HEAD_DIM:V_HEAD_DIM + 1]).T.astype(o_ref.dtype)


def _attention(qT, k, vT, blk, heads):
    t = k.shape[0]
    b = t // SEQ
    nq = SEQ // blk
    per_head_scratch = [pltpu.VMEM((blk, blk), F32),
                        pltpu.VMEM((blk, blk), F32),
                        pltpu.VMEM((1, blk), F32),
                        pltpu.VMEM((VT_ROWS, blk), F32)]
    return pl.pallas_call(
        functools.partial(_attn_kernel, blk=blk, heads=heads),
        grid=(b, MLA_HEADS // heads, nq),
        in_specs=[pl.BlockSpec((None, heads * HEAD_QK, blk), lambda bi, g, i: (bi * nq + i, g, 0)),
                  pl.BlockSpec((SEQ, heads * HEAD_QK), lambda bi, g, i: (bi, g)),
                  pl.BlockSpec((nq, heads * VT_ROWS, blk), lambda bi, g, i: (bi, g, 0))],
        out_specs=pl.BlockSpec((blk, heads * V_HEAD_DIM), lambda bi, g, i: (bi * nq + i, g)),
        out_shape=jax.ShapeDtypeStruct((t, MLA_HEADS * V_HEAD_DIM), BF16),
        scratch_shapes=per_head_scratch * heads,
        compiler_params=_params("arbitrary", "arbitrary", "arbitrary"),
        name="attention",
    )(qT, k, vT)


def _outproj_kernel(a_ref, gb_ref, gc_ref, ci_ref, gch_ref, cih_ref, cw_ref, cb_ref,
                    x_ref, gt_ref, g_ref, wa_ref, wc_ref, o_ref, *, per_b):
    first = (pl.program_id(0) % per_b) == 0
    g = gc_ref[...].astype(F32) * ci_ref[...].astype(F32)
    gh = gch_ref[...].astype(F32) * cih_ref[...].astype(F32)
    halo = jnp.where(first, 0.0, gh[8:16, :])
    conv = (gb_ref[...].astype(F32) * _causal_conv3(g, halo, cw_ref, cb_ref)).astype(BF16)
    mix = (jnp.dot(a_ref[...], wa_ref[...], preferred_element_type=F32)
           + jnp.dot(conv, wc_ref[...], preferred_element_type=F32))
    o_ref[...] = x_ref[...] + _rms(mix, gt_ref[...] * g_ref[...])


def _out_proj(attn, proj, conv_w, conv_b, x2, mod3, g_post, w_o_bf, tm):
    t, d = x2.shape
    ka = attn.shape[1]
    c = CONV_WIDTH
    hb = 16
    per_b = SEQ // tm
    halo_idx = lambda col: (lambda i: (jnp.maximum(i * (tm // hb) - 1, 0), col))
    return pl.pallas_call(
        functools.partial(_outproj_kernel, per_b=per_b),
        grid=(t // tm,),
        in_specs=[pl.BlockSpec((tm, ka), lambda i: (i, 0)),
                  pl.BlockSpec((tm, c), lambda i: (i, OFF_GATE_B // c)),
                  pl.BlockSpec((tm, c), lambda i: (i, OFF_GATE_C // c)),
                  pl.BlockSpec((tm, c), lambda i: (i, OFF_CONV_IN // c)),
                  pl.BlockSpec((hb, c), halo_idx(OFF_GATE_C // c)),
                  pl.BlockSpec((hb, c), halo_idx(OFF_CONV_IN // c)),
                  pl.BlockSpec((3, c), lambda i: (0, 0)),
                  pl.BlockSpec((1, c), lambda i: (0, 0)),
                  pl.BlockSpec((tm, d), lambda i: (i, 0)),
                  pl.BlockSpec((None, 1, d), lambda i: ((i // per_b) * N_MOD + 2, 0, 0)),
                  pl.BlockSpec((1, d), lambda i: (0, 0)),
                  pl.BlockSpec((ka, d), lambda i: (0, 0)),
                  pl.BlockSpec((c, d), lambda i: (ka // c, 0))],
        out_specs=pl.BlockSpec((tm, d), lambda i: (i, 0)),
        out_shape=jax.ShapeDtypeStruct((t, d), F32),
        compiler_params=_params("arbitrary"),
        name="out_proj",
    )(attn, proj, proj, proj, proj, proj, conv_w, conv_b, x2, mod3, g_post, w_o_bf, w_o_bf)


def _ffn_kernel(x_ref, sc_ref, sh_ref, gt_ref, gpre_ref, gpost_ref, wa_ref, wg_ref,
                cwa_ref, cwg_ref, cba_ref, cbg_ref, wd_ref, o_ref,
                h_scr, halo_a, halo_g, y0_scr, y1_scr, *, per_b, nf, sub):
    i = pl.program_id(0)
    f = pl.program_id(1)
    tm = x_ref.shape[0]
    tf = wa_ref.shape[1]
    first = (i % per_b) == 0

    def branch(w_ref, cw_ref, cb_ref, halo_ref, cols):
        u = jnp.dot(h_scr[...], w_ref[:, cols], preferred_element_type=F32)
        halo = jnp.where(first, 0.0, halo_ref[f, :, cols])
        halo_ref[f, :, cols] = u[tm - 8:, :]
        return _causal_conv3(u, halo, cw_ref.at[:, cols], cb_ref.at[:, cols])

    def up(y_out):
        for s in range(0, tf, sub):
            cols = slice(s, s + sub)
            a = branch(wa_ref, cwa_ref, cba_ref, halo_a, cols)
            g = branch(wg_ref, cwg_ref, cbg_ref, halo_g, cols)
            hg = 0.5 * g
            y_out[:, cols] = ((hg + hg * jnp.tanh(hg)) * a).astype(BF16)

    def down(y_in):
        o_ref[...] += jnp.dot(y_in[...], wd_ref[...], preferred_element_type=F32)

    @pl.when(f == 0)
    def _():
        h = _rms(x_ref[...], gpre_ref[...] * (1.0 + sc_ref[...])) + sh_ref[...]
        h_scr[...] = h.astype(BF16)
        o_ref[...] = jnp.zeros(o_ref.shape, F32)
        up(y0_scr)

    interior = jnp.logical_and(f > 0, f < nf)

    @pl.when(jnp.logical_and(interior, f % 2 == 1))
    def _():
        up(y1_scr)
        down(y0_scr)

    @pl.when(jnp.logical_and(interior, f % 2 == 0))
    def _():
        up(y0_scr)
        down(y1_scr)

    @pl.when(f == nf)
    def _():
        down(y1_scr if (nf - 1) % 2 == 1 else y0_scr)
        o_ref[...] = x_ref[...] + _rms(o_ref[...], gt_ref[...] * gpost_ref[...])


def _conv_ffn(x1, mod3, g_pre, g_post, w_up_bf, conv_w, conv_b, w_down_bf, tm):
    t, d = x1.shape
    tf = FFN_TF
    nf = D_FF // tf
    per_b = SEQ // tm
    modspec = lambda k: pl.BlockSpec((None, 1, d), lambda i, f: ((i // per_b) * N_MOD + k, 0, 0))
    up_f = lambda f: jnp.minimum(f, nf - 1)
    dn_f = lambda f: jnp.maximum(f - 1, 0)
    return pl.pallas_call(
        functools.partial(_ffn_kernel, per_b=per_b, nf=nf, sub=2 * LANES),
        grid=(t // tm, nf + 1),
        in_specs=[pl.BlockSpec((tm, d), lambda i, f: (i, 0)),
                  modspec(4), modspec(3), modspec(5),
                  pl.BlockSpec((1, d), lambda i, f: (0, 0)),
                  pl.BlockSpec((1, d), lambda i, f: (0, 0)),
                  pl.BlockSpec((d, tf), lambda i, f: (0, up_f(f))),
                  pl.BlockSpec((d, tf), lambda i, f: (0, nf + up_f(f))),
                  pl.BlockSpec((3, tf), lambda i, f: (0, up_f(f))),
                  pl.BlockSpec((3, tf), lambda i, f: (0, nf + up_f(f))),
                  pl.BlockSpec((1, tf), lambda i, f: (0, up_f(f))),
                  pl.BlockSpec((1, tf), lambda i, f: (0, nf + up_f(f))),
                  pl.BlockSpec((tf, d), lambda i, f: (dn_f(f), 0))],
        out_specs=pl.BlockSpec((tm, d), lambda i, f: (i, 0)),
        out_shape=jax.ShapeDtypeStruct((t, d), F32),
        scratch_shapes=[pltpu.VMEM((tm, d), BF16),
                        pltpu.VMEM((nf, 8, tf), F32),
                        pltpu.VMEM((nf, 8, tf), F32),
                        pltpu.VMEM((tm, tf), BF16),
                        pltpu.VMEM((tm, tf), BF16)],
        compiler_params=_params("arbitrary", "arbitrary"),
        name="conv_ffn",
    )(x1, mod3, mod3, mod3, g_pre, g_post, w_up_bf, w_up_bf, conv_w, conv_w,
      conv_b, conv_b, w_down_bf)


def _rot_half_cols(w):
    half = w.shape[-1] // 2
    return jnp.concatenate([-w[..., half:], w[..., :half]], axis=-1)


def _regroup_w_in(w):
    d = w.shape[0]
    c0 = Q_LORA_RANK
    c1 = c0 + KV_LORA_RANK
    c2 = c1 + QK_ROPE_DIM
    c3 = c2 + CONV_WIDTH
    c4 = c3 + CONV_WIDTH
    q_lat, kv_lat, k_rope = w[:, :c0], w[:, c0:c1], w[:, c1:c2]
    gate_b, gate_c, conv_in = w[:, c2:c3], w[:, c3:c4], w[:, c4:]
    pad = jnp.zeros((d, OFF_KV_LAT - OFF_K_ROPE - 2 * QK_ROPE_DIM), w.dtype)
    return jnp.concatenate([gate_b, gate_c, conv_in, q_lat, k_rope, _rot_half_cols(k_rope),
                            pad, kv_lat], axis=1)


def _block_cols(w, tn):
    k, n = w.shape
    return w.reshape(k, n // tn, tn).transpose(1, 0, 2)


def _regroup_w_uq(w):
    r = w.shape[0]
    w = w.reshape(r, MLA_HEADS, QK_NOPE_DIM + QK_ROPE_DIM)
    nope, rope = w[..., :QK_NOPE_DIM], w[..., QK_NOPE_DIM:]
    return jnp.concatenate([nope, rope, _rot_half_cols(rope)], axis=-1).reshape(r, MLA_HEADS * HEAD_QK)


def _regroup_w_ukv(w):
    r = w.shape[0]
    w = w.reshape(r, MLA_HEADS, QK_NOPE_DIM + V_HEAD_DIM)
    k = w[..., :QK_NOPE_DIM].reshape(r, MLA_HEADS * QK_NOPE_DIM)
    v = w[..., QK_NOPE_DIM:].reshape(r, MLA_HEADS * V_HEAD_DIM)
    return jnp.concatenate([k, v], axis=1)


def kernel(x, c, positions, w_ada, b_ada, g_pre_mix, g_post_mix, w_in, g_q, w_uq, g_kv, w_ukv,
           conv_w_mix, conv_b_mix, w_o, g_pre_ffn, g_post_ffn, w_up, conv_w_ffn, conv_b_ffn, w_down):
    b, s, d = x.shape
    assert (s, d) == (SEQ, D_MODEL) and w_ada.shape[0] == 1
    t = b * s
    x2 = x.reshape(t, d)
    pos2 = positions.reshape(t, 1)
    half = jnp.arange(0, QK_ROPE_DIM, 2, dtype=F32) / QK_ROPE_DIM
    inv_freq = 1.0 / (ROPE_THETA ** half)
    invf = jnp.concatenate([inv_freq, inv_freq, jnp.zeros((LANES - QK_ROPE_DIM,), F32)]).reshape(1, LANES)

    rows = 8
    c_pad = jnp.zeros((rows, d), F32).at[:b].set(c)
    x_cur = x2
    for l in range(w_ada.shape[0]):
        mod = _adaln_mod(c_pad, w_ada[l], b_ada[l].reshape(1, -1))
        mod3 = mod[:b].reshape(b * N_MOD, 1, d)

        w_in_blk = _block_cols(_regroup_w_in(w_in[l].astype(BF16)), IN_PROJ_TN)
        wq_p = _regroup_w_uq(w_uq[l]).astype(BF16)
        wkv_p = _regroup_w_ukv(w_ukv[l]).astype(BF16)

        proj = _in_proj(x_cur, mod3, g_pre_mix[l].reshape(1, d), w_in_blk, tm=1024)
        qT, k, vT = _latent_qkv(proj, pos2, invf, g_q[l].reshape(1, -1), g_kv[l].reshape(1, -1),
                                wq_p, wkv_p, tm=ATT_BLK)
        attn = _attention(qT, k, vT, blk=ATT_BLK, heads=ATT_HEADS_PER_STEP)
        x_cur = _out_proj(attn, proj, conv_w_mix[l], conv_b_mix[l].reshape(1, -1), x_cur, mod3,
                          g_post_mix[l].reshape(1, d), w_o[l].astype(BF16), tm=512)
        x_cur = _conv_ffn(x_cur, mod3, g_pre_ffn[l].reshape(1, d), g_post_ffn[l].reshape(1, d),
                          w_up[l].astype(BF16), conv_w_ffn[l],
                          conv_b_ffn[l].reshape(1, -1), w_down[l].astype(BF16), tm=1024)
    return x_cur.reshape(b, s, d)
```

```python
import functools
import math

import jax
import jax.numpy as jnp
from jax import lax
from jax.experimental import pallas as pl
from jax.experimental.pallas import tpu as pltpu

D_MODEL = 2048
SEQ = 8192
CONV_WIDTH = 1024
MLA_HEADS = 8
QK_NOPE_DIM = 128
QK_ROPE_DIM = 64
V_HEAD_DIM = 128
Q_LORA_RANK = 768
KV_LORA_RANK = 512
ROPE_THETA = 10000.0
D_FF = 5632
RMS_EPS = 1e-6
N_MOD = 6

LANES = 128
HEAD_QK = 2 * LANES
PROJ_COLS = 4608
VMEM_LIMIT = 56 * 1024 * 1024
ATT_BLK = 512
ATT_HEADS_PER_STEP = 2
VT_ROWS = V_HEAD_DIM + 16
LOG2_E = 1.4426950408889634
IN_PROJ_TN = 1536
EDGE_ROWS = 256
FFN_TF = 512

OFF_GATE_B = 0
OFF_GATE_C = 1024
OFF_CONV_IN = 2048
OFF_Q_LAT = 3072
OFF_K_ROPE = 3840
OFF_KV_LAT = 4096

BF16 = jnp.bfloat16
F32 = jnp.float32


def _params(*sem):
    return pltpu.CompilerParams(dimension_semantics=sem, vmem_limit_bytes=VMEM_LIMIT)


def _rms(x, g):
    return x * lax.rsqrt(jnp.mean(x * x, axis=-1, keepdims=True) + RMS_EPS) * g


def _mod_kernel(c_ref, w_ref, b_ref, o_ref):
    c = c_ref[...]
    c_act = c * (1.0 / (1.0 + jnp.exp(-c)))
    o_ref[...] = jnp.dot(c_act.astype(BF16), w_ref[...].astype(BF16),
                         preferred_element_type=F32) + b_ref[...]


def _adaln_mod(c_pad, w_ada, b_ada):
    rows, d = c_pad.shape
    n = w_ada.shape[1]
    tn = 1536
    return pl.pallas_call(
        _mod_kernel,
        grid=(n // tn,),
        in_specs=[pl.BlockSpec((rows, d), lambda j: (0, 0)),
                  pl.BlockSpec((d, tn), lambda j: (0, j)),
                  pl.BlockSpec((1, tn), lambda j: (0, j))],
        out_specs=pl.BlockSpec((rows, tn), lambda j: (0, j)),
        out_shape=jax.ShapeDtypeStruct((rows, n), F32),
        compiler_params=_params("arbitrary"),
        name="adaln_mod",
    )(c_pad, w_ada, b_ada)


def _inproj_kernel(x_ref, sc_ref, sh_ref, g_ref, w_ref, o_ref, h_scr):
    tm = x_ref.shape[0]

    def project(rows):
        for r in range(0, tm, rows):
            o_ref[r:r + rows, :] = jnp.dot(h_scr[r:r + rows, :], w_ref[...],
                                           preferred_element_type=F32).astype(o_ref.dtype)

    @pl.when(pl.program_id(1) == 0)
    def _():
        rows = tm // 4
        for r in range(0, tm, rows):
            h = _rms(x_ref[r:r + rows, :], g_ref[...] * (1.0 + sc_ref[...])) + sh_ref[...]
            h_scr[r:r + rows, :] = h.astype(BF16)
        project(rows)

    @pl.when(pl.program_id(1) != 0)
    def _():
        project(tm)


def _in_proj(x2, mod3, g_pre, w_blk, tm):
    t, d = x2.shape
    nb, _, tn = w_blk.shape
    n = nb * tn
    per_b = SEQ // tm
    return pl.pallas_call(
        _inproj_kernel,
        grid=(t // tm, n // tn),
        in_specs=[pl.BlockSpec((tm, d), lambda i, j: (i, 0)),
                  pl.BlockSpec((None, 1, d), lambda i, j: ((i // per_b) * N_MOD + 1, 0, 0)),
                  pl.BlockSpec((None, 1, d), lambda i, j: ((i // per_b) * N_MOD + 0, 0, 0)),
                  pl.BlockSpec((1, d), lambda i, j: (0, 0)),
                  pl.BlockSpec((None, d, tn), lambda i, j: (j, 0, 0))],
        out_specs=pl.BlockSpec((tm, tn), lambda i, j: (i, j)),
        out_shape=jax.ShapeDtypeStruct((t, n), BF16),
        scratch_shapes=[pltpu.VMEM((tm, d), BF16)],
        compiler_params=_params("arbitrary", "arbitrary"),
        name="in_proj",
    )(x2, mod3, mod3, g_pre, w_blk)


def _qkv_kernel(ql_ref, kr_ref, kvl_ref, pos_ref, invf_ref, gq_ref, gkv_ref,
                wq_ref, wkv_ref, qT_ref, k_ref, vT_ref):
    tm = ql_ref.shape[0]
    ang = invf_ref[...] * pos_ref[...].astype(F32)
    cos = jnp.cos(ang)
    sin = jnp.sin(ang)
    cos2 = jnp.concatenate([cos, cos], axis=0)
    sin2 = jnp.concatenate([sin, sin], axis=0)

    def rope_t(a):
        at = a.T
        return at[:QK_ROPE_DIM, :] * cos2 + at[QK_ROPE_DIM:, :] * sin2

    qn = _rms(ql_ref[...].astype(F32), gq_ref[...]).astype(BF16)
    q = jnp.dot(qn, wq_ref[...], preferred_element_type=F32)
    scale = LOG2_E / math.sqrt(QK_NOPE_DIM + QK_ROPE_DIM)
    kvn = _rms(kvl_ref[...].astype(F32), gkv_ref[...]).astype(BF16)
    kv = jnp.dot(kvn, wkv_ref[...], preferred_element_type=F32)
    zeros_t = jnp.zeros((LANES - QK_ROPE_DIM, tm), F32)
    k_rope = jnp.concatenate([rope_t(kr_ref[...].astype(F32)), zeros_t], axis=0).T.astype(BF16)
    v_off = MLA_HEADS * QK_NOPE_DIM
    ones = jnp.ones((VT_ROWS - V_HEAD_DIM, tm), BF16)
    for h in range(MLA_HEADS):
        lo = h * HEAD_QK
        qT_ref[lo:lo + LANES, :] = (q[:, lo:lo + LANES] * scale).T.astype(BF16)
        qT_ref[lo + LANES:lo + LANES + QK_ROPE_DIM, :] = (rope_t(q[:, lo + LANES:lo + HEAD_QK]) * scale).astype(BF16)
        qT_ref[lo + LANES + QK_ROPE_DIM:lo + HEAD_QK, :] = zeros_t.astype(BF16)
        k_ref[:, lo:lo + LANES] = kv[:, h * LANES:(h + 1) * LANES].astype(BF16)
        k_ref[:, lo + LANES:lo + HEAD_QK] = k_rope
        vlo = h * VT_ROWS
        vT_ref[vlo:vlo + V_HEAD_DIM, :] = kv[:, v_off + h * LANES:v_off + (h + 1) * LANES].T.astype(BF16)
        vT_ref[vlo + V_HEAD_DIM:vlo + VT_ROWS, :] = ones


def _latent_qkv(proj, pos3, invf, g_q, g_kv, wq_p, wkv_p, tm):
    t = proj.shape[0]
    hq = MLA_HEADS * HEAD_QK
    hv = MLA_HEADS * V_HEAD_DIM
    hvt = MLA_HEADS * VT_ROWS
    const = lambda i: (0, 0)
    return pl.pallas_call(
        _qkv_kernel,
        grid=(t // tm,),
        in_specs=[pl.BlockSpec((tm, Q_LORA_RANK), lambda i: (i, OFF_Q_LAT // Q_LORA_RANK)),
                  pl.BlockSpec((tm, LANES), lambda i: (i, OFF_K_ROPE // LANES)),
                  pl.BlockSpec((tm, KV_LORA_RANK), lambda i: (i, OFF_KV_LAT // KV_LORA_RANK)),
                  pl.BlockSpec((None, 1, tm), lambda i: (i, 0, 0)),
                  pl.BlockSpec((QK_ROPE_DIM // 2, 1), const),
                  pl.BlockSpec((1, Q_LORA_RANK), const),
                  pl.BlockSpec((1, KV_LORA_RANK), const),
                  pl.BlockSpec((Q_LORA_RANK, hq), const),
                  pl.BlockSpec((KV_LORA_RANK, 2 * hv), const)],
        out_specs=[pl.BlockSpec((None, hq, tm), lambda i: (i, 0, 0)),
                   pl.BlockSpec((tm, hq), lambda i: (i, 0)),
                   pl.BlockSpec((None, hvt, tm), lambda i: (i, 0, 0))],
        out_shape=[jax.ShapeDtypeStruct((t // tm, hq, tm), BF16),
                   jax.ShapeDtypeStruct((t, hq), BF16),
                   jax.ShapeDtypeStruct((t // tm, hvt, tm), BF16)],
        compiler_params=_params("arbitrary"),
        name="latent_qkv",
    )(proj, proj, proj, pos3, invf, g_q, g_kv, wq_p, wkv_p)


def _causal_conv3(u, halo, w_ref, b_ref):
    w0, w1, w2, b = w_ref[0:1, :], w_ref[1:2, :], w_ref[2:3, :], b_ref[...]

    def taps(p2, p1, p0):
        return p2 * w0 + p1 * w1 + p0 * w2 + b

    body = taps(pltpu.roll(u, 2, 0), pltpu.roll(u, 1, 0), u)
    head = jnp.concatenate([halo, u[0:8, :]], axis=0)
    first = taps(pltpu.roll(head, 2, 0), pltpu.roll(head, 1, 0), head)[8:16, :]
    return jnp.concatenate([first, body[8:, :]], axis=0)


def _attn_kernel(qT_ref, k_ref, vT_ref, o_ref, *scratch, blk, heads):
    qi = pl.program_id(2)
    per_head = 4
    state = [scratch[per_head * hh:per_head * (hh + 1)] for hh in range(heads)]

    def q_rows(hh):
        return slice(hh * HEAD_QK, (hh + 1) * HEAD_QK)

    def scores(hh, j):
        start = pl.multiple_of(j * blk, blk)
        return jnp.dot(k_ref[pl.ds(start, blk), q_rows(hh)], qT_ref[q_rows(hh), :],
                       preferred_element_type=F32)

    def consume(hh, sT, j):
        _, _, m_scr, acc_scr = state[hh]
        m_old = m_scr[...]
        m_new = jnp.maximum(m_old, jnp.max(sT, axis=0, keepdims=True))
        alpha = jnp.exp2(m_old - m_new)
        pT = jnp.exp2(sT - m_new).astype(BF16)
        vT = vT_ref[j, hh * VT_ROWS:(hh + 1) * VT_ROWS, :]
        acc_scr[...] = alpha * acc_scr[...] + jnp.dot(vT, pT, preferred_element_type=F32)
        m_scr[...] = m_new

    def consume_diag(hh, s_ref):
        sT = s_ref[...]
        r = lax.broadcasted_iota(jnp.int32, sT.shape, 0)
        c = lax.broadcasted_iota(jnp.int32, sT.shape, 1)
        consume(hh, jnp.where(r <= c, sT, -1e30), qi)

    def pair(j):
        for hh in range(heads):
            sa, sb, _, _ = state[hh]
            sb[...] = scores(hh, j + 1)
            consume(hh, sa[...], j)
        for hh in range(heads):
            sa, sb, _, _ = state[hh]
            sa[...] = scores(hh, j + 2)
            consume(hh, sb[...], j + 1)

    for hh in range(heads):
        sa, _, m_scr, acc_scr = state[hh]
        m_scr[...] = jnp.full(m_scr.shape, -1e30, F32)
        acc_scr[...] = jnp.zeros(acc_scr.shape, F32)
        sa[...] = scores(hh, 0)

    def body(t, carry):
        pair(4 * t)
        pair(4 * t + 2)
        return carry

    lax.fori_loop(0, qi // 4, body, 0)
    rem = qi % 4

    @pl.when(rem >= 2)
    def _():
        pair(qi - rem)

    @pl.when(rem % 2 == 0)
    def _():
        for hh in range(heads):
            consume_diag(hh, state[hh][0])

    @pl.when(rem % 2 == 1)
    def _():
        for hh in range(heads):
            sa, sb, _, _ = state[hh]
            sb[...] = scores(hh, qi)
            consume(hh, sa[...], qi - 1)
        for hh in range(heads):
            consume_diag(hh, state[hh][1])

    for hh in range(heads):
        acc = state[hh][3][...]
        o_ref[:, hh * V_HEAD_DIM:(hh + 1) * V_HEAD_DIM] = (
            acc[:V_HEAD_DIM] / acc[V_HEAD_DIM:V_HEAD_DIM + 1]).T.astype(o_ref.dtype)


def _attention(qT, k, vT, blk, heads):
    t = k.shape[0]
    b = t // SEQ
    nq = SEQ // blk
    per_head_scratch = [pltpu.VMEM((blk, blk), F32),
                        pltpu.VMEM((blk, blk), F32),
                        pltpu.VMEM((1, blk), F32),
                        pltpu.VMEM((VT_ROWS, blk), F32)]
    return pl.pallas_call(
        functools.partial(_attn_kernel, blk=blk, heads=heads),
        grid=(b, MLA_HEADS // heads, nq),
        in_specs=[pl.BlockSpec((None, heads * HEAD_QK, blk), lambda bi, g, i: (bi * nq + i, g, 0)),
                  pl.BlockSpec((SEQ, heads * HEAD_QK), lambda bi, g, i: (bi, g)),
                  pl.BlockSpec((nq, heads * VT_ROWS, blk), lambda bi, g, i: (bi, g, 0))],
        out_specs=pl.BlockSpec((blk, heads * V_HEAD_DIM), lambda bi, g, i: (bi * nq + i, g)),
        out_shape=jax.ShapeDtypeStruct((t, MLA_HEADS * V_HEAD_DIM), BF16),
        scratch_shapes=per_head_scratch * heads,
        compiler_params=_params("arbitrary", "arbitrary", "arbitrary"),
        name="attention",
    )(qT, k, vT)


def _outproj_kernel(a_ref, gb_ref, gc_ref, ci_ref, gch_ref, cih_ref, cw_ref, cb_ref,
                    x_ref, gt_ref, g_ref, wa_ref, wc_ref, o_ref, *, per_b):
    first = (pl.program_id(0) % per_b) == 0
    g = gc_ref[...].astype(F32) * ci_ref[...].astype(F32)
    gh = gch_ref[...].astype(F32) * cih_ref[...].astype(F32)
    halo = jnp.where(first, 0.0, gh[8:16, :])
    conv = (gb_ref[...].astype(F32) * _causal_conv3(g, halo, cw_ref, cb_ref)).astype(BF16)
    mix = (jnp.dot(a_ref[...], wa_ref[...], preferred_element_type=F32)
           + jnp.dot(conv, wc_ref[...], preferred_element_type=F32))
    o_ref[...] = x_ref[...] + _rms(mix, gt_ref[...] * g_ref[...])


def _out_proj(attn, proj, conv_w, conv_b, x2, mod3, g_post, w_o_bf, tm):
    t, d = x2.shape
    ka = attn.shape[1]
    c = CONV_WIDTH
    hb = 16
    per_b = SEQ // tm
    halo_idx = lambda col: (lambda i: (jnp.maximum(i * (tm // hb) - 1, 0), col))
    return pl.pallas_call(
        functools.partial(_outproj_kernel, per_b=per_b),
        grid=(t // tm,),
        in_specs=[pl.BlockSpec((tm, ka), lambda i: (i, 0)),
                  pl.BlockSpec((tm, c), lambda i: (i, OFF_GATE_B // c)),
                  pl.BlockSpec((tm, c), lambda i: (i, OFF_GATE_C // c)),
                  pl.BlockSpec((tm, c), lambda i: (i, OFF_CONV_IN // c)),
                  pl.BlockSpec((hb, c), halo_idx(OFF_GATE_C // c)),
                  pl.BlockSpec((hb, c), halo_idx(OFF_CONV_IN // c)),
                  pl.BlockSpec((3, c), lambda i: (0, 0)),
                  pl.BlockSpec((1, c), lambda i: (0, 0)),
                  pl.BlockSpec((tm, d), lambda i: (i, 0)),
                  pl.BlockSpec((None, 1, d), lambda i: ((i // per_b) * N_MOD + 2, 0, 0)),
                  pl.BlockSpec((1, d), lambda i: (0, 0)),
                  pl.BlockSpec((ka, d), lambda i: (0, 0)),
                  pl.BlockSpec((c, d), lambda i: (ka // c, 0))],
        out_specs=pl.BlockSpec((tm, d), lambda i: (i, 0)),
        out_shape=jax.ShapeDtypeStruct((t, d), F32),
        compiler_params=_params("arbitrary"),
        name="out_proj",
    )(attn, proj, proj, proj, proj, proj, conv_w, conv_b, x2, mod3, g_post, w_o_bf, w_o_bf)


def _ffn_kernel(x_ref, sc_ref, sh_ref, gt_ref, gpre_ref, gpost_ref, wa_ref, wg_ref,
                cwa_ref, cwg_ref, cba_ref, cbg_ref, wd_ref, o_ref,
                h_scr, halo_a, halo_g, y0_scr, y1_scr, *, per_b, nf, sub):
    i = pl.program_id(0)
    f = pl.program_id(1)
    tm = x_ref.shape[0]
    tf = wa_ref.shape[1]
    first = (i % per_b) == 0

    def branch(w_ref, cw_ref, cb_ref, halo_ref, cols, dot_rows):
        w = w_ref[:, cols]
        u = jnp.concatenate([jnp.dot(h_scr[r:r + dot_rows, :], w, preferred_element_type=F32)
                             for r in range(0, tm, dot_rows)], axis=0)
        halo = jnp.where(first, 0.0, halo_ref[f, :, cols])
        halo_ref[f, :, cols] = u[tm - 8:, :]
        return _causal_conv3(u, halo, cw_ref.at[:, cols], cb_ref.at[:, cols])

    def up(y_out, dot_rows=None):
        dot_rows = dot_rows or tm
        for s in range(0, tf, sub):
            cols = slice(s, s + sub)
            a = branch(wa_ref, cwa_ref, cba_ref, halo_a, cols, dot_rows)
            g = branch(wg_ref, cwg_ref, cbg_ref, halo_g, cols, dot_rows)
            hg = 0.5 * g
            y_out[:, cols] = ((hg + hg * jnp.tanh(hg)) * a).astype(BF16)

    def down(y_in):
        o_ref[...] += jnp.dot(y_in[...], wd_ref[...], preferred_element_type=F32)

    @pl.when(f == 0)
    def _():
        gain = gpre_ref[...] * (1.0 + sc_ref[...])
        for r in range(0, tm, EDGE_ROWS):
            h_scr[r:r + EDGE_ROWS, :] = (_rms(x_ref[r:r + EDGE_ROWS, :], gain) + sh_ref[...]).astype(BF16)
        o_ref[...] = jnp.zeros(o_ref.shape, F32)
        up(y0_scr, EDGE_ROWS)

    interior = jnp.logical_and(f > 0, f < nf)

    @pl.when(jnp.logical_and(interior, f % 2 == 1))
    def _():
        up(y1_scr)
        down(y0_scr)

    @pl.when(jnp.logical_and(interior, f % 2 == 0))
    def _():
        up(y0_scr)
        down(y1_scr)

    @pl.when(f == nf)
    def _():
        y_last = y1_scr if (nf - 1) % 2 == 1 else y0_scr
        gain = gt_ref[...] * gpost_ref[...]
        for r in range(0, tm, tm // 2):
            rows = slice(r, r + tm // 2)
            acc = o_ref[rows, :] + jnp.dot(y_last[rows, :], wd_ref[...], preferred_element_type=F32)
            o_ref[rows, :] = x_ref[rows, :] + _rms(acc, gain)


def _conv_ffn(x1, mod3, g_pre, g_post, w_up_bf, conv_w, conv_b, w_down_bf, tm):
    t, d = x1.shape
    tf = FFN_TF
    nf = D_FF // tf
    per_b = SEQ // tm
    modspec = lambda k: pl.BlockSpec((None, 1, d), lambda i, f: ((i // per_b) * N_MOD + k, 0, 0))
    up_f = lambda f: jnp.minimum(f, nf - 1)
    dn_f = lambda f: jnp.maximum(f - 1, 0)
    return pl.pallas_call(
        functools.partial(_ffn_kernel, per_b=per_b, nf=nf, sub=2 * LANES),
        grid=(t // tm, nf + 1),
        in_specs=[pl.BlockSpec((tm, d), lambda i, f: (i, 0)),
                  modspec(4), modspec(3), modspec(5),
                  pl.BlockSpec((1, d), lambda i, f: (0, 0)),
                  pl.BlockSpec((1, d), lambda i, f: (0, 0)),
                  pl.BlockSpec((d, tf), lambda i, f: (0, up_f(f))),
                  pl.BlockSpec((d, tf), lambda i, f: (0, nf + up_f(f))),
                  pl.BlockSpec((3, tf), lambda i, f: (0, up_f(f))),
                  pl.BlockSpec((3, tf), lambda i, f: (0, nf + up_f(f))),
                  pl.BlockSpec((1, tf), lambda i, f: (0, up_f(f))),
                  pl.BlockSpec((1, tf), lambda i, f: (0, nf + up_f(f))),
                  pl.BlockSpec((tf, d), lambda i, f: (dn_f(f), 0))],
        out_specs=pl.BlockSpec((tm, d), lambda i, f: (i, 0)),
        out_shape=jax.ShapeDtypeStruct((t, d), F32),
        scratch_shapes=[pltpu.VMEM((tm, d), BF16),
                        pltpu.VMEM((nf, 8, tf), F32),
                        pltpu.VMEM((nf, 8, tf), F32),
                        pltpu.VMEM((tm, tf), BF16),
                        pltpu.VMEM((tm, tf), BF16)],
        compiler_params=_params("arbitrary", "arbitrary"),
        name="conv_ffn",
    )(x1, mod3, mod3, mod3, g_pre, g_post, w_up_bf, w_up_bf, conv_w, conv_w,
      conv_b, conv_b, w_down_bf)


def _rot_half_cols(w):
    half = w.shape[-1] // 2
    return jnp.concatenate([-w[..., half:], w[..., :half]], axis=-1)


def _regroup_w_in(w):
    d = w.shape[0]
    c0 = Q_LORA_RANK
    c1 = c0 + KV_LORA_RANK
    c2 = c1 + QK_ROPE_DIM
    c3 = c2 + CONV_WIDTH
    c4 = c3 + CONV_WIDTH
    q_lat, kv_lat, k_rope = w[:, :c0], w[:, c0:c1], w[:, c1:c2]
    gate_b, gate_c, conv_in = w[:, c2:c3], w[:, c3:c4], w[:, c4:]
    pad = jnp.zeros((d, OFF_KV_LAT - OFF_K_ROPE - 2 * QK_ROPE_DIM), w.dtype)
    return jnp.concatenate([gate_b, gate_c, conv_in, q_lat, k_rope, _rot_half_cols(k_rope),
                            pad, kv_lat], axis=1)


def _block_cols(w, tn):
    k, n = w.shape
    return w.reshape(k, n // tn, tn).transpose(1, 0, 2)


def _regroup_w_uq(w):
    r = w.shape[0]
    w = w.reshape(r, MLA_HEADS, QK_NOPE_DIM + QK_ROPE_DIM)
    nope, rope = w[..., :QK_NOPE_DIM], w[..., QK_NOPE_DIM:]
    return jnp.concatenate([nope, rope, _rot_half_cols(rope)], axis=-1).reshape(r, MLA_HEADS * HEAD_QK)


def _regroup_w_ukv(w):
    r = w.shape[0]
    w = w.reshape(r, MLA_HEADS, QK_NOPE_DIM + V_HEAD_DIM)
    k = w[..., :QK_NOPE_DIM].reshape(r, MLA_HEADS * QK_NOPE_DIM)
    v = w[..., QK_NOPE_DIM:].reshape(r, MLA_HEADS * V_HEAD_DIM)
    return jnp.concatenate([k, v], axis=1)


def kernel(x, c, positions, w_ada, b_ada, g_pre_mix, g_post_mix, w_in, g_q, w_uq, g_kv, w_ukv,
           conv_w_mix, conv_b_mix, w_o, g_pre_ffn, g_post_ffn, w_up, conv_w_ffn, conv_b_ffn, w_down):
    b, s, d = x.shape
    assert (s, d) == (SEQ, D_MODEL) and w_ada.shape[0] == 1
    t = b * s
    x2 = x.reshape(t, d)
    pos3 = positions.reshape(t // ATT_BLK, 1, ATT_BLK)
    half = jnp.arange(0, QK_ROPE_DIM, 2, dtype=F32) / QK_ROPE_DIM
    invf = (1.0 / (ROPE_THETA ** half)).reshape(QK_ROPE_DIM // 2, 1)

    rows = 8
    c_pad = jnp.zeros((rows, d), F32).at[:b].set(c)
    x_cur = x2
    for l in range(w_ada.shape[0]):
        mod = _adaln_mod(c_pad, w_ada[l], b_ada[l].reshape(1, -1))
        mod3 = mod[:b].reshape(b * N_MOD, 1, d)

        w_in_blk = _block_cols(_regroup_w_in(w_in[l].astype(BF16)), IN_PROJ_TN)
        wq_p = _regroup_w_uq(w_uq[l]).astype(BF16)
        wkv_p = _regroup_w_ukv(w_ukv[l]).astype(BF16)

        proj = _in_proj(x_cur, mod3, g_pre_mix[l].reshape(1, d), w_in_blk, tm=1024)
        qT, k, vT = _latent_qkv(proj, pos3, invf, g_q[l].reshape(1, -1), g_kv[l].reshape(1, -1),
                                wq_p, wkv_p, tm=ATT_BLK)
        attn = _attention(qT, k, vT, blk=ATT_BLK, heads=ATT_HEADS_PER_STEP)
        x_cur = _out_proj(attn, proj, conv_w_mix[l], conv_b_mix[l].reshape(1, -1), x_cur, mod3,
                          g_post_mix[l].reshape(1, d), w_o[l].astype(BF16), tm=512)
        x_cur = _conv_ffn(x_cur, mod3, g_pre_ffn[l].reshape(1, d), g_post_ffn[l].reshape(1, d),
                          w_up[l].astype(BF16), conv_w_ffn[l],
                          conv_b_ffn[l].reshape(1, -1), w_down[l].astype(BF16), tm=1024)
    return x_cur.reshape(b, s, d)
```

```python
import functools
import math

import jax
import jax.numpy as jnp
from jax import lax
from jax.experimental import pallas as pl
from jax.experimental.pallas import tpu as pltpu

D_MODEL = 2048
SEQ = 8192
CONV_WIDTH = 1024
MLA_HEADS = 8
QK_NOPE_DIM = 128
QK_ROPE_DIM = 64
V_HEAD_DIM = 128
Q_LORA_RANK = 768
KV_LORA_RANK = 512
ROPE_THETA = 10000.0
D_FF = 5632
RMS_EPS = 1e-6
N_MOD = 6

LANES = 128
HEAD_QK = 2 * LANES
PROJ_COLS = 4608
VMEM_LIMIT = 56 * 1024 * 1024
ATT_BLK = 512
ATT_HEADS_PER_STEP = 2
VT_ROWS = V_HEAD_DIM + 16
LOG2_E = 1.4426950408889634
IN_PROJ_TN = 1536
UP_DOT_ROWS = 512
UP_SUB = 512
FFN_TF = 512

OFF_GATE_B = 0
OFF_GATE_C = 1024
OFF_CONV_IN = 2048
OFF_Q_LAT = 3072
OFF_K_ROPE = 3840
OFF_KV_LAT = 4096

BF16 = jnp.bfloat16
F32 = jnp.float32


def _params(*sem):
    return pltpu.CompilerParams(dimension_semantics=sem, vmem_limit_bytes=VMEM_LIMIT)


def _rms(x, g):
    return x * lax.rsqrt(jnp.mean(x * x, axis=-1, keepdims=True) + RMS_EPS) * g


def _mod_kernel(c_ref, w_ref, b_ref, o_ref):
    c = c_ref[...]
    c_act = c * (1.0 / (1.0 + jnp.exp(-c)))
    o_ref[...] = jnp.dot(c_act.astype(BF16), w_ref[...].astype(BF16),
                         preferred_element_type=F32) + b_ref[...]


def _adaln_mod(c_pad, w_ada, b_ada):
    rows, d = c_pad.shape
    n = w_ada.shape[1]
    tn = 1536
    return pl.pallas_call(
        _mod_kernel,
        grid=(n // tn,),
        in_specs=[pl.BlockSpec((rows, d), lambda j: (0, 0)),
                  pl.BlockSpec((d, tn), lambda j: (0, j)),
                  pl.BlockSpec((1, tn), lambda j: (0, j))],
        out_specs=pl.BlockSpec((rows, tn), lambda j: (0, j)),
        out_shape=jax.ShapeDtypeStruct((rows, n), F32),
        compiler_params=_params("arbitrary"),
        name="adaln_mod",
    )(c_pad, w_ada, b_ada)


def _inproj_kernel(x_ref, sc_ref, sh_ref, g_ref, w_ref, o_ref, h_scr):
    tm = x_ref.shape[0]

    def project(rows):
        for r in range(0, tm, rows):
            o_ref[r:r + rows, :] = jnp.dot(h_scr[r:r + rows, :], w_ref[...],
                                           preferred_element_type=F32).astype(o_ref.dtype)

    @pl.when(pl.program_id(1) == 0)
    def _():
        rows = tm // 4
        for r in range(0, tm, rows):
            h = _rms(x_ref[r:r + rows, :], g_ref[...] * (1.0 + sc_ref[...])) + sh_ref[...]
            h_scr[r:r + rows, :] = h.astype(BF16)
        project(rows)

    @pl.when(pl.program_id(1) != 0)
    def _():
        project(tm)


def _in_proj(x2, mod3, g_pre, w_blk, tm):
    t, d = x2.shape
    nb, _, tn = w_blk.shape
    n = nb * tn
    per_b = SEQ // tm
    return pl.pallas_call(
        _inproj_kernel,
        grid=(t // tm, n // tn),
        in_specs=[pl.BlockSpec((tm, d), lambda i, j: (i, 0)),
                  pl.BlockSpec((None, 1, d), lambda i, j: ((i // per_b) * N_MOD + 1, 0, 0)),
                  pl.BlockSpec((None, 1, d), lambda i, j: ((i // per_b) * N_MOD + 0, 0, 0)),
                  pl.BlockSpec((1, d), lambda i, j: (0, 0)),
                  pl.BlockSpec((None, d, tn), lambda i, j: (j, 0, 0))],
        out_specs=pl.BlockSpec((tm, tn), lambda i, j: (i, j)),
        out_shape=jax.ShapeDtypeStruct((t, n), BF16),
        scratch_shapes=[pltpu.VMEM((tm, d), BF16)],
        compiler_params=_params("arbitrary", "arbitrary"),
        name="in_proj",
    )(x2, mod3, mod3, g_pre, w_blk)


def _qkv_kernel(ql_ref, kr_ref, kvl_ref, pos_ref, invf_ref, gq_ref, gkv_ref,
                wq_ref, wkv_ref, qT_ref, k_ref, vT_ref):
    tm = ql_ref.shape[0]
    ang = invf_ref[...] * pos_ref[...].astype(F32)
    cos = jnp.cos(ang)
    sin = jnp.sin(ang)
    cos2 = jnp.concatenate([cos, cos], axis=0)
    sin2 = jnp.concatenate([sin, sin], axis=0)

    def rope_t(a):
        at = a.T
        return at[:QK_ROPE_DIM, :] * cos2 + at[QK_ROPE_DIM:, :] * sin2

    qn = _rms(ql_ref[...].astype(F32), gq_ref[...]).astype(BF16)
    q = jnp.dot(qn, wq_ref[...], preferred_element_type=F32)
    scale = LOG2_E / math.sqrt(QK_NOPE_DIM + QK_ROPE_DIM)
    kvn = _rms(kvl_ref[...].astype(F32), gkv_ref[...]).astype(BF16)
    kv = jnp.dot(kvn, wkv_ref[...], preferred_element_type=F32)
    zeros_t = jnp.zeros((LANES - QK_ROPE_DIM, tm), F32)
    k_rope = jnp.concatenate([rope_t(kr_ref[...].astype(F32)), zeros_t], axis=0).T.astype(BF16)
    v_off = MLA_HEADS * QK_NOPE_DIM
    ones = jnp.ones((VT_ROWS - V_HEAD_DIM, tm), BF16)
    for h in range(MLA_HEADS):
        lo = h * HEAD_QK
        qT_ref[lo:lo + LANES, :] = (q[:, lo:lo + LANES] * scale).T.astype(BF16)
        qT_ref[lo + LANES:lo + LANES + QK_ROPE_DIM, :] = (rope_t(q[:, lo + LANES:lo + HEAD_QK]) * scale).astype(BF16)
        qT_ref[lo + LANES + QK_ROPE_DIM:lo + HEAD_QK, :] = zeros_t.astype(BF16)
        k_ref[:, lo:lo + LANES] = kv[:, h * LANES:(h + 1) * LANES].astype(BF16)
        k_ref[:, lo + LANES:lo + HEAD_QK] = k_rope
        vlo = h * VT_ROWS
        vT_ref[vlo:vlo + V_HEAD_DIM, :] = kv[:, v_off + h * LANES:v_off + (h + 1) * LANES].T.astype(BF16)
        vT_ref[vlo + V_HEAD_DIM:vlo + VT_ROWS, :] = ones


def _latent_qkv(proj, pos3, invf, g_q, g_kv, wq_p, wkv_p, tm):
    t = proj.shape[0]
    hq = MLA_HEADS * HEAD_QK
    hv = MLA_HEADS * V_HEAD_DIM
    hvt = MLA_HEADS * VT_ROWS
    const = lambda i: (0, 0)
    return pl.pallas_call(
        _qkv_kernel,
        grid=(t // tm,),
        in_specs=[pl.BlockSpec((tm, Q_LORA_RANK), lambda i: (i, OFF_Q_LAT // Q_LORA_RANK)),
                  pl.BlockSpec((tm, LANES), lambda i: (i, OFF_K_ROPE // LANES)),
                  pl.BlockSpec((tm, KV_LORA_RANK), lambda i: (i, OFF_KV_LAT // KV_LORA_RANK)),
                  pl.BlockSpec((None, 1, tm), lambda i: (i, 0, 0)),
                  pl.BlockSpec((QK_ROPE_DIM // 2, 1), const),
                  pl.BlockSpec((1, Q_LORA_RANK), const),
                  pl.BlockSpec((1, KV_LORA_RANK), const),
                  pl.BlockSpec((Q_LORA_RANK, hq), const),
                  pl.BlockSpec((KV_LORA_RANK, 2 * hv), const)],
        out_specs=[pl.BlockSpec((None, hq, tm), lambda i: (i, 0, 0)),
                   pl.BlockSpec((tm, hq), lambda i: (i, 0)),
                   pl.BlockSpec((None, hvt, tm), lambda i: (i, 0, 0))],
        out_shape=[jax.ShapeDtypeStruct((t // tm, hq, tm), BF16),
                   jax.ShapeDtypeStruct((t, hq), BF16),
                   jax.ShapeDtypeStruct((t // tm, hvt, tm), BF16)],
        compiler_params=_params("arbitrary"),
        name="latent_qkv",
    )(proj, proj, proj, pos3, invf, g_q, g_kv, wq_p, wkv_p)


def _causal_conv3(u, halo, w_ref, b_ref):
    w0, w1, w2, b = w_ref[0:1, :], w_ref[1:2, :], w_ref[2:3, :], b_ref[...]

    def taps(p2, p1, p0):
        return p2 * w0 + p1 * w1 + p0 * w2 + b

    body = taps(pltpu.roll(u, 2, 0), pltpu.roll(u, 1, 0), u)
    head = jnp.concatenate([halo, u[0:8, :]], axis=0)
    first = taps(pltpu.roll(head, 2, 0), pltpu.roll(head, 1, 0), head)[8:16, :]
    return jnp.concatenate([first, body[8:, :]], axis=0)


def _attn_kernel(qT_ref, k_ref, vT_ref, o_ref, *scratch, blk, heads):
    qi = pl.program_id(2)
    per_head = 4
    state = [scratch[per_head * hh:per_head * (hh + 1)] for hh in range(heads)]

    def q_rows(hh):
        return slice(hh * HEAD_QK, (hh + 1) * HEAD_QK)

    def scores(hh, j):
        start = pl.multiple_of(j * blk, blk)
        return jnp.dot(k_ref[pl.ds(start, blk), q_rows(hh)], qT_ref[q_rows(hh), :],
                       preferred_element_type=F32)

    def consume(hh, sT, j):
        _, _, m_scr, acc_scr = state[hh]
        m_old = m_scr[...]
        m_new = jnp.maximum(m_old, jnp.max(sT, axis=0, keepdims=True))
        alpha = jnp.exp2(m_old - m_new)
        pT = jnp.exp2(sT - m_new).astype(BF16)
        vT = vT_ref[j, hh * VT_ROWS:(hh + 1) * VT_ROWS, :]
        acc_scr[...] = alpha * acc_scr[...] + jnp.dot(vT, pT, preferred_element_type=F32)
        m_scr[...] = m_new

    def consume_diag(hh, s_ref):
        sT = s_ref[...]
        r = lax.broadcasted_iota(jnp.int32, sT.shape, 0)
        c = lax.broadcasted_iota(jnp.int32, sT.shape, 1)
        consume(hh, jnp.where(r <= c, sT, -1e30), qi)

    def pair(j):
        for hh in range(heads):
            sa, sb, _, _ = state[hh]
            sb[...] = scores(hh, j + 1)
            consume(hh, sa[...], j)
        for hh in range(heads):
            sa, sb, _, _ = state[hh]
            sa[...] = scores(hh, j + 2)
            consume(hh, sb[...], j + 1)

    for hh in range(heads):
        sa, _, m_scr, acc_scr = state[hh]
        m_scr[...] = jnp.full(m_scr.shape, -1e30, F32)
        acc_scr[...] = jnp.zeros(acc_scr.shape, F32)
        sa[...] = scores(hh, 0)

    def body(t, carry):
        pair(4 * t)
        pair(4 * t + 2)
        return carry

    lax.fori_loop(0, qi // 4, body, 0)
    rem = qi % 4

    @pl.when(rem >= 2)
    def _():
        pair(qi - rem)

    @pl.when(rem % 2 == 0)
    def _():
        for hh in range(heads):
            consume_diag(hh, state[hh][0])

    @pl.when(rem % 2 == 1)
    def _():
        for hh in range(heads):
            sa, sb, _, _ = state[hh]
            sb[...] = scores(hh, qi)
            consume(hh, sa[...], qi - 1)
        for hh in range(heads):
            consume_diag(hh, state[hh][1])

    for hh in range(heads):
        acc = state[hh][3][...]
        o_ref[:, hh * V_HEAD_DIM:(hh + 1) * V_HEAD_DIM] = (
            acc[:V_HEAD_DIM] / acc[V_HEAD_DIM:V_HEAD_DIM + 1]).T.astype(o_ref.dtype)


def _attention(qT, k, vT, blk, heads):
    t = k.shape[0]
    b = t // SEQ
    nq = SEQ // blk
    per_head_scratch = [pltpu.VMEM((blk, blk), F32),
                        pltpu.VMEM((blk, blk), F32),
                        pltpu.VMEM((1, blk), F32),
                        pltpu.VMEM((VT_ROWS, blk), F32)]
    return pl.pallas_call(
        functools.partial(_attn_kernel, blk=blk, heads=heads),
        grid=(b, MLA_HEADS // heads, nq),
        in_specs=[pl.BlockSpec((None, heads * HEAD_QK, blk), lambda bi, g, i: (bi * nq + i, g, 0)),
                  pl.BlockSpec((SEQ, heads * HEAD_QK), lambda bi, g, i: (bi, g)),
                  pl.BlockSpec((nq, heads * VT_ROWS, blk), lambda bi, g, i: (bi, g, 0))],
        out_specs=pl.BlockSpec((blk, heads * V_HEAD_DIM), lambda bi, g, i: (bi * nq + i, g)),
        out_shape=jax.ShapeDtypeStruct((t, MLA_HEADS * V_HEAD_DIM), BF16),
        scratch_shapes=per_head_scratch * heads,
        compiler_params=_params("arbitrary", "arbitrary", "arbitrary"),
        name="attention",
    )(qT, k, vT)


def _outproj_kernel(a_ref, gb_ref, gc_ref, ci_ref, gch_ref, cih_ref, cw_ref, cb_ref,
                    x_ref, gt_ref, g_ref, wa_ref, wc_ref, o_ref, *, per_b):
    first = (pl.program_id(0) % per_b) == 0
    g = gc_ref[...].astype(F32) * ci_ref[...].astype(F32)
    gh = gch_ref[...].astype(F32) * cih_ref[...].astype(F32)
    halo = jnp.where(first, 0.0, gh[8:16, :])
    conv = (gb_ref[...].astype(F32) * _causal_conv3(g, halo, cw_ref, cb_ref)).astype(BF16)
    mix = (jnp.dot(a_ref[...], wa_ref[...], preferred_element_type=F32)
           + jnp.dot(conv, wc_ref[...], preferred_element_type=F32))
    o_ref[...] = x_ref[...] + _rms(mix, gt_ref[...] * g_ref[...])


def _out_proj(attn, proj, conv_w, conv_b, x2, mod3, g_post, w_o_bf, tm):
    t, d = x2.shape
    ka = attn.shape[1]
    c = CONV_WIDTH
    hb = 16
    per_b = SEQ // tm
    halo_idx = lambda col: (lambda i: (jnp.maximum(i * (tm // hb) - 1, 0), col))
    return pl.pallas_call(
        functools.partial(_outproj_kernel, per_b=per_b),
        grid=(t // tm,),
        in_specs=[pl.BlockSpec((tm, ka), lambda i: (i, 0)),
                  pl.BlockSpec((tm, c), lambda i: (i, OFF_GATE_B // c)),
                  pl.BlockSpec((tm, c), lambda i: (i, OFF_GATE_C // c)),
                  pl.BlockSpec((tm, c), lambda i: (i, OFF_CONV_IN // c)),
                  pl.BlockSpec((hb, c), halo_idx(OFF_GATE_C // c)),
                  pl.BlockSpec((hb, c), halo_idx(OFF_CONV_IN // c)),
                  pl.BlockSpec((3, c), lambda i: (0, 0)),
                  pl.BlockSpec((1, c), lambda i: (0, 0)),
                  pl.BlockSpec((tm, d), lambda i: (i, 0)),
                  pl.BlockSpec((None, 1, d), lambda i: ((i // per_b) * N_MOD + 2, 0, 0)),
                  pl.BlockSpec((1, d), lambda i: (0, 0)),
                  pl.BlockSpec((ka, d), lambda i: (0, 0)),
                  pl.BlockSpec((c, d), lambda i: (ka // c, 0))],
        out_specs=pl.BlockSpec((tm, d), lambda i: (i, 0)),
        out_shape=jax.ShapeDtypeStruct((t, d), F32),
        compiler_params=_params("arbitrary"),
        name="out_proj",
    )(attn, proj, proj, proj, proj, proj, conv_w, conv_b, x2, mod3, g_post, w_o_bf, w_o_bf)


def _ffn_kernel(x_ref, sc_ref, sh_ref, gt_ref, gpre_ref, gpost_ref, wa_ref, wg_ref,
                cwa_ref, cwg_ref, cba_ref, cbg_ref, wd_ref, o_ref,
                h_scr, halo_a, halo_g, y0_scr, y1_scr, *, per_b, nf, sub):
    i = pl.program_id(0)
    f = pl.program_id(1)
    tm = x_ref.shape[0]
    tf = wa_ref.shape[1]
    first = (i % per_b) == 0

    def branch(w_ref, cw_ref, cb_ref, halo_ref, cols):
        w = w_ref[:, cols]
        u = jnp.concatenate([jnp.dot(h_scr[r:r + UP_DOT_ROWS, :], w, preferred_element_type=F32)
                             for r in range(0, tm, UP_DOT_ROWS)], axis=0)
        halo = jnp.where(first, 0.0, halo_ref[f, :, cols])
        halo_ref[f, :, cols] = u[tm - 8:, :]
        return _causal_conv3(u, halo, cw_ref.at[:, cols], cb_ref.at[:, cols])

    def up(y_out):
        for s in range(0, tf, sub):
            cols = slice(s, s + sub)
            a = branch(wa_ref, cwa_ref, cba_ref, halo_a, cols)
            g = branch(wg_ref, cwg_ref, cbg_ref, halo_g, cols)
            hg = 0.5 * g
            y_out[:, cols] = ((hg + hg * jnp.tanh(hg)) * a).astype(BF16)

    def down(y_in):
        o_ref[...] += jnp.dot(y_in[...], wd_ref[...], preferred_element_type=F32)

    @pl.when(f == 0)
    def _():
        h = _rms(x_ref[...], gpre_ref[...] * (1.0 + sc_ref[...])) + sh_ref[...]
        h_scr[...] = h.astype(BF16)
        o_ref[...] = jnp.zeros(o_ref.shape, F32)
        up(y0_scr)

    interior = jnp.logical_and(f > 0, f < nf)

    @pl.when(jnp.logical_and(interior, f % 2 == 1))
    def _():
        up(y1_scr)
        down(y0_scr)

    @pl.when(jnp.logical_and(interior, f % 2 == 0))
    def _():
        up(y0_scr)
        down(y1_scr)

    @pl.when(f == nf)
    def _():
        down(y1_scr if (nf - 1) % 2 == 1 else y0_scr)
        o_ref[...] = x_ref[...] + _rms(o_ref[...], gt_ref[...] * gpost_ref[...])


def _conv_ffn(x1, mod3, g_pre, g_post, w_up_bf, conv_w, conv_b, w_down_bf, tm):
    t, d = x1.shape
    tf = FFN_TF
    nf = D_FF // tf
    per_b = SEQ // tm
    modspec = lambda k: pl.BlockSpec((None, 1, d), lambda i, f: ((i // per_b) * N_MOD + k, 0, 0))
    up_f = lambda f: jnp.minimum(f, nf - 1)
    dn_f = lambda f: jnp.maximum(f - 1, 0)
    return pl.pallas_call(
        functools.partial(_ffn_kernel, per_b=per_b, nf=nf, sub=UP_SUB),
        grid=(t // tm, nf + 1),
        in_specs=[pl.BlockSpec((tm, d), lambda i, f: (i, 0)),
                  modspec(4), modspec(3), modspec(5),
                  pl.BlockSpec((1, d), lambda i, f: (0, 0)),
                  pl.BlockSpec((1, d), lambda i, f: (0, 0)),
                  pl.BlockSpec((d, tf), lambda i, f: (0, up_f(f))),
                  pl.BlockSpec((d, tf), lambda i, f: (0, nf + up_f(f))),
                  pl.BlockSpec((3, tf), lambda i, f: (0, up_f(f))),
                  pl.BlockSpec((3, tf), lambda i, f: (0, nf + up_f(f))),
                  pl.BlockSpec((1, tf), lambda i, f: (0, up_f(f))),
                  pl.BlockSpec((1, tf), lambda i, f: (0, nf + up_f(f))),
                  pl.BlockSpec((tf, d), lambda i, f: (dn_f(f), 0))],
        out_specs=pl.BlockSpec((tm, d), lambda i, f: (i, 0)),
        out_shape=jax.ShapeDtypeStruct((t, d), F32),
        scratch_shapes=[pltpu.VMEM((tm, d), BF16),
                        pltpu.VMEM((nf, 8, tf), F32),
                        pltpu.VMEM((nf, 8, tf), F32),
                        pltpu.VMEM((tm, tf), BF16),
                        pltpu.VMEM((tm, tf), BF16)],
        compiler_params=_params("arbitrary", "arbitrary"),
        name="conv_ffn",
    )(x1, mod3, mod3, mod3, g_pre, g_post, w_up_bf, w_up_bf, conv_w, conv_w,
      conv_b, conv_b, w_down_bf)


def _rot_half_cols(w):
    half = w.shape[-1] // 2
    return jnp.concatenate([-w[..., half:], w[..., :half]], axis=-1)


def _regroup_w_in(w):
    d = w.shape[0]
    c0 = Q_LORA_RANK
    c1 = c0 + KV_LORA_RANK
    c2 = c1 + QK_ROPE_DIM
    c3 = c2 + CONV_WIDTH
    c4 = c3 + CONV_WIDTH
    q_lat, kv_lat, k_rope = w[:, :c0], w[:, c0:c1], w[:, c1:c2]
    gate_b, gate_c, conv_in = w[:, c2:c3], w[:, c3:c4], w[:, c4:]
    pad = jnp.zeros((d, OFF_KV_LAT - OFF_K_ROPE - 2 * QK_ROPE_DIM), w.dtype)
    return jnp.concatenate([gate_b, gate_c, conv_in, q_lat, k_rope, _rot_half_cols(k_rope),
                            pad, kv_lat], axis=1)


def _block_cols(w, tn):
    k, n = w.shape
    return w.reshape(k, n // tn, tn).transpose(1, 0, 2)


def _regroup_w_uq(w):
    r = w.shape[0]
    w = w.reshape(r, MLA_HEADS, QK_NOPE_DIM + QK_ROPE_DIM)
    nope, rope = w[..., :QK_NOPE_DIM], w[..., QK_NOPE_DIM:]
    return jnp.concatenate([nope, rope, _rot_half_cols(rope)], axis=-1).reshape(r, MLA_HEADS * HEAD_QK)


def _regroup_w_ukv(w):
    r = w.shape[0]
    w = w.reshape(r, MLA_HEADS, QK_NOPE_DIM + V_HEAD_DIM)
    k = w[..., :QK_NOPE_DIM].reshape(r, MLA_HEADS * QK_NOPE_DIM)
    v = w[..., QK_NOPE_DIM:].reshape(r, MLA_HEADS * V_HEAD_DIM)
    return jnp.concatenate([k, v], axis=1)


def kernel(x, c, positions, w_ada, b_ada, g_pre_mix, g_post_mix, w_in, g_q, w_uq, g_kv, w_ukv,
           conv_w_mix, conv_b_mix, w_o, g_pre_ffn, g_post_ffn, w_up, conv_w_ffn, conv_b_ffn, w_down):
    b, s, d = x.shape
    assert (s, d) == (SEQ, D_MODEL) and w_ada.shape[0] == 1
    t = b * s
    x2 = x.reshape(t, d)
    pos3 = positions.reshape(t // ATT_BLK, 1, ATT_BLK)
    half = jnp.arange(0, QK_ROPE_DIM, 2, dtype=F32) / QK_ROPE_DIM
    invf = (1.0 / (ROPE_THETA ** half)).reshape(QK_ROPE_DIM // 2, 1)

    rows = 8
    c_pad = jnp.zeros((rows, d), F32).at[:b].set(c)
    x_cur = x2
    for l in range(w_ada.shape[0]):
        mod = _adaln_mod(c_pad, w_ada[l], b_ada[l].reshape(1, -1))
        mod3 = mod[:b].reshape(b * N_MOD, 1, d)

        w_in_blk = _block_cols(_regroup_w_in(w_in[l].astype(BF16)), IN_PROJ_TN)
        wq_p = _regroup_w_uq(w_uq[l]).astype(BF16)
        wkv_p = _regroup_w_ukv(w_ukv[l]).astype(BF16)

        proj = _in_proj(x_cur, mod3, g_pre_mix[l].reshape(1, d), w_in_blk, tm=1024)
        qT, k, vT = _latent_qkv(proj, pos3, invf, g_q[l].reshape(1, -1), g_kv[l].reshape(1, -1),
                                wq_p, wkv_p, tm=ATT_BLK)
        attn = _attention(qT, k, vT, blk=ATT_BLK, heads=ATT_HEADS_PER_STEP)
        x_cur = _out_proj(attn, proj, conv_w_mix[l], conv_b_mix[l].reshape(1, -1), x_cur, mod3,
                          g_post_mix[l].reshape(1, d), w_o[l].astype(BF16), tm=512)
        x_cur = _conv_ffn(x_cur, mod3, g_pre_ffn[l].reshape(1, d), g_post_ffn[l].reshape(1, d),
                          w_up[l].astype(BF16), conv_w_ffn[l],
                          conv_b_ffn[l].reshape(1, -1), w_down[l].astype(BF16), tm=1024)
    return x_cur.reshape(b, s, d)
```

```python
import functools
import math

import jax
import jax.numpy as jnp
from jax import lax
from jax.experimental import pallas as pl
from jax.experimental.pallas import tpu as pltpu

D_MODEL = 2048
SEQ = 8192
CONV_WIDTH = 1024
MLA_HEADS = 8
QK_NOPE_DIM = 128
QK_ROPE_DIM = 64
V_HEAD_DIM = 128
Q_LORA_RANK = 768
KV_LORA_RANK = 512
ROPE_THETA = 10000.0
D_FF = 5632
RMS_EPS = 1e-6
N_MOD = 6

LANES = 128
HEAD_QK = 2 * LANES
PROJ_COLS = 4608
VMEM_LIMIT = 56 * 1024 * 1024
ATT_BLK = 512
ATT_HEADS_PER_STEP = 2
VT_ROWS = V_HEAD_DIM + 16
LOG2_E = 1.4426950408889634
IN_PROJ_TN = 1536
UP_DOT_ROWS = 512
UP_SUB = 512
FFN_TF = 512

OFF_GATE_B = 0
OFF_GATE_C = 1024
OFF_CONV_IN = 2048
OFF_Q_LAT = 3072
OFF_K_ROPE = 3840
OFF_KV_LAT = 4096

BF16 = jnp.bfloat16
F32 = jnp.float32


def _params(*sem):
    return pltpu.CompilerParams(dimension_semantics=sem, vmem_limit_bytes=VMEM_LIMIT)


def _rms(x, g):
    return x * lax.rsqrt(jnp.mean(x * x, axis=-1, keepdims=True) + RMS_EPS) * g


def _mod_kernel(c_ref, w_ref, b_ref, o_ref):
    c = c_ref[...]
    c_act = c * (1.0 / (1.0 + jnp.exp(-c)))
    o_ref[...] = jnp.dot(c_act.astype(BF16), w_ref[...].astype(BF16),
                         preferred_element_type=F32) + b_ref[...]


def _adaln_mod(c_pad, w_ada, b_ada):
    rows, d = c_pad.shape
    n = w_ada.shape[1]
    tn = 1536
    return pl.pallas_call(
        _mod_kernel,
        grid=(n // tn,),
        in_specs=[pl.BlockSpec((rows, d), lambda j: (0, 0)),
                  pl.BlockSpec((d, tn), lambda j: (0, j)),
                  pl.BlockSpec((1, tn), lambda j: (0, j))],
        out_specs=pl.BlockSpec((rows, tn), lambda j: (0, j)),
        out_shape=jax.ShapeDtypeStruct((rows, n), F32),
        compiler_params=_params("arbitrary"),
        name="adaln_mod",
    )(c_pad, w_ada, b_ada)


def _inproj_kernel(x_ref, sc_ref, sh_ref, g_ref, w_ref, o_ref, h_scr):
    tm = x_ref.shape[0]

    def project(rows):
        for r in range(0, tm, rows):
            o_ref[r:r + rows, :] = jnp.dot(h_scr[r:r + rows, :], w_ref[...],
                                           preferred_element_type=F32).astype(o_ref.dtype)

    @pl.when(pl.program_id(1) == 0)
    def _():
        rows = tm // 4
        for r in range(0, tm, rows):
            h = _rms(x_ref[r:r + rows, :], g_ref[...] * (1.0 + sc_ref[...])) + sh_ref[...]
            h_scr[r:r + rows, :] = h.astype(BF16)
        project(rows)

    @pl.when(pl.program_id(1) != 0)
    def _():
        project(tm)


def _in_proj(x2, mod3, g_pre, w_blk, tm):
    t, d = x2.shape
    nb, _, tn = w_blk.shape
    n = nb * tn
    per_b = SEQ // tm
    return pl.pallas_call(
        _inproj_kernel,
        grid=(t // tm, n // tn),
        in_specs=[pl.BlockSpec((tm, d), lambda i, j: (i, 0)),
                  pl.BlockSpec((None, 1, d), lambda i, j: ((i // per_b) * N_MOD + 1, 0, 0)),
                  pl.BlockSpec((None, 1, d), lambda i, j: ((i // per_b) * N_MOD + 0, 0, 0)),
                  pl.BlockSpec((1, d), lambda i, j: (0, 0)),
                  pl.BlockSpec((None, d, tn), lambda i, j: (j, 0, 0))],
        out_specs=pl.BlockSpec((tm, tn), lambda i, j: (i, j)),
        out_shape=jax.ShapeDtypeStruct((t, n), BF16),
        scratch_shapes=[pltpu.VMEM((tm, d), BF16)],
        compiler_params=_params("arbitrary", "arbitrary"),
        name="in_proj",
    )(x2, mod3, mod3, g_pre, w_blk)


def _qkv_kernel(ql_ref, kr_ref, kvl_ref, pos_ref, invf_ref, gq_ref, gkv_ref,
                wq_ref, wkv_ref, qT_ref, k_ref, vT_ref):
    tm = ql_ref.shape[0]
    ang = invf_ref[...] * pos_ref[...].astype(F32)
    cos = jnp.cos(ang)
    sin = jnp.sin(ang)
    cos2 = jnp.concatenate([cos, cos], axis=0)
    sin2 = jnp.concatenate([sin, sin], axis=0)

    def rope_t(a):
        at = a.T
        return at[:QK_ROPE_DIM, :] * cos2 + at[QK_ROPE_DIM:, :] * sin2

    qn = _rms(ql_ref[...].astype(F32), gq_ref[...]).astype(BF16)
    q = jnp.dot(qn, wq_ref[...], preferred_element_type=F32)
    scale = LOG2_E / math.sqrt(QK_NOPE_DIM + QK_ROPE_DIM)
    kvn = _rms(kvl_ref[...].astype(F32), gkv_ref[...]).astype(BF16)
    kv = jnp.dot(kvn, wkv_ref[...], preferred_element_type=F32)
    zeros_t = jnp.zeros((LANES - QK_ROPE_DIM, tm), F32)
    k_rope = jnp.concatenate([rope_t(kr_ref[...].astype(F32)), zeros_t], axis=0).T.astype(BF16)
    v_off = MLA_HEADS * QK_NOPE_DIM
    ones = jnp.ones((VT_ROWS - V_HEAD_DIM, tm), BF16)
    for h in range(MLA_HEADS):
        lo = h * HEAD_QK
        qT_ref[lo:lo + LANES, :] = (q[:, lo:lo + LANES] * scale).T.astype(BF16)
        qT_ref[lo + LANES:lo + LANES + QK_ROPE_DIM, :] = (rope_t(q[:, lo + LANES:lo + HEAD_QK]) * scale).astype(BF16)
        qT_ref[lo + LANES + QK_ROPE_DIM:lo + HEAD_QK, :] = zeros_t.astype(BF16)
        grp, klo = divmod(h, ATT_HEADS_PER_STEP)
        klo *= HEAD_QK
        k_ref[grp, :, klo:klo + LANES] = kv[:, h * LANES:(h + 1) * LANES].astype(BF16)
        k_ref[grp, :, klo + LANES:klo + HEAD_QK] = k_rope
        vlo = h * VT_ROWS
        vT_ref[vlo:vlo + V_HEAD_DIM, :] = kv[:, v_off + h * LANES:v_off + (h + 1) * LANES].T.astype(BF16)
        vT_ref[vlo + V_HEAD_DIM:vlo + VT_ROWS, :] = ones


def _latent_qkv(proj, pos3, invf, g_q, g_kv, wq_p, wkv_p, tm):
    t = proj.shape[0]
    hq = MLA_HEADS * HEAD_QK
    hv = MLA_HEADS * V_HEAD_DIM
    hvt = MLA_HEADS * VT_ROWS
    groups = MLA_HEADS // ATT_HEADS_PER_STEP
    const = lambda i: (0, 0)
    return pl.pallas_call(
        _qkv_kernel,
        grid=(t // tm,),
        in_specs=[pl.BlockSpec((tm, Q_LORA_RANK), lambda i: (i, OFF_Q_LAT // Q_LORA_RANK)),
                  pl.BlockSpec((tm, LANES), lambda i: (i, OFF_K_ROPE // LANES)),
                  pl.BlockSpec((tm, KV_LORA_RANK), lambda i: (i, OFF_KV_LAT // KV_LORA_RANK)),
                  pl.BlockSpec((None, 1, tm), lambda i: (i, 0, 0)),
                  pl.BlockSpec((QK_ROPE_DIM // 2, 1), const),
                  pl.BlockSpec((1, Q_LORA_RANK), const),
                  pl.BlockSpec((1, KV_LORA_RANK), const),
                  pl.BlockSpec((Q_LORA_RANK, hq), const),
                  pl.BlockSpec((KV_LORA_RANK, 2 * hv), const)],
        out_specs=[pl.BlockSpec((None, hq, tm), lambda i: (i, 0, 0)),
                   pl.BlockSpec((groups, tm, hq // groups), lambda i: (0, i, 0)),
                   pl.BlockSpec((None, hvt, tm), lambda i: (i, 0, 0))],
        out_shape=[jax.ShapeDtypeStruct((t // tm, hq, tm), BF16),
                   jax.ShapeDtypeStruct((groups, t, hq // groups), BF16),
                   jax.ShapeDtypeStruct((t // tm, hvt, tm), BF16)],
        compiler_params=_params("arbitrary"),
        name="latent_qkv",
    )(proj, proj, proj, pos3, invf, g_q, g_kv, wq_p, wkv_p)


def _causal_conv3(u, halo, w_ref, b_ref):
    w0, w1, w2, b = w_ref[0:1, :], w_ref[1:2, :], w_ref[2:3, :], b_ref[...]

    def taps(p2, p1, p0):
        return p2 * w0 + p1 * w1 + p0 * w2 + b

    body = taps(pltpu.roll(u, 2, 0), pltpu.roll(u, 1, 0), u)
    head = jnp.concatenate([halo, u[0:8, :]], axis=0)
    first = taps(pltpu.roll(head, 2, 0), pltpu.roll(head, 1, 0), head)[8:16, :]
    return jnp.concatenate([first, body[8:, :]], axis=0)


def _attn_kernel(qT_ref, k_ref, vT_ref, o_ref, *scratch, blk, heads):
    qi = pl.program_id(2)
    per_head = 4
    state = [scratch[per_head * hh:per_head * (hh + 1)] for hh in range(heads)]

    def q_rows(hh):
        return slice(hh * HEAD_QK, (hh + 1) * HEAD_QK)

    def scores(hh, j):
        start = pl.multiple_of(j * blk, blk)
        return jnp.dot(k_ref[pl.ds(start, blk), q_rows(hh)], qT_ref[q_rows(hh), :],
                       preferred_element_type=F32)

    def consume(hh, sT, j):
        _, _, m_scr, acc_scr = state[hh]
        m_old = m_scr[...]
        m_new = jnp.maximum(m_old, jnp.max(sT, axis=0, keepdims=True))
        alpha = jnp.exp2(m_old - m_new)
        pT = jnp.exp2(sT - m_new).astype(BF16)
        vT = vT_ref[j, hh * VT_ROWS:(hh + 1) * VT_ROWS, :]
        acc_scr[...] = alpha * acc_scr[...] + jnp.dot(vT, pT, preferred_element_type=F32)
        m_scr[...] = m_new

    def consume_diag(hh, s_ref):
        sT = s_ref[...]
        r = lax.broadcasted_iota(jnp.int32, sT.shape, 0)
        c = lax.broadcasted_iota(jnp.int32, sT.shape, 1)
        consume(hh, jnp.where(r <= c, sT, -1e30), qi)

    def pair(j):
        for hh in range(heads):
            sa, sb, _, _ = state[hh]
            sb[...] = scores(hh, j + 1)
            consume(hh, sa[...], j)
        for hh in range(heads):
            sa, sb, _, _ = state[hh]
            sa[...] = scores(hh, j + 2)
            consume(hh, sb[...], j + 1)

    for hh in range(heads):
        sa, _, m_scr, acc_scr = state[hh]
        m_scr[...] = jnp.full(m_scr.shape, -1e30, F32)
        acc_scr[...] = jnp.zeros(acc_scr.shape, F32)
        sa[...] = scores(hh, 0)

    def body(t, carry):
        pair(4 * t)
        pair(4 * t + 2)
        return carry

    lax.fori_loop(0, qi // 4, body, 0)
    rem = qi % 4

    @pl.when(rem >= 2)
    def _():
        pair(qi - rem)

    @pl.when(rem % 2 == 0)
    def _():
        for hh in range(heads):
            consume_diag(hh, state[hh][0])

    @pl.when(rem % 2 == 1)
    def _():
        for hh in range(heads):
            sa, sb, _, _ = state[hh]
            sb[...] = scores(hh, qi)
            consume(hh, sa[...], qi - 1)
        for hh in range(heads):
            consume_diag(hh, state[hh][1])

    for hh in range(heads):
        acc = state[hh][3][...]
        o_ref[:, hh * V_HEAD_DIM:(hh + 1) * V_HEAD_DIM] = (
            acc[:V_HEAD_DIM] / acc[V_HEAD_DIM:V_HEAD_DIM + 1]).T.astype(o_ref.dtype)


def _attention(qT, k, vT, blk, heads):
    t = k.shape[1]
    b = t // SEQ
    nq = SEQ // blk
    per_head_scratch = [pltpu.VMEM((blk, blk), F32),
                        pltpu.VMEM((blk, blk), F32),
                        pltpu.VMEM((1, blk), F32),
                        pltpu.VMEM((VT_ROWS, blk), F32)]
    return pl.pallas_call(
        functools.partial(_attn_kernel, blk=blk, heads=heads),
        grid=(b, MLA_HEADS // heads, nq),
        in_specs=[pl.BlockSpec((None, heads * HEAD_QK, blk), lambda bi, g, i: (bi * nq + i, g, 0)),
                  pl.BlockSpec((None, SEQ, heads * HEAD_QK), lambda bi, g, i: (g, bi, 0)),
                  pl.BlockSpec((nq, heads * VT_ROWS, blk), lambda bi, g, i: (bi, g, 0))],
        out_specs=pl.BlockSpec((blk, heads * V_HEAD_DIM), lambda bi, g, i: (bi * nq + i, g)),
        out_shape=jax.ShapeDtypeStruct((t, MLA_HEADS * V_HEAD_DIM), BF16),
        scratch_shapes=per_head_scratch * heads,
        compiler_params=_params("arbitrary", "arbitrary", "arbitrary"),
        name="attention",
    )(qT, k, vT)


def _outproj_kernel(a_ref, gb_ref, gc_ref, ci_ref, gch_ref, cih_ref, cw_ref, cb_ref,
                    x_ref, gt_ref, g_ref, wa_ref, wc_ref, o_ref, *, per_b):
    first = (pl.program_id(0) % per_b) == 0
    g = gc_ref[...].astype(F32) * ci_ref[...].astype(F32)
    gh = gch_ref[...].astype(F32) * cih_ref[...].astype(F32)
    halo = jnp.where(first, 0.0, gh[8:16, :])
    conv = (gb_ref[...].astype(F32) * _causal_conv3(g, halo, cw_ref, cb_ref)).astype(BF16)
    mix = (jnp.dot(a_ref[...], wa_ref[...], preferred_element_type=F32)
           + jnp.dot(conv, wc_ref[...], preferred_element_type=F32))
    o_ref[...] = x_ref[...] + _rms(mix, gt_ref[...] * g_ref[...])


def _out_proj(attn, proj, conv_w, conv_b, x2, mod3, g_post, w_o_bf, tm):
    t, d = x2.shape
    ka = attn.shape[1]
    c = CONV_WIDTH
    hb = 16
    per_b = SEQ // tm
    halo_idx = lambda col: (lambda i: (jnp.maximum(i * (tm // hb) - 1, 0), col))
    return pl.pallas_call(
        functools.partial(_outproj_kernel, per_b=per_b),
        grid=(t // tm,),
        in_specs=[pl.BlockSpec((tm, ka), lambda i: (i, 0)),
                  pl.BlockSpec((tm, c), lambda i: (i, OFF_GATE_B // c)),
                  pl.BlockSpec((tm, c), lambda i: (i, OFF_GATE_C // c)),
                  pl.BlockSpec((tm, c), lambda i: (i, OFF_CONV_IN // c)),
                  pl.BlockSpec((hb, c), halo_idx(OFF_GATE_C // c)),
                  pl.BlockSpec((hb, c), halo_idx(OFF_CONV_IN // c)),
                  pl.BlockSpec((3, c), lambda i: (0, 0)),
                  pl.BlockSpec((1, c), lambda i: (0, 0)),
                  pl.BlockSpec((tm, d), lambda i: (i, 0)),
                  pl.BlockSpec((None, 1, d), lambda i: ((i // per_b) * N_MOD + 2, 0, 0)),
                  pl.BlockSpec((1, d), lambda i: (0, 0)),
                  pl.BlockSpec((ka, d), lambda i: (0, 0)),
                  pl.BlockSpec((c, d), lambda i: (ka // c, 0))],
        out_specs=pl.BlockSpec((tm, d), lambda i: (i, 0)),
        out_shape=jax.ShapeDtypeStruct((t, d), F32),
        compiler_params=_params("arbitrary"),
        name="out_proj",
    )(attn, proj, proj, proj, proj, proj, conv_w, conv_b, x2, mod3, g_post, w_o_bf, w_o_bf)


def _ffn_kernel(x_ref, sc_ref, sh_ref, gt_ref, gpre_ref, gpost_ref, wa_ref, wg_ref,
                cwa_ref, cwg_ref, cba_ref, cbg_ref, wd_ref, o_ref,
                h_scr, halo_a, halo_g, y0_scr, y1_scr, *, per_b, nf, sub):
    i = pl.program_id(0)
    f = pl.program_id(1)
    tm = x_ref.shape[0]
    tf = wa_ref.shape[1]
    first = (i % per_b) == 0

    def branch(w_ref, cw_ref, cb_ref, halo_ref, cols):
        w = w_ref[:, cols]
        u = jnp.concatenate([jnp.dot(h_scr[r:r + UP_DOT_ROWS, :], w, preferred_element_type=F32)
                             for r in range(0, tm, UP_DOT_ROWS)], axis=0)
        halo = jnp.where(first, 0.0, halo_ref[f, :, cols])
        halo_ref[f, :, cols] = u[tm - 8:, :]
        return _causal_conv3(u, halo, cw_ref.at[:, cols], cb_ref.at[:, cols])

    def up(y_out):
        for s in range(0, tf, sub):
            cols = slice(s, s + sub)
            a = branch(wa_ref, cwa_ref, cba_ref, halo_a, cols)
            g = branch(wg_ref, cwg_ref, cbg_ref, halo_g, cols)
            hg = 0.5 * g
            y_out[:, cols] = ((hg + hg * jnp.tanh(hg)) * a).astype(BF16)

    def down(y_in):
        for r in range(0, tm, UP_DOT_ROWS):
            rows = slice(r, r + UP_DOT_ROWS)
            o_ref[rows, :] += jnp.dot(y_in[rows, :], wd_ref[...], preferred_element_type=F32)

    @pl.when(f == 0)
    def _():
        h = _rms(x_ref[...], gpre_ref[...] * (1.0 + sc_ref[...])) + sh_ref[...]
        h_scr[...] = h.astype(BF16)
        o_ref[...] = jnp.zeros(o_ref.shape, F32)
        up(y0_scr)

    interior = jnp.logical_and(f > 0, f < nf)

    @pl.when(jnp.logical_and(interior, f % 2 == 1))
    def _():
        up(y1_scr)
        down(y0_scr)

    @pl.when(jnp.logical_and(interior, f % 2 == 0))
    def _():
        up(y0_scr)
        down(y1_scr)

    @pl.when(f == nf)
    def _():
        down(y1_scr if (nf - 1) % 2 == 1 else y0_scr)
        o_ref[...] = x_ref[...] + _rms(o_ref[...], gt_ref[...] * gpost_ref[...])


def _conv_ffn(x1, mod3, g_pre, g_post, w_up_bf, conv_w, conv_b, w_down_bf, tm):
    t, d = x1.shape
    tf = FFN_TF
    nf = D_FF // tf
    per_b = SEQ // tm
    modspec = lambda k: pl.BlockSpec((None, 1, d), lambda i, f: ((i // per_b) * N_MOD + k, 0, 0))
    up_f = lambda f: jnp.minimum(f, nf - 1)
    dn_f = lambda f: jnp.maximum(f - 1, 0)
    return pl.pallas_call(
        functools.partial(_ffn_kernel, per_b=per_b, nf=nf, sub=UP_SUB),
        grid=(t // tm, nf + 1),
        in_specs=[pl.BlockSpec((tm, d), lambda i, f: (i, 0)),
                  modspec(4), modspec(3), modspec(5),
                  pl.BlockSpec((1, d), lambda i, f: (0, 0)),
                  pl.BlockSpec((1, d), lambda i, f: (0, 0)),
                  pl.BlockSpec((d, tf), lambda i, f: (0, up_f(f))),
                  pl.BlockSpec((d, tf), lambda i, f: (0, nf + up_f(f))),
                  pl.BlockSpec((3, tf), lambda i, f: (0, up_f(f))),
                  pl.BlockSpec((3, tf), lambda i, f: (0, nf + up_f(f))),
                  pl.BlockSpec((1, tf), lambda i, f: (0, up_f(f))),
                  pl.BlockSpec((1, tf), lambda i, f: (0, nf + up_f(f))),
                  pl.BlockSpec((tf, d), lambda i, f: (dn_f(f), 0))],
        out_specs=pl.BlockSpec((tm, d), lambda i, f: (i, 0)),
        out_shape=jax.ShapeDtypeStruct((t, d), F32),
        scratch_shapes=[pltpu.VMEM((tm, d), BF16),
                        pltpu.VMEM((nf, 8, tf), F32),
                        pltpu.VMEM((nf, 8, tf), F32),
                        pltpu.VMEM((tm, tf), BF16),
                        pltpu.VMEM((tm, tf), BF16)],
        compiler_params=_params("arbitrary", "arbitrary"),
        name="conv_ffn",
    )(x1, mod3, mod3, mod3, g_pre, g_post, w_up_bf, w_up_bf, conv_w, conv_w,
      conv_b, conv_b, w_down_bf)


def _rot_half_cols(w):
    half = w.shape[-1] // 2
    return jnp.concatenate([-w[..., half:], w[..., :half]], axis=-1)


def _regroup_w_in(w):
    d = w.shape[0]
    c0 = Q_LORA_RANK
    c1 = c0 + KV_LORA_RANK
    c2 = c1 + QK_ROPE_DIM
    c3 = c2 + CONV_WIDTH
    c4 = c3 + CONV_WIDTH
    q_lat, kv_lat, k_rope = w[:, :c0], w[:, c0:c1], w[:, c1:c2]
    gate_b, gate_c, conv_in = w[:, c2:c3], w[:, c3:c4], w[:, c4:]
    pad = jnp.zeros((d, OFF_KV_LAT - OFF_K_ROPE - 2 * QK_ROPE_DIM), w.dtype)
    return jnp.concatenate([gate_b, gate_c, conv_in, q_lat, k_rope, _rot_half_cols(k_rope),
                            pad, kv_lat], axis=1)


def _block_cols(w, tn):
    k, n = w.shape
    return w.reshape(k, n // tn, tn).transpose(1, 0, 2)


def _regroup_w_uq(w):
    r = w.shape[0]
    w = w.reshape(r, MLA_HEADS, QK_NOPE_DIM + QK_ROPE_DIM)
    nope, rope = w[..., :QK_NOPE_DIM], w[..., QK_NOPE_DIM:]
    return jnp.concatenate([nope, rope, _rot_half_cols(rope)], axis=-1).reshape(r, MLA_HEADS * HEAD_QK)


def _regroup_w_ukv(w):
    r = w.shape[0]
    w = w.reshape(r, MLA_HEADS, QK_NOPE_DIM + V_HEAD_DIM)
    k = w[..., :QK_NOPE_DIM].reshape(r, MLA_HEADS * QK_NOPE_DIM)
    v = w[..., QK_NOPE_DIM:].reshape(r, MLA_HEADS * V_HEAD_DIM)
    return jnp.concatenate([k, v], axis=1)


def kernel(x, c, positions, w_ada, b_ada, g_pre_mix, g_post_mix, w_in, g_q, w_uq, g_kv, w_ukv,
           conv_w_mix, conv_b_mix, w_o, g_pre_ffn, g_post_ffn, w_up, conv_w_ffn, conv_b_ffn, w_down):
    b, s, d = x.shape
    assert (s, d) == (SEQ, D_MODEL) and w_ada.shape[0] == 1
    t = b * s
    x2 = x.reshape(t, d)
    pos3 = positions.reshape(t // ATT_BLK, 1, ATT_BLK)
    half = jnp.arange(0, QK_ROPE_DIM, 2, dtype=F32) / QK_ROPE_DIM
    invf = (1.0 / (ROPE_THETA ** half)).reshape(QK_ROPE_DIM // 2, 1)

    rows = 8
    c_pad = jnp.zeros((rows, d), F32).at[:b].set(c)
    x_cur = x2
    for l in range(w_ada.shape[0]):
        mod = _adaln_mod(c_pad, w_ada[l], b_ada[l].reshape(1, -1))
        mod3 = mod[:b].reshape(b * N_MOD, 1, d)

        w_in_blk = _block_cols(_regroup_w_in(w_in[l].astype(BF16)), IN_PROJ_TN)
        wq_p = _regroup_w_uq(w_uq[l]).astype(BF16)
        wkv_p = _regroup_w_ukv(w_ukv[l]).astype(BF16)

        proj = _in_proj(x_cur, mod3, g_pre_mix[l].reshape(1, d), w_in_blk, tm=1024)
        qT, k, vT = _latent_qkv(proj, pos3, invf, g_q[l].reshape(1, -1), g_kv[l].reshape(1, -1),
                                wq_p, wkv_p, tm=ATT_BLK)
        attn = _attention(qT, k, vT, blk=ATT_BLK, heads=ATT_HEADS_PER_STEP)
        x_cur = _out_proj(attn, proj, conv_w_mix[l], conv_b_mix[l].reshape(1, -1), x_cur, mod3,
                          g_post_mix[l].reshape(1, d), w_o[l].astype(BF16), tm=512)
        x_cur = _conv_ffn(x_cur, mod3, g_pre_ffn[l].reshape(1, d), g_post_ffn[l].reshape(1, d),
                          w_up[l].astype(BF16), conv_w_ffn[l],
                          conv_b_ffn[l].reshape(1, -1), w_down[l].astype(BF16), tm=1024)
    return x_cur.reshape(b, s, d)
```

```python
import functools
import math

import jax
import jax.numpy as jnp
from jax import lax
from jax.experimental import pallas as pl
from jax.experimental.pallas import tpu as pltpu

D_MODEL = 2048
SEQ = 8192
CONV_WIDTH = 1024
MLA_HEADS = 8
QK_NOPE_DIM = 128
QK_ROPE_DIM = 64
V_HEAD_DIM = 128
Q_LORA_RANK = 768
KV_LORA_RANK = 512
ROPE_THETA = 10000.0
D_FF = 5632
RMS_EPS = 1e-6
N_MOD = 6

LANES = 128
HEAD_QK = 2 * LANES
PROJ_COLS = 4608
VMEM_LIMIT = 56 * 1024 * 1024
ATT_BLK = 512
ATT_HEADS_PER_STEP = 2
ATT_Q_BLOCKS_PER_STEP = 2
VT_ROWS = V_HEAD_DIM + 16
LOG2_E = 1.4426950408889634
IN_PROJ_TN = 1536
UP_DOT_ROWS = 512
UP_SUB = 512
FFN_TF = 512

OFF_GATE_B = 0
OFF_GATE_C = 1024
OFF_CONV_IN = 2048
OFF_Q_LAT = 3072
OFF_K_ROPE = 3840
OFF_KV_LAT = 4096

BF16 = jnp.bfloat16
F32 = jnp.float32


def _params(*sem):
    return pltpu.CompilerParams(dimension_semantics=sem, vmem_limit_bytes=VMEM_LIMIT)


def _rms(x, g):
    return x * lax.rsqrt(jnp.mean(x * x, axis=-1, keepdims=True) + RMS_EPS) * g


def _mod_kernel(c_ref, w_ref, b_ref, o_ref):
    c = c_ref[...]
    c_act = c * (1.0 / (1.0 + jnp.exp(-c)))
    o_ref[...] = jnp.dot(c_act.astype(BF16), w_ref[...].astype(BF16),
                         preferred_element_type=F32) + b_ref[...]


def _adaln_mod(c_pad, w_ada, b_ada):
    rows, d = c_pad.shape
    n = w_ada.shape[1]
    tn = 1536
    return pl.pallas_call(
        _mod_kernel,
        grid=(n // tn,),
        in_specs=[pl.BlockSpec((rows, d), lambda j: (0, 0)),
                  pl.BlockSpec((d, tn), lambda j: (0, j)),
                  pl.BlockSpec((1, tn), lambda j: (0, j))],
        out_specs=pl.BlockSpec((rows, tn), lambda j: (0, j)),
        out_shape=jax.ShapeDtypeStruct((rows, n), F32),
        compiler_params=_params("arbitrary"),
        name="adaln_mod",
    )(c_pad, w_ada, b_ada)


def _inproj_kernel(x_ref, sc_ref, sh_ref, g_ref, w_ref, o_ref, h_scr):
    tm = x_ref.shape[0]

    def project(rows):
        for r in range(0, tm, rows):
            o_ref[r:r + rows, :] = jnp.dot(h_scr[r:r + rows, :], w_ref[...],
                                           preferred_element_type=F32).astype(o_ref.dtype)

    @pl.when(pl.program_id(1) == 0)
    def _():
        rows = tm // 4
        for r in range(0, tm, rows):
            h = _rms(x_ref[r:r + rows, :], g_ref[...] * (1.0 + sc_ref[...])) + sh_ref[...]
            h_scr[r:r + rows, :] = h.astype(BF16)
        project(rows)

    @pl.when(pl.program_id(1) != 0)
    def _():
        project(tm // 2)


def _in_proj(x2, mod3, g_pre, w_blk, tm):
    t, d = x2.shape
    nb, _, tn = w_blk.shape
    n = nb * tn
    per_b = SEQ // tm
    return pl.pallas_call(
        _inproj_kernel,
        grid=(t // tm, n // tn),
        in_specs=[pl.BlockSpec((tm, d), lambda i, j: (i, 0)),
                  pl.BlockSpec((None, 1, d), lambda i, j: ((i // per_b) * N_MOD + 1, 0, 0)),
                  pl.BlockSpec((None, 1, d), lambda i, j: ((i // per_b) * N_MOD + 0, 0, 0)),
                  pl.BlockSpec((1, d), lambda i, j: (0, 0)),
                  pl.BlockSpec((None, d, tn), lambda i, j: (j, 0, 0))],
        out_specs=pl.BlockSpec((tm, tn), lambda i, j: (i, j)),
        out_shape=jax.ShapeDtypeStruct((t, n), BF16),
        scratch_shapes=[pltpu.VMEM((tm, d), BF16)],
        compiler_params=_params("arbitrary", "arbitrary"),
        name="in_proj",
    )(x2, mod3, mod3, g_pre, w_blk)


def _qkv_kernel(ql_ref, kr_ref, kvl_ref, pos_ref, invf_ref, gq_ref, gkv_ref,
                wq_ref, wkv_ref, qT_ref, k_ref, vT_ref):
    tm = ql_ref.shape[0]
    ang = invf_ref[...] * pos_ref[...].astype(F32)
    cos = jnp.cos(ang)
    sin = jnp.sin(ang)
    cos2 = jnp.concatenate([cos, cos], axis=0)
    sin2 = jnp.concatenate([sin, sin], axis=0)

    def rope_t(a):
        at = a.T
        return at[:QK_ROPE_DIM, :] * cos2 + at[QK_ROPE_DIM:, :] * sin2

    qn = _rms(ql_ref[...].astype(F32), gq_ref[...]).astype(BF16)
    q = jnp.dot(qn, wq_ref[...], preferred_element_type=F32)
    scale = LOG2_E / math.sqrt(QK_NOPE_DIM + QK_ROPE_DIM)
    kvn = _rms(kvl_ref[...].astype(F32), gkv_ref[...]).astype(BF16)
    kv = jnp.dot(kvn, wkv_ref[...], preferred_element_type=F32)
    zeros_t = jnp.zeros((LANES - QK_ROPE_DIM, tm), F32)
    k_rope = jnp.concatenate([rope_t(kr_ref[...].astype(F32)), zeros_t], axis=0).T.astype(BF16)
    v_off = MLA_HEADS * QK_NOPE_DIM
    ones = jnp.ones((VT_ROWS - V_HEAD_DIM, tm), BF16)
    for h in range(MLA_HEADS):
        lo = h * HEAD_QK
        qT_ref[lo:lo + LANES, :] = (q[:, lo:lo + LANES] * scale).T.astype(BF16)
        qT_ref[lo + LANES:lo + LANES + QK_ROPE_DIM, :] = (rope_t(q[:, lo + LANES:lo + HEAD_QK]) * scale).astype(BF16)
        qT_ref[lo + LANES + QK_ROPE_DIM:lo + HEAD_QK, :] = zeros_t.astype(BF16)
        k_ref[:, lo:lo + LANES] = kv[:, h * LANES:(h + 1) * LANES].astype(BF16)
        k_ref[:, lo + LANES:lo + HEAD_QK] = k_rope
        vlo = h * VT_ROWS
        vT_ref[vlo:vlo + V_HEAD_DIM, :] = kv[:, v_off + h * LANES:v_off + (h + 1) * LANES].T.astype(BF16)
        vT_ref[vlo + V_HEAD_DIM:vlo + VT_ROWS, :] = ones


def _latent_qkv(proj, pos3, invf, g_q, g_kv, wq_p, wkv_p, tm):
    t = proj.shape[0]
    hq = MLA_HEADS * HEAD_QK
    hv = MLA_HEADS * V_HEAD_DIM
    hvt = MLA_HEADS * VT_ROWS
    const = lambda i: (0, 0)
    return pl.pallas_call(
        _qkv_kernel,
        grid=(t // tm,),
        in_specs=[pl.BlockSpec((tm, Q_LORA_RANK), lambda i: (i, OFF_Q_LAT // Q_LORA_RANK)),
                  pl.BlockSpec((tm, LANES), lambda i: (i, OFF_K_ROPE // LANES)),
                  pl.BlockSpec((tm, KV_LORA_RANK), lambda i: (i, OFF_KV_LAT // KV_LORA_RANK)),
                  pl.BlockSpec((None, 1, tm), lambda i: (i, 0, 0)),
                  pl.BlockSpec((QK_ROPE_DIM // 2, 1), const),
                  pl.BlockSpec((1, Q_LORA_RANK), const),
                  pl.BlockSpec((1, KV_LORA_RANK), const),
                  pl.BlockSpec((Q_LORA_RANK, hq), const),
                  pl.BlockSpec((KV_LORA_RANK, 2 * hv), const)],
        out_specs=[pl.BlockSpec((None, hq, tm), lambda i: (i, 0, 0)),
                   pl.BlockSpec((tm, hq), lambda i: (i, 0)),
                   pl.BlockSpec((None, hvt, tm), lambda i: (i, 0, 0))],
        out_shape=[jax.ShapeDtypeStruct((t // tm, hq, tm), BF16),
                   jax.ShapeDtypeStruct((t, hq), BF16),
                   jax.ShapeDtypeStruct((t // tm, hvt, tm), BF16)],
        compiler_params=_params("arbitrary"),
        name="latent_qkv",
    )(proj, proj, proj, pos3, invf, g_q, g_kv, wq_p, wkv_p)


def _causal_conv3(u, halo, w_ref, b_ref):
    w0, w1, w2, b = w_ref[0:1, :], w_ref[1:2, :], w_ref[2:3, :], b_ref[...]

    def taps(p2, p1, p0):
        return p2 * w0 + p1 * w1 + p0 * w2 + b

    body = taps(pltpu.roll(u, 2, 0), pltpu.roll(u, 1, 0), u)
    head = jnp.concatenate([halo, u[0:8, :]], axis=0)
    first = taps(pltpu.roll(head, 2, 0), pltpu.roll(head, 1, 0), head)[8:16, :]
    return jnp.concatenate([first, body[8:, :]], axis=0)


def _attn_kernel(qT_ref, k_ref, vT_ref, o_ref, *scratch, blk, heads, q_blocks):
    per_head = 4
    state = [scratch[per_head * hh:per_head * (hh + 1)] for hh in range(heads)]

    def q_rows(hh):
        return slice(hh * HEAD_QK, (hh + 1) * HEAD_QK)

    def one_q_block(qb):
        qi = pl.program_id(2) * q_blocks + qb

        def scores(hh, j):
            start = pl.multiple_of(j * blk, blk)
            return jnp.dot(k_ref[pl.ds(start, blk), q_rows(hh)], qT_ref[qb, q_rows(hh), :],
                           preferred_element_type=F32)

        def consume(hh, sT, j):
            _, _, m_scr, acc_scr = state[hh]
            m_old = m_scr[...]
            m_new = jnp.maximum(m_old, jnp.max(sT, axis=0, keepdims=True))
            alpha = jnp.exp2(m_old - m_new)
            pT = jnp.exp2(sT - m_new).astype(BF16)
            vT = vT_ref[j, hh * VT_ROWS:(hh + 1) * VT_ROWS, :]
            acc_scr[...] = alpha * acc_scr[...] + jnp.dot(vT, pT, preferred_element_type=F32)
            m_scr[...] = m_new

        def consume_diag(hh, s_ref):
            sT = s_ref[...]
            r = lax.broadcasted_iota(jnp.int32, sT.shape, 0)
            c = lax.broadcasted_iota(jnp.int32, sT.shape, 1)
            consume(hh, jnp.where(r <= c, sT, -1e30), qi)

        def pair(j):
            for hh in range(heads):
                sa, sb, _, _ = state[hh]
                sb[...] = scores(hh, j + 1)
                consume(hh, sa[...], j)
            for hh in range(heads):
                sa, sb, _, _ = state[hh]
                sa[...] = scores(hh, j + 2)
                consume(hh, sb[...], j + 1)

        for hh in range(heads):
            sa, _, m_scr, acc_scr = state[hh]
            m_scr[...] = jnp.full(m_scr.shape, -1e30, F32)
            acc_scr[...] = jnp.zeros(acc_scr.shape, F32)
            sa[...] = scores(hh, 0)

        def body(t, carry):
            pair(4 * t)
            pair(4 * t + 2)
            return carry

        lax.fori_loop(0, qi // 4, body, 0)
        rem = qi % 4

        @pl.when(rem >= 2)
        def _():
            pair(qi - rem)

        @pl.when(rem % 2 == 0)
        def _():
            for hh in range(heads):
                consume_diag(hh, state[hh][0])

        @pl.when(rem % 2 == 1)
        def _():
            for hh in range(heads):
                sa, sb, _, _ = state[hh]
                sb[...] = scores(hh, qi)
                consume(hh, sa[...], qi - 1)
            for hh in range(heads):
                consume_diag(hh, state[hh][1])

        for hh in range(heads):
            acc = state[hh][3][...]
            o_ref[qb * blk:(qb + 1) * blk, hh * V_HEAD_DIM:(hh + 1) * V_HEAD_DIM] = (
                acc[:V_HEAD_DIM] / acc[V_HEAD_DIM:V_HEAD_DIM + 1]).T.astype(o_ref.dtype)

    for qb in range(q_blocks):
        one_q_block(qb)


def _attention(qT, k, vT, blk, heads, q_blocks):
    t = k.shape[0]
    b = t // SEQ
    nq = SEQ // blk
    nqs = nq // q_blocks
    per_head_scratch = [pltpu.VMEM((blk, blk), F32),
                        pltpu.VMEM((blk, blk), F32),
                        pltpu.VMEM((1, blk), F32),
                        pltpu.VMEM((VT_ROWS, blk), F32)]
    return pl.pallas_call(
        functools.partial(_attn_kernel, blk=blk, heads=heads, q_blocks=q_blocks),
        grid=(b, MLA_HEADS // heads, nqs),
        in_specs=[pl.BlockSpec((q_blocks, heads * HEAD_QK, blk), lambda bi, g, i: (bi * nqs + i, g, 0)),
                  pl.BlockSpec((SEQ, heads * HEAD_QK), lambda bi, g, i: (bi, g)),
                  pl.BlockSpec((nq, heads * VT_ROWS, blk), lambda bi, g, i: (bi, g, 0))],
        out_specs=pl.BlockSpec((q_blocks * blk, heads * V_HEAD_DIM), lambda bi, g, i: (bi * nqs + i, g)),
        out_shape=jax.ShapeDtypeStruct((t, MLA_HEADS * V_HEAD_DIM), BF16),
        scratch_shapes=per_head_scratch * heads,
        compiler_params=_params("arbitrary", "arbitrary", "arbitrary"),
        name="attention",
    )(qT, k, vT)


def _outproj_kernel(a_ref, gb_ref, gc_ref, ci_ref, gch_ref, cih_ref, cw_ref, cb_ref,
                    x_ref, gt_ref, g_ref, wa_ref, wc_ref, o_ref, *, per_b):
    first = (pl.program_id(0) % per_b) == 0
    g = gc_ref[...].astype(F32) * ci_ref[...].astype(F32)
    gh = gch_ref[...].astype(F32) * cih_ref[...].astype(F32)
    halo = jnp.where(first, 0.0, gh[8:16, :])
    conv = (gb_ref[...].astype(F32) * _causal_conv3(g, halo, cw_ref, cb_ref)).astype(BF16)
    mix = (jnp.dot(a_ref[...], wa_ref[...], preferred_element_type=F32)
           + jnp.dot(conv, wc_ref[...], preferred_element_type=F32))
    o_ref[...] = x_ref[...] + _rms(mix, gt_ref[...] * g_ref[...])


def _out_proj(attn, proj, conv_w, conv_b, x2, mod3, g_post, w_o_bf, tm):
    t, d = x2.shape
    ka = attn.shape[1]
    c = CONV_WIDTH
    hb = 16
    per_b = SEQ // tm
    halo_idx = lambda col: (lambda i: (jnp.maximum(i * (tm // hb) - 1, 0), col))
    return pl.pallas_call(
        functools.partial(_outproj_kernel, per_b=per_b),
        grid=(t // tm,),
        in_specs=[pl.BlockSpec((tm, ka), lambda i: (i, 0)),
                  pl.BlockSpec((tm, c), lambda i: (i, OFF_GATE_B // c)),
                  pl.BlockSpec((tm, c), lambda i: (i, OFF_GATE_C // c)),
                  pl.BlockSpec((tm, c), lambda i: (i, OFF_CONV_IN // c)),
                  pl.BlockSpec((hb, c), halo_idx(OFF_GATE_C // c)),
                  pl.BlockSpec((hb, c), halo_idx(OFF_CONV_IN // c)),
                  pl.BlockSpec((3, c), lambda i: (0, 0)),
                  pl.BlockSpec((1, c), lambda i: (0, 0)),
                  pl.BlockSpec((tm, d), lambda i: (i, 0)),
                  pl.BlockSpec((None, 1, d), lambda i: ((i // per_b) * N_MOD + 2, 0, 0)),
                  pl.BlockSpec((1, d), lambda i: (0, 0)),
                  pl.BlockSpec((ka, d), lambda i: (0, 0)),
                  pl.BlockSpec((c, d), lambda i: (ka // c, 0))],
        out_specs=pl.BlockSpec((tm, d), lambda i: (i, 0)),
        out_shape=jax.ShapeDtypeStruct((t, d), F32),
        compiler_params=_params("arbitrary"),
        name="out_proj",
    )(attn, proj, proj, proj, proj, proj, conv_w, conv_b, x2, mod3, g_post, w_o_bf, w_o_bf)


def _ffn_kernel(x_ref, sc_ref, sh_ref, gt_ref, gpre_ref, gpost_ref, wa_ref, wg_ref,
                cwa_ref, cwg_ref, cba_ref, cbg_ref, wd_ref, o_ref,
                h_scr, halo_a, halo_g, y0_scr, y1_scr, *, per_b, nf, sub):
    i = pl.program_id(0)
    f = pl.program_id(1)
    tm = x_ref.shape[0]
    tf = wa_ref.shape[1]
    first = (i % per_b) == 0

    def branch(w_ref, cw_ref, cb_ref, halo_ref, cols):
        w = w_ref[:, cols]
        u = jnp.concatenate([jnp.dot(h_scr[r:r + UP_DOT_ROWS, :], w, preferred_element_type=F32)
                             for r in range(0, tm, UP_DOT_ROWS)], axis=0)
        halo = jnp.where(first, 0.0, halo_ref[f, :, cols])
        halo_ref[f, :, cols] = u[tm - 8:, :]
        return _causal_conv3(u, halo, cw_ref.at[:, cols], cb_ref.at[:, cols])

    def up(y_out):
        for s in range(0, tf, sub):
            cols = slice(s, s + sub)
            a = branch(wa_ref, cwa_ref, cba_ref, halo_a, cols)
            g = branch(wg_ref, cwg_ref, cbg_ref, halo_g, cols)
            hg = 0.5 * g
            y_out[:, cols] = ((hg + hg * jnp.tanh(hg)) * a).astype(BF16)

    def down(y_in):
        o_ref[...] += jnp.dot(y_in[...], wd_ref[...], preferred_element_type=F32)

    @pl.when(f == 0)
    def _():
        h = _rms(x_ref[...], gpre_ref[...] * (1.0 + sc_ref[...])) + sh_ref[...]
        h_scr[...] = h.astype(BF16)
        o_ref[...] = jnp.zeros(o_ref.shape, F32)
        up(y0_scr)

    interior = jnp.logical_and(f > 0, f < nf)

    @pl.when(jnp.logical_and(interior, f % 2 == 1))
    def _():
        up(y1_scr)
        down(y0_scr)

    @pl.when(jnp.logical_and(interior, f % 2 == 0))
    def _():
        up(y0_scr)
        down(y1_scr)

    @pl.when(f == nf)
    def _():
        down(y1_scr if (nf - 1) % 2 == 1 else y0_scr)
        o_ref[...] = x_ref[...] + _rms(o_ref[...], gt_ref[...] * gpost_ref[...])


def _conv_ffn(x1, mod3, g_pre, g_post, w_up_bf, conv_w, conv_b, w_down_bf, tm):
    t, d = x1.shape
    tf = FFN_TF
    nf = D_FF // tf
    per_b = SEQ // tm
    modspec = lambda k: pl.BlockSpec((None, 1, d), lambda i, f: ((i // per_b) * N_MOD + k, 0, 0))
    up_f = lambda f: jnp.minimum(f, nf - 1)
    dn_f = lambda f: jnp.maximum(f - 1, 0)
    return pl.pallas_call(
        functools.partial(_ffn_kernel, per_b=per_b, nf=nf, sub=UP_SUB),
        grid=(t // tm, nf + 1),
        in_specs=[pl.BlockSpec((tm, d), lambda i, f: (i, 0)),
                  modspec(4), modspec(3), modspec(5),
                  pl.BlockSpec((1, d), lambda i, f: (0, 0)),
                  pl.BlockSpec((1, d), lambda i, f: (0, 0)),
                  pl.BlockSpec((d, tf), lambda i, f: (0, up_f(f))),
                  pl.BlockSpec((d, tf), lambda i, f: (0, nf + up_f(f))),
                  pl.BlockSpec((3, tf), lambda i, f: (0, up_f(f))),
                  pl.BlockSpec((3, tf), lambda i, f: (0, nf + up_f(f))),
                  pl.BlockSpec((1, tf), lambda i, f: (0, up_f(f))),
                  pl.BlockSpec((1, tf), lambda i, f: (0, nf + up_f(f))),
                  pl.BlockSpec((tf, d), lambda i, f: (dn_f(f), 0))],
        out_specs=pl.BlockSpec((tm, d), lambda i, f: (i, 0)),
        out_shape=jax.ShapeDtypeStruct((t, d), F32),
        scratch_shapes=[pltpu.VMEM((tm, d), BF16),
                        pltpu.VMEM((nf, 8, tf), F32),
                        pltpu.VMEM((nf, 8, tf), F32),
                        pltpu.VMEM((tm, tf), BF16),
                        pltpu.VMEM((tm, tf), BF16)],
        compiler_params=_params("arbitrary", "arbitrary"),
        name="conv_ffn",
    )(x1, mod3, mod3, mod3, g_pre, g_post, w_up_bf, w_up_bf, conv_w, conv_w,
      conv_b, conv_b, w_down_bf)


def _rot_half_cols(w):
    half = w.shape[-1] // 2
    return jnp.concatenate([-w[..., half:], w[..., :half]], axis=-1)


def _regroup_w_in(w):
    d = w.shape[0]
    c0 = Q_LORA_RANK
    c1 = c0 + KV_LORA_RANK
    c2 = c1 + QK_ROPE_DIM
    c3 = c2 + CONV_WIDTH
    c4 = c3 + CONV_WIDTH
    q_lat, kv_lat, k_rope = w[:, :c0], w[:, c0:c1], w[:, c1:c2]
    gate_b, gate_c, conv_in = w[:, c2:c3], w[:, c3:c4], w[:, c4:]
    pad = jnp.zeros((d, OFF_KV_LAT - OFF_K_ROPE - 2 * QK_ROPE_DIM), w.dtype)
    return jnp.concatenate([gate_b, gate_c, conv_in, q_lat, k_rope, _rot_half_cols(k_rope),
                            pad, kv_lat], axis=1)


def _block_cols(w, tn):
    k, n = w.shape
    return w.reshape(k, n // tn, tn).transpose(1, 0, 2)


def _regroup_w_uq(w):
    r = w.shape[0]
    w = w.reshape(r, MLA_HEADS, QK_NOPE_DIM + QK_ROPE_DIM)
    nope, rope = w[..., :QK_NOPE_DIM], w[..., QK_NOPE_DIM:]
    return jnp.concatenate([nope, rope, _rot_half_cols(rope)], axis=-1).reshape(r, MLA_HEADS * HEAD_QK)


def _regroup_w_ukv(w):
    r = w.shape[0]
    w = w.reshape(r, MLA_HEADS, QK_NOPE_DIM + V_HEAD_DIM)
    k = w[..., :QK_NOPE_DIM].reshape(r, MLA_HEADS * QK_NOPE_DIM)
    v = w[..., QK_NOPE_DIM:].reshape(r, MLA_HEADS * V_HEAD_DIM)
    return jnp.concatenate([k, v], axis=1)


def kernel(x, c, positions, w_ada, b_ada, g_pre_mix, g_post_mix, w_in, g_q, w_uq, g_kv, w_ukv,
           conv_w_mix, conv_b_mix, w_o, g_pre_ffn, g_post_ffn, w_up, conv_w_ffn, conv_b_ffn, w_down):
    b, s, d = x.shape
    assert (s, d) == (SEQ, D_MODEL) and w_ada.shape[0] == 1
    t = b * s
    x2 = x.reshape(t, d)
    pos3 = positions.reshape(t // ATT_BLK, 1, ATT_BLK)
    half = jnp.arange(0, QK_ROPE_DIM, 2, dtype=F32) / QK_ROPE_DIM
    invf = (1.0 / (ROPE_THETA ** half)).reshape(QK_ROPE_DIM // 2, 1)

    rows = 8
    c_pad = jnp.zeros((rows, d), F32).at[:b].set(c)
    x_cur = x2
    for l in range(w_ada.shape[0]):
        mod = _adaln_mod(c_pad, w_ada[l], b_ada[l].reshape(1, -1))
        mod3 = mod[:b].reshape(b * N_MOD, 1, d)

        w_in_blk = _block_cols(_regroup_w_in(w_in[l].astype(BF16)), IN_PROJ_TN)
        wq_p = _regroup_w_uq(w_uq[l]).astype(BF16)
        wkv_p = _regroup_w_ukv(w_ukv[l]).astype(BF16)

        proj = _in_proj(x_cur, mod3, g_pre_mix[l].reshape(1, d), w_in_blk, tm=1024)
        qT, k, vT = _latent_qkv(proj, pos3, invf, g_q[l].reshape(1, -1), g_kv[l].reshape(1, -1),
                                wq_p, wkv_p, tm=ATT_BLK)
        attn = _attention(qT, k, vT, blk=ATT_BLK, heads=ATT_HEADS_PER_STEP, q_blocks=ATT_Q_BLOCKS_PER_STEP)
        x_cur = _out_proj(attn, proj, conv_w_mix[l], conv_b_mix[l].reshape(1, -1), x_cur, mod3,
                          g_post_mix[l].reshape(1, d), w_o[l].astype(BF16), tm=512)
        x_cur = _conv_ffn(x_cur, mod3, g_pre_ffn[l].reshape(1, d), g_post_ffn[l].reshape(1, d),
                          w_up[l].astype(BF16), conv_w_ffn[l],
                          conv_b_ffn[l].reshape(1, -1), w_down[l].astype(BF16), tm=1024)
    return x_cur.reshape(b, s, d)
```

```python
import functools
import math

import jax
import jax.numpy as jnp
from jax import lax
from jax.experimental import pallas as pl
from jax.experimental.pallas import tpu as pltpu

D_MODEL = 2048
SEQ = 8192
CONV_WIDTH = 1024
MLA_HEADS = 8
QK_NOPE_DIM = 128
QK_ROPE_DIM = 64
V_HEAD_DIM = 128
Q_LORA_RANK = 768
KV_LORA_RANK = 512
ROPE_THETA = 10000.0
D_FF = 5632
RMS_EPS = 1e-6
N_MOD = 6

LANES = 128
HEAD_QK = 2 * LANES
PROJ_COLS = 4608
VMEM_LIMIT = 56 * 1024 * 1024
ATT_BLK = 512
ATT_HEADS_PER_STEP = 2
ATT_Q_BLOCKS_PER_STEP = 4
VT_ROWS = V_HEAD_DIM + 16
LOG2_E = 1.4426950408889634
IN_PROJ_TN = 1536
UP_DOT_ROWS = 256
UP_SUB = 512
FFN_TF = 512

OFF_GATE_B = 0
OFF_GATE_C = 1024
OFF_CONV_IN = 2048
OFF_Q_LAT = 3072
OFF_K_ROPE = 3840
OFF_KV_LAT = 4096

BF16 = jnp.bfloat16
F32 = jnp.float32


def _params(*sem):
    return pltpu.CompilerParams(dimension_semantics=sem, vmem_limit_bytes=VMEM_LIMIT)


def _rms(x, g):
    return x * lax.rsqrt(jnp.mean(x * x, axis=-1, keepdims=True) + RMS_EPS) * g


def _mod_kernel(c_ref, w_ref, b_ref, o_ref):
    c = c_ref[...]
    c_act = c * (1.0 / (1.0 + jnp.exp(-c)))
    o_ref[...] = jnp.dot(c_act.astype(BF16), w_ref[...].astype(BF16),
                         preferred_element_type=F32) + b_ref[...]


def _adaln_mod(c_pad, w_ada, b_ada):
    rows, d = c_pad.shape
    n = w_ada.shape[1]
    tn = 1536
    return pl.pallas_call(
        _mod_kernel,
        grid=(n // tn,),
        in_specs=[pl.BlockSpec((rows, d), lambda j: (0, 0)),
                  pl.BlockSpec((d, tn), lambda j: (0, j)),
                  pl.BlockSpec((1, tn), lambda j: (0, j))],
        out_specs=pl.BlockSpec((rows, tn), lambda j: (0, j)),
        out_shape=jax.ShapeDtypeStruct((rows, n), F32),
        compiler_params=_params("arbitrary"),
        name="adaln_mod",
    )(c_pad, w_ada, b_ada)


def _inproj_kernel(x_ref, sc_ref, sh_ref, g_ref, w_ref, o_ref, h_scr):
    tm = x_ref.shape[0]

    def project(rows):
        for r in range(0, tm, rows):
            o_ref[r:r + rows, :] = jnp.dot(h_scr[r:r + rows, :], w_ref[...],
                                           preferred_element_type=F32).astype(o_ref.dtype)

    @pl.when(pl.program_id(1) == 0)
    def _():
        rows = tm // 4
        for r in range(0, tm, rows):
            h = _rms(x_ref[r:r + rows, :], g_ref[...] * (1.0 + sc_ref[...])) + sh_ref[...]
            h_scr[r:r + rows, :] = h.astype(BF16)
        project(rows)

    @pl.when(pl.program_id(1) != 0)
    def _():
        project(tm)


def _in_proj(x2, mod3, g_pre, w_blk, tm):
    t, d = x2.shape
    nb, _, tn = w_blk.shape
    n = nb * tn
    per_b = SEQ // tm
    return pl.pallas_call(
        _inproj_kernel,
        grid=(t // tm, n // tn),
        in_specs=[pl.BlockSpec((tm, d), lambda i, j: (i, 0)),
                  pl.BlockSpec((None, 1, d), lambda i, j: ((i // per_b) * N_MOD + 1, 0, 0)),
                  pl.BlockSpec((None, 1, d), lambda i, j: ((i // per_b) * N_MOD + 0, 0, 0)),
                  pl.BlockSpec((1, d), lambda i, j: (0, 0)),
                  pl.BlockSpec((None, d, tn), lambda i, j: (j, 0, 0))],
        out_specs=pl.BlockSpec((tm, tn), lambda i, j: (i, j)),
        out_shape=jax.ShapeDtypeStruct((t, n), BF16),
        scratch_shapes=[pltpu.VMEM((tm, d), BF16)],
        compiler_params=_params("arbitrary", "arbitrary"),
        name="in_proj",
    )(x2, mod3, mod3, g_pre, w_blk)


def _qkv_kernel(ql_ref, kr_ref, kvl_ref, pos_ref, invf_ref, gq_ref, gkv_ref,
                wq_ref, wkv_ref, qT_ref, k_ref, vT_ref):
    tm = ql_ref.shape[0]
    ang = invf_ref[...] * pos_ref[...].astype(F32)
    cos = jnp.cos(ang)
    sin = jnp.sin(ang)
    cos2 = jnp.concatenate([cos, cos], axis=0)
    sin2 = jnp.concatenate([sin, sin], axis=0)

    def rope_t(a):
        at = a.T
        return at[:QK_ROPE_DIM, :] * cos2 + at[QK_ROPE_DIM:, :] * sin2

    qn = _rms(ql_ref[...].astype(F32), gq_ref[...]).astype(BF16)
    q = jnp.dot(qn, wq_ref[...], preferred_element_type=F32)
    scale = LOG2_E / math.sqrt(QK_NOPE_DIM + QK_ROPE_DIM)
    kvn = _rms(kvl_ref[...].astype(F32), gkv_ref[...]).astype(BF16)
    kv = jnp.dot(kvn, wkv_ref[...], preferred_element_type=F32)
    zeros_t = jnp.zeros((LANES - QK_ROPE_DIM, tm), F32)
    k_rope = jnp.concatenate([rope_t(kr_ref[...].astype(F32)), zeros_t], axis=0).T.astype(BF16)
    v_off = MLA_HEADS * QK_NOPE_DIM
    ones = jnp.ones((VT_ROWS - V_HEAD_DIM, tm), BF16)
    for h in range(MLA_HEADS):
        lo = h * HEAD_QK
        qT_ref[lo:lo + LANES, :] = (q[:, lo:lo + LANES] * scale).T.astype(BF16)
        qT_ref[lo + LANES:lo + LANES + QK_ROPE_DIM, :] = (rope_t(q[:, lo + LANES:lo + HEAD_QK]) * scale).astype(BF16)
        qT_ref[lo + LANES + QK_ROPE_DIM:lo + HEAD_QK, :] = zeros_t.astype(BF16)
        k_ref[:, lo:lo + LANES] = kv[:, h * LANES:(h + 1) * LANES].astype(BF16)
        k_ref[:, lo + LANES:lo + HEAD_QK] = k_rope
        vlo = h * VT_ROWS
        vT_ref[vlo:vlo + V_HEAD_DIM, :] = kv[:, v_off + h * LANES:v_off + (h + 1) * LANES].T.astype(BF16)
        vT_ref[vlo + V_HEAD_DIM:vlo + VT_ROWS, :] = ones


def _latent_qkv(proj, pos3, invf, g_q, g_kv, wq_p, wkv_p, tm):
    t = proj.shape[0]
    hq = MLA_HEADS * HEAD_QK
    hv = MLA_HEADS * V_HEAD_DIM
    hvt = MLA_HEADS * VT_ROWS
    const = lambda i: (0, 0)
    return pl.pallas_call(
        _qkv_kernel,
        grid=(t // tm,),
        in_specs=[pl.BlockSpec((tm, Q_LORA_RANK), lambda i: (i, OFF_Q_LAT // Q_LORA_RANK)),
                  pl.BlockSpec((tm, LANES), lambda i: (i, OFF_K_ROPE // LANES)),
                  pl.BlockSpec((tm, KV_LORA_RANK), lambda i: (i, OFF_KV_LAT // KV_LORA_RANK)),
                  pl.BlockSpec((None, 1, tm), lambda i: (i, 0, 0)),
                  pl.BlockSpec((QK_ROPE_DIM // 2, 1), const),
                  pl.BlockSpec((1, Q_LORA_RANK), const),
                  pl.BlockSpec((1, KV_LORA_RANK), const),
                  pl.BlockSpec((Q_LORA_RANK, hq), const),
                  pl.BlockSpec((KV_LORA_RANK, 2 * hv), const)],
        out_specs=[pl.BlockSpec((None, hq, tm), lambda i: (i, 0, 0)),
                   pl.BlockSpec((tm, hq), lambda i: (i, 0)),
                   pl.BlockSpec((None, hvt, tm), lambda i: (i, 0, 0))],
        out_shape=[jax.ShapeDtypeStruct((t // tm, hq, tm), BF16),
                   jax.ShapeDtypeStruct((t, hq), BF16),
                   jax.ShapeDtypeStruct((t // tm, hvt, tm), BF16)],
        compiler_params=_params("arbitrary"),
        name="latent_qkv",
    )(proj, proj, proj, pos3, invf, g_q, g_kv, wq_p, wkv_p)


def _causal_conv3(u, halo, w_ref, b_ref):
    w0, w1, w2, b = w_ref[0:1, :], w_ref[1:2, :], w_ref[2:3, :], b_ref[...]

    def taps(p2, p1, p0):
        return p2 * w0 + p1 * w1 + p0 * w2 + b

    body = taps(pltpu.roll(u, 2, 0), pltpu.roll(u, 1, 0), u)
    head = jnp.concatenate([halo, u[0:8, :]], axis=0)
    first = taps(pltpu.roll(head, 2, 0), pltpu.roll(head, 1, 0), head)[8:16, :]
    return jnp.concatenate([first, body[8:, :]], axis=0)


def _attn_kernel(qT_ref, k_ref, vT_ref, o_ref, *scratch, blk, heads, q_blocks):
    per_head = 4
    state = [scratch[per_head * hh:per_head * (hh + 1)] for hh in range(heads)]

    def q_rows(hh):
        return slice(hh * HEAD_QK, (hh + 1) * HEAD_QK)

    def one_q_block(qb):
        qi = pl.program_id(2) * q_blocks + qb

        def scores(hh, j):
            start = pl.multiple_of(j * blk, blk)
            return jnp.dot(k_ref[pl.ds(start, blk), q_rows(hh)], qT_ref[qb, q_rows(hh), :],
                           preferred_element_type=F32)

        def consume(hh, sT, j):
            _, _, m_scr, acc_scr = state[hh]
            m_old = m_scr[...]
            m_new = jnp.maximum(m_old, jnp.max(sT, axis=0, keepdims=True))
            alpha = jnp.exp2(m_old - m_new)
            pT = jnp.exp2(sT - m_new).astype(BF16)
            vT = vT_ref[j, hh * VT_ROWS:(hh + 1) * VT_ROWS, :]
            acc_scr[...] = alpha * acc_scr[...] + jnp.dot(vT, pT, preferred_element_type=F32)
            m_scr[...] = m_new

        def consume_diag(hh, s_ref):
            sT = s_ref[...]
            r = lax.broadcasted_iota(jnp.int32, sT.shape, 0)
            c = lax.broadcasted_iota(jnp.int32, sT.shape, 1)
            consume(hh, jnp.where(r <= c, sT, -1e30), qi)

        def pair(j):
            for hh in range(heads):
                sa, sb, _, _ = state[hh]
                sb[...] = scores(hh, j + 1)
                consume(hh, sa[...], j)
            for hh in range(heads):
                sa, sb, _, _ = state[hh]
                sa[...] = scores(hh, j + 2)
                consume(hh, sb[...], j + 1)

        for hh in range(heads):
            sa, _, m_scr, acc_scr = state[hh]
            m_scr[...] = jnp.full(m_scr.shape, -1e30, F32)
            acc_scr[...] = jnp.zeros(acc_scr.shape, F32)
            sa[...] = scores(hh, 0)

        def body(t, carry):
            pair(4 * t)
            pair(4 * t + 2)
            return carry

        lax.fori_loop(0, qi // 4, body, 0)
        rem = qi % 4

        @pl.when(rem >= 2)
        def _():
            pair(qi - rem)

        @pl.when(rem % 2 == 0)
        def _():
            for hh in range(heads):
                consume_diag(hh, state[hh][0])

        @pl.when(rem % 2 == 1)
        def _():
            for hh in range(heads):
                sa, sb, _, _ = state[hh]
                sb[...] = scores(hh, qi)
                consume(hh, sa[...], qi - 1)
            for hh in range(heads):
                consume_diag(hh, state[hh][1])

        for hh in range(heads):
            acc = state[hh][3][...]
            o_ref[qb * blk:(qb + 1) * blk, hh * V_HEAD_DIM:(hh + 1) * V_HEAD_DIM] = (
                acc[:V_HEAD_DIM] / acc[V_HEAD_DIM:V_HEAD_DIM + 1]).T.astype(o_ref.dtype)

    for qb in range(q_blocks):
        one_q_block(qb)


def _attention(qT, k, vT, blk, heads, q_blocks):
    t = k.shape[0]
    b = t // SEQ
    nq = SEQ // blk
    nqs = nq // q_blocks
    per_head_scratch = [pltpu.VMEM((blk, blk), F32),
                        pltpu.VMEM((blk, blk), F32),
                        pltpu.VMEM((1, blk), F32),
                        pltpu.VMEM((VT_ROWS, blk), F32)]
    return pl.pallas_call(
        functools.partial(_attn_kernel, blk=blk, heads=heads, q_blocks=q_blocks),
        grid=(b, MLA_HEADS // heads, nqs),
        in_specs=[pl.BlockSpec((q_blocks, heads * HEAD_QK, blk), lambda bi, g, i: (bi * nqs + i, g, 0)),
                  pl.BlockSpec((SEQ, heads * HEAD_QK), lambda bi, g, i: (bi, g)),
                  pl.BlockSpec((nq, heads * VT_ROWS, blk), lambda bi, g, i: (bi, g, 0))],
        out_specs=pl.BlockSpec((q_blocks * blk, heads * V_HEAD_DIM), lambda bi, g, i: (bi * nqs + i, g)),
        out_shape=jax.ShapeDtypeStruct((t, MLA_HEADS * V_HEAD_DIM), BF16),
        scratch_shapes=per_head_scratch * heads,
        compiler_params=_params("arbitrary", "arbitrary", "arbitrary"),
        name="attention",
    )(qT, k, vT)


def _outproj_kernel(a_ref, gb_ref, gc_ref, ci_ref, gch_ref, cih_ref, cw_ref, cb_ref,
                    x_ref, gt_ref, g_ref, wa_ref, wc_ref, o_ref, *, per_b):
    first = (pl.program_id(0) % per_b) == 0
    g = gc_ref[...].astype(F32) * ci_ref[...].astype(F32)
    gh = gch_ref[...].astype(F32) * cih_ref[...].astype(F32)
    halo = jnp.where(first, 0.0, gh[8:16, :])
    conv = (gb_ref[...].astype(F32) * _causal_conv3(g, halo, cw_ref, cb_ref)).astype(BF16)
    mix = (jnp.dot(a_ref[...], wa_ref[...], preferred_element_type=F32)
           + jnp.dot(conv, wc_ref[...], preferred_element_type=F32))
    o_ref[...] = x_ref[...] + _rms(mix, gt_ref[...] * g_ref[...])


def _out_proj(attn, proj, conv_w, conv_b, x2, mod3, g_post, w_o_bf, tm):
    t, d = x2.shape
    ka = attn.shape[1]
    c = CONV_WIDTH
    hb = 16
    per_b = SEQ // tm
    halo_idx = lambda col: (lambda i: (jnp.maximum(i * (tm // hb) - 1, 0), col))
    return pl.pallas_call(
        functools.partial(_outproj_kernel, per_b=per_b),
        grid=(t // tm,),
        in_specs=[pl.BlockSpec((tm, ka), lambda i: (i, 0)),
                  pl.BlockSpec((tm, c), lambda i: (i, OFF_GATE_B // c)),
                  pl.BlockSpec((tm, c), lambda i: (i, OFF_GATE_C // c)),
                  pl.BlockSpec((tm, c), lambda i: (i, OFF_CONV_IN // c)),
                  pl.BlockSpec((hb, c), halo_idx(OFF_GATE_C // c)),
                  pl.BlockSpec((hb, c), halo_idx(OFF_CONV_IN // c)),
                  pl.BlockSpec((3, c), lambda i: (0, 0)),
                  pl.BlockSpec((1, c), lambda i: (0, 0)),
                  pl.BlockSpec((tm, d), lambda i: (i, 0)),
                  pl.BlockSpec((None, 1, d), lambda i: ((i // per_b) * N_MOD + 2, 0, 0)),
                  pl.BlockSpec((1, d), lambda i: (0, 0)),
                  pl.BlockSpec((ka, d), lambda i: (0, 0)),
                  pl.BlockSpec((c, d), lambda i: (ka // c, 0))],
        out_specs=pl.BlockSpec((tm, d), lambda i: (i, 0)),
        out_shape=jax.ShapeDtypeStruct((t, d), F32),
        compiler_params=_params("arbitrary"),
        name="out_proj",
    )(attn, proj, proj, proj, proj, proj, conv_w, conv_b, x2, mod3, g_post, w_o_bf, w_o_bf)


def _ffn_kernel(x_ref, sc_ref, sh_ref, gt_ref, gpre_ref, gpost_ref, wa_ref, wg_ref,
                cwa_ref, cwg_ref, cba_ref, cbg_ref, wd_ref, o_ref,
                h_scr, halo_a, halo_g, y0_scr, y1_scr, *, per_b, nf, sub):
    i = pl.program_id(0)
    f = pl.program_id(1)
    tm = x_ref.shape[0]
    tf = wa_ref.shape[1]
    first = (i % per_b) == 0

    def branch(w_ref, cw_ref, cb_ref, halo_ref, cols):
        w = w_ref[:, cols]
        u = jnp.concatenate([jnp.dot(h_scr[r:r + UP_DOT_ROWS, :], w, preferred_element_type=F32)
                             for r in range(0, tm, UP_DOT_ROWS)], axis=0)
        halo = jnp.where(first, 0.0, halo_ref[f, :, cols])
        halo_ref[f, :, cols] = u[tm - 8:, :]
        return _causal_conv3(u, halo, cw_ref.at[:, cols], cb_ref.at[:, cols])

    def up(y_out):
        for s in range(0, tf, sub):
            cols = slice(s, s + sub)
            a = branch(wa_ref, cwa_ref, cba_ref, halo_a, cols)
            g = branch(wg_ref, cwg_ref, cbg_ref, halo_g, cols)
            hg = 0.5 * g
            y_out[:, cols] = ((hg + hg * jnp.tanh(hg)) * a).astype(BF16)

    def down(y_in):
        o_ref[...] += jnp.dot(y_in[...], wd_ref[...], preferred_element_type=F32)

    @pl.when(f == 0)
    def _():
        h = _rms(x_ref[...], gpre_ref[...] * (1.0 + sc_ref[...])) + sh_ref[...]
        h_scr[...] = h.astype(BF16)
        o_ref[...] = jnp.zeros(o_ref.shape, F32)
        up(y0_scr)

    interior = jnp.logical_and(f > 0, f < nf)

    @pl.when(jnp.logical_and(interior, f % 2 == 1))
    def _():
        up(y1_scr)
        down(y0_scr)

    @pl.when(jnp.logical_and(interior, f % 2 == 0))
    def _():
        up(y0_scr)
        down(y1_scr)

    @pl.when(f == nf)
    def _():
        down(y1_scr if (nf - 1) % 2 == 1 else y0_scr)
        o_ref[...] = x_ref[...] + _rms(o_ref[...], gt_ref[...] * gpost_ref[...])


def _conv_ffn(x1, mod3, g_pre, g_post, w_up_bf, conv_w, conv_b, w_down_bf, tm):
    t, d = x1.shape
    tf = FFN_TF
    nf = D_FF // tf
    per_b = SEQ // tm
    modspec = lambda k: pl.BlockSpec((None, 1, d), lambda i, f: ((i // per_b) * N_MOD + k, 0, 0))
    up_f = lambda f: jnp.minimum(f, nf - 1)
    dn_f = lambda f: jnp.maximum(f - 1, 0)
    return pl.pallas_call(
        functools.partial(_ffn_kernel, per_b=per_b, nf=nf, sub=UP_SUB),
        grid=(t // tm, nf + 1),
        in_specs=[pl.BlockSpec((tm, d), lambda i, f: (i, 0)),
                  modspec(4), modspec(3), modspec(5),
                  pl.BlockSpec((1, d), lambda i, f: (0, 0)),
                  pl.BlockSpec((1, d), lambda i, f: (0, 0)),
                  pl.BlockSpec((d, tf), lambda i, f: (0, up_f(f))),
                  pl.BlockSpec((d, tf), lambda i, f: (0, nf + up_f(f))),
                  pl.BlockSpec((3, tf), lambda i, f: (0, up_f(f))),
                  pl.BlockSpec((3, tf), lambda i, f: (0, nf + up_f(f))),
                  pl.BlockSpec((1, tf), lambda i, f: (0, up_f(f))),
                  pl.BlockSpec((1, tf), lambda i, f: (0, nf + up_f(f))),
                  pl.BlockSpec((tf, d), lambda i, f: (dn_f(f), 0))],
        out_specs=pl.BlockSpec((tm, d), lambda i, f: (i, 0)),
        out_shape=jax.ShapeDtypeStruct((t, d), F32),
        scratch_shapes=[pltpu.VMEM((tm, d), BF16),
                        pltpu.VMEM((nf, 8, tf), F32),
                        pltpu.VMEM((nf, 8, tf), F32),
                        pltpu.VMEM((tm, tf), BF16),
                        pltpu.VMEM((tm, tf), BF16)],
        compiler_params=_params("arbitrary", "arbitrary"),
        name="conv_ffn",
    )(x1, mod3, mod3, mod3, g_pre, g_post, w_up_bf, w_up_bf, conv_w, conv_w,
      conv_b, conv_b, w_down_bf)


def _rot_half_cols(w):
    half = w.shape[-1] // 2
    return jnp.concatenate([-w[..., half:], w[..., :half]], axis=-1)


def _regroup_w_in(w):
    d = w.shape[0]
    c0 = Q_LORA_RANK
    c1 = c0 + KV_LORA_RANK
    c2 = c1 + QK_ROPE_DIM
    c3 = c2 + CONV_WIDTH
    c4 = c3 + CONV_WIDTH
    q_lat, kv_lat, k_rope = w[:, :c0], w[:, c0:c1], w[:, c1:c2]
    gate_b, gate_c, conv_in = w[:, c2:c3], w[:, c3:c4], w[:, c4:]
    pad = jnp.zeros((d, OFF_KV_LAT - OFF_K_ROPE - 2 * QK_ROPE_DIM), w.dtype)
    return jnp.concatenate([gate_b, gate_c, conv_in, q_lat, k_rope, _rot_half_cols(k_rope),
                            pad, kv_lat], axis=1)


def _block_cols(w, tn):
    k, n = w.shape
    return w.reshape(k, n // tn, tn).transpose(1, 0, 2)


def _regroup_w_uq(w):
    r = w.shape[0]
    w = w.reshape(r, MLA_HEADS, QK_NOPE_DIM + QK_ROPE_DIM)
    nope, rope = w[..., :QK_NOPE_DIM], w[..., QK_NOPE_DIM:]
    return jnp.concatenate([nope, rope, _rot_half_cols(rope)], axis=-1).reshape(r, MLA_HEADS * HEAD_QK)


def _regroup_w_ukv(w):
    r = w.shape[0]
    w = w.reshape(r, MLA_HEADS, QK_NOPE_DIM + V_HEAD_DIM)
    k = w[..., :QK_NOPE_DIM].reshape(r, MLA_HEADS * QK_NOPE_DIM)
    v = w[..., QK_NOPE_DIM:].reshape(r, MLA_HEADS * V_HEAD_DIM)
    return jnp.concatenate([k, v], axis=1)


def kernel(x, c, positions, w_ada, b_ada, g_pre_mix, g_post_mix, w_in, g_q, w_uq, g_kv, w_ukv,
           conv_w_mix, conv_b_mix, w_o, g_pre_ffn, g_post_ffn, w_up, conv_w_ffn, conv_b_ffn, w_down):
    b, s, d = x.shape
    assert (s, d) == (SEQ, D_MODEL) and w_ada.shape[0] == 1
    t = b * s
    x2 = x.reshape(t, d)
    pos3 = positions.reshape(t // ATT_BLK, 1, ATT_BLK)
    half = jnp.arange(0, QK_ROPE_DIM, 2, dtype=F32) / QK_ROPE_DIM
    invf = (1.0 / (ROPE_THETA ** half)).reshape(QK_ROPE_DIM // 2, 1)

    rows = 8
    c_pad = jnp.zeros((rows, d), F32).at[:b].set(c)
    x_cur = x2
    for l in range(w_ada.shape[0]):
        mod = _adaln_mod(c_pad, w_ada[l], b_ada[l].reshape(1, -1))
        mod3 = mod[:b].reshape(b * N_MOD, 1, d)

        w_in_blk = _block_cols(_regroup_w_in(w_in[l].astype(BF16)), IN_PROJ_TN)
        wq_p = _regroup_w_uq(w_uq[l]).astype(BF16)
        wkv_p = _regroup_w_ukv(w_ukv[l]).astype(BF16)

        proj = _in_proj(x_cur, mod3, g_pre_mix[l].reshape(1, d), w_in_blk, tm=1024)
        qT, k, vT = _latent_qkv(proj, pos3, invf, g_q[l].reshape(1, -1), g_kv[l].reshape(1, -1),
                                wq_p, wkv_p, tm=ATT_BLK)
        attn = _attention(qT, k, vT, blk=ATT_BLK, heads=ATT_HEADS_PER_STEP, q_blocks=ATT_Q_BLOCKS_PER_STEP)
        x_cur = _out_proj(attn, proj, conv_w_mix[l], conv_b_mix[l].reshape(1, -1), x_cur, mod3,
                          g_post_mix[l].reshape(1, d), w_o[l].astype(BF16), tm=512)
        x_cur = _conv_ffn(x_cur, mod3, g_pre_ffn[l].reshape(1, d), g_post_ffn[l].reshape(1, d),
                          w_up[l].astype(BF16), conv_w_ffn[l],
                          conv_b_ffn[l].reshape(1, -1), w_down[l].astype(BF16), tm=1024)
    return x_cur.reshape(b, s, d)
```

```python
import functools
import math

import jax
import jax.numpy as jnp
from jax import lax
from jax.experimental import pallas as pl
from jax.experimental.pallas import tpu as pltpu

D_MODEL = 2048
SEQ = 8192
CONV_WIDTH = 1024
MLA_HEADS = 8
QK_NOPE_DIM = 128
QK_ROPE_DIM = 64
V_HEAD_DIM = 128
Q_LORA_RANK = 768
KV_LORA_RANK = 512
ROPE_THETA = 10000.0
D_FF = 5632
RMS_EPS = 1e-6
N_MOD = 6

LANES = 128
HEAD_QK = 2 * LANES
PROJ_COLS = 4608
VMEM_LIMIT = 56 * 1024 * 1024
ATT_BLK = 512
ATT_HEADS_PER_STEP = 2
ATT_Q_BLOCKS_PER_STEP = 4
VT_ROWS = V_HEAD_DIM + 16
LOG2_E = 1.4426950408889634
IN_PROJ_TN = 1536
UP_DOT_ROWS = 512
UP_SUB = 512
FFN_TF = 512

OFF_GATE_B = 0
OFF_GATE_C = 1024
OFF_CONV_IN = 2048
OFF_Q_LAT = 3072
OFF_K_ROPE = 3840
OFF_KV_LAT = 4096

BF16 = jnp.bfloat16
F32 = jnp.float32


def _params(*sem):
    return pltpu.CompilerParams(dimension_semantics=sem, vmem_limit_bytes=VMEM_LIMIT)


def _rms(x, g):
    return x * lax.rsqrt(jnp.mean(x * x, axis=-1, keepdims=True) + RMS_EPS) * g


def _mod_kernel(c_ref, w_ref, b_ref, o_ref):
    c = c_ref[...]
    c_act = c * (1.0 / (1.0 + jnp.exp(-c)))
    o_ref[...] = jnp.dot(c_act.astype(BF16), w_ref[...].astype(BF16),
                         preferred_element_type=F32) + b_ref[...]


def _adaln_mod(c_pad, w_ada, b_ada):
    rows, d = c_pad.shape
    n = w_ada.shape[1]
    tn = 1536
    return pl.pallas_call(
        _mod_kernel,
        grid=(n // tn,),
        in_specs=[pl.BlockSpec((rows, d), lambda j: (0, 0)),
                  pl.BlockSpec((d, tn), lambda j: (0, j)),
                  pl.BlockSpec((1, tn), lambda j: (0, j))],
        out_specs=pl.BlockSpec((rows, tn), lambda j: (0, j)),
        out_shape=jax.ShapeDtypeStruct((rows, n), F32),
        compiler_params=_params("arbitrary"),
        name="adaln_mod",
    )(c_pad, w_ada, b_ada)


def _inproj_kernel(x_ref, sc_ref, sh_ref, g_ref, w_ref, o_ref, h_scr):
    tm = x_ref.shape[0]

    def project(rows):
        for r in range(0, tm, rows):
            o_ref[r:r + rows, :] = jnp.dot(h_scr[r:r + rows, :], w_ref[...],
                                           preferred_element_type=F32).astype(o_ref.dtype)

    @pl.when(pl.program_id(1) == 0)
    def _():
        rows = tm // 4
        for r in range(0, tm, rows):
            h = _rms(x_ref[r:r + rows, :], g_ref[...] * (1.0 + sc_ref[...])) + sh_ref[...]
            h_scr[r:r + rows, :] = h.astype(BF16)
        project(rows)

    @pl.when(pl.program_id(1) != 0)
    def _():
        project(tm)


def _in_proj(x2, mod3, g_pre, w_blk, tm):
    t, d = x2.shape
    nb, _, tn = w_blk.shape
    n = nb * tn
    per_b = SEQ // tm
    return pl.pallas_call(
        _inproj_kernel,
        grid=(t // tm, n // tn),
        in_specs=[pl.BlockSpec((tm, d), lambda i, j: (i, 0)),
                  pl.BlockSpec((None, 1, d), lambda i, j: ((i // per_b) * N_MOD + 1, 0, 0)),
                  pl.BlockSpec((None, 1, d), lambda i, j: ((i // per_b) * N_MOD + 0, 0, 0)),
                  pl.BlockSpec((1, d), lambda i, j: (0, 0)),
                  pl.BlockSpec((None, d, tn), lambda i, j: (j, 0, 0))],
        out_specs=pl.BlockSpec((tm, tn), lambda i, j: (i, j)),
        out_shape=jax.ShapeDtypeStruct((t, n), BF16),
        scratch_shapes=[pltpu.VMEM((tm, d), BF16)],
        compiler_params=_params("arbitrary", "arbitrary"),
        name="in_proj",
    )(x2, mod3, mod3, g_pre, w_blk)


def _qkv_kernel(ql_ref, kr_ref, kvl_ref, pos_ref, invf_ref, gq_ref, gkv_ref,
                wq_ref, wkv_ref, qT_ref, k_ref, vT_ref):
    tm = ql_ref.shape[0]
    ang = invf_ref[...] * pos_ref[...].astype(F32)
    cos = jnp.cos(ang)
    sin = jnp.sin(ang)
    cos2 = jnp.concatenate([cos, cos], axis=0)
    sin2 = jnp.concatenate([sin, sin], axis=0)

    def rope_t(a):
        at = a.T
        return at[:QK_ROPE_DIM, :] * cos2 + at[QK_ROPE_DIM:, :] * sin2

    qn = _rms(ql_ref[...].astype(F32), gq_ref[...]).astype(BF16)
    q = jnp.dot(qn, wq_ref[...], preferred_element_type=F32)
    scale = LOG2_E / math.sqrt(QK_NOPE_DIM + QK_ROPE_DIM)
    kvn = _rms(kvl_ref[...].astype(F32), gkv_ref[...]).astype(BF16)
    kv = jnp.dot(kvn, wkv_ref[...], preferred_element_type=F32)
    zeros_t = jnp.zeros((LANES - QK_ROPE_DIM, tm), F32)
    k_rope = jnp.concatenate([rope_t(kr_ref[...].astype(F32)), zeros_t], axis=0).T.astype(BF16)
    v_off = MLA_HEADS * QK_NOPE_DIM
    ones = jnp.ones((VT_ROWS - V_HEAD_DIM, tm), BF16)
    for h in range(MLA_HEADS):
        lo = h * HEAD_QK
        qT_ref[lo:lo + LANES, :] = (q[:, lo:lo + LANES] * scale).T.astype(BF16)
        qT_ref[lo + LANES:lo + LANES + QK_ROPE_DIM, :] = (rope_t(q[:, lo + LANES:lo + HEAD_QK]) * scale).astype(BF16)
        qT_ref[lo + LANES + QK_ROPE_DIM:lo + HEAD_QK, :] = zeros_t.astype(BF16)
        k_ref[:, lo:lo + LANES] = kv[:, h * LANES:(h + 1) * LANES].astype(BF16)
        k_ref[:, lo + LANES:lo + HEAD_QK] = k_rope
        vlo = h * VT_ROWS
        vT_ref[vlo:vlo + V_HEAD_DIM, :] = kv[:, v_off + h * LANES:v_off + (h + 1) * LANES].T.astype(BF16)
        vT_ref[vlo + V_HEAD_DIM:vlo + VT_ROWS, :] = ones


def _latent_qkv(proj, pos3, invf, g_q, g_kv, wq_p, wkv_p, tm):
    t = proj.shape[0]
    hq = MLA_HEADS * HEAD_QK
    hv = MLA_HEADS * V_HEAD_DIM
    hvt = MLA_HEADS * VT_ROWS
    const = lambda i: (0, 0)
    return pl.pallas_call(
        _qkv_kernel,
        grid=(t // tm,),
        in_specs=[pl.BlockSpec((tm, Q_LORA_RANK), lambda i: (i, OFF_Q_LAT // Q_LORA_RANK)),
                  pl.BlockSpec((tm, LANES), lambda i: (i, OFF_K_ROPE // LANES)),
                  pl.BlockSpec((tm, KV_LORA_RANK), lambda i: (i, OFF_KV_LAT // KV_LORA_RANK)),
                  pl.BlockSpec((None, 1, tm), lambda i: (i, 0, 0)),
                  pl.BlockSpec((QK_ROPE_DIM // 2, 1), const),
                  pl.BlockSpec((1, Q_LORA_RANK), const),
                  pl.BlockSpec((1, KV_LORA_RANK), const),
                  pl.BlockSpec((Q_LORA_RANK, hq), const),
                  pl.BlockSpec((KV_LORA_RANK, 2 * hv), const)],
        out_specs=[pl.BlockSpec((None, hq, tm), lambda i: (i, 0, 0)),
                   pl.BlockSpec((tm, hq), lambda i: (i, 0)),
                   pl.BlockSpec((None, hvt, tm), lambda i: (i, 0, 0))],
        out_shape=[jax.ShapeDtypeStruct((t // tm, hq, tm), BF16),
                   jax.ShapeDtypeStruct((t, hq), BF16),
                   jax.ShapeDtypeStruct((t // tm, hvt, tm), BF16)],
        compiler_params=_params("arbitrary"),
        name="latent_qkv",
    )(proj, proj, proj, pos3, invf, g_q, g_kv, wq_p, wkv_p)


def _causal_conv3(u, halo, w_ref, b_ref):
    w0, w1, w2, b = w_ref[0:1, :], w_ref[1:2, :], w_ref[2:3, :], b_ref[...]

    def taps(p2, p1, p0):
        return p2 * w0 + p1 * w1 + p0 * w2 + b

    body = taps(pltpu.roll(u, 2, 0), pltpu.roll(u, 1, 0), u)
    head = jnp.concatenate([halo, u[0:8, :]], axis=0)
    first = taps(pltpu.roll(head, 2, 0), pltpu.roll(head, 1, 0), head)[8:16, :]
    return jnp.concatenate([first, body[8:, :]], axis=0)


def _attn_kernel(qT_ref, k_ref, vT_ref, o_ref, *scratch, blk, heads, q_blocks):
    per_head = 4
    state = [scratch[per_head * hh:per_head * (hh + 1)] for hh in range(heads)]

    def q_rows(hh):
        return slice(hh * HEAD_QK, (hh + 1) * HEAD_QK)

    def one_q_block(qb):
        qi = pl.program_id(2) * q_blocks + qb

        def scores(hh, j):
            start = pl.multiple_of(j * blk, blk)
            return jnp.dot(k_ref[pl.ds(start, blk), q_rows(hh)], qT_ref[qb, q_rows(hh), :],
                           preferred_element_type=F32)

        def consume(hh, sT, j):
            _, _, m_scr, acc_scr = state[hh]
            m_old = m_scr[...]
            m_new = jnp.maximum(m_old, jnp.max(sT, axis=0, keepdims=True))
            alpha = jnp.exp2(m_old - m_new)
            pT = jnp.exp2(sT - m_new).astype(BF16)
            vT = vT_ref[j, hh * VT_ROWS:(hh + 1) * VT_ROWS, :]
            acc_scr[...] = alpha * acc_scr[...] + jnp.dot(vT, pT, preferred_element_type=F32)
            m_scr[...] = m_new

        def consume_diag(hh, s_ref):
            sT = s_ref[...]
            r = lax.broadcasted_iota(jnp.int32, sT.shape, 0)
            c = lax.broadcasted_iota(jnp.int32, sT.shape, 1)
            consume(hh, jnp.where(r <= c, sT, -1e30), qi)

        def pair(j):
            for hh in range(heads):
                sa, sb, _, _ = state[hh]
                sb[...] = scores(hh, j + 1)
                consume(hh, sa[...], j)
            for hh in range(heads):
                sa, sb, _, _ = state[hh]
                sa[...] = scores(hh, j + 2)
                consume(hh, sb[...], j + 1)

        for hh in range(heads):
            sa, _, m_scr, acc_scr = state[hh]
            m_scr[...] = jnp.full(m_scr.shape, -1e30, F32)
            acc_scr[...] = jnp.zeros(acc_scr.shape, F32)
            sa[...] = scores(hh, 0)

        def body(t, carry):
            pair(4 * t)
            pair(4 * t + 2)
            return carry

        lax.fori_loop(0, qi // 4, body, 0)
        rem = qi % 4

        @pl.when(rem >= 2)
        def _():
            pair(qi - rem)

        @pl.when(rem % 2 == 0)
        def _():
            for hh in range(heads):
                consume_diag(hh, state[hh][0])

        @pl.when(rem % 2 == 1)
        def _():
            for hh in range(heads):
                sa, sb, _, _ = state[hh]
                sb[...] = scores(hh, qi)
                consume(hh, sa[...], qi - 1)
            for hh in range(heads):
                consume_diag(hh, state[hh][1])

        for hh in range(heads):
            acc = state[hh][3][...]
            o_ref[qb * blk:(qb + 1) * blk, hh * V_HEAD_DIM:(hh + 1) * V_HEAD_DIM] = (
                acc[:V_HEAD_DIM] / acc[V_HEAD_DIM:V_HEAD_DIM + 1]).T.astype(o_ref.dtype)

    for qb in range(q_blocks):
        one_q_block(qb)


def _attention(qT, k, vT, blk, heads, q_blocks):
    t = k.shape[0]
    b = t // SEQ
    nq = SEQ // blk
    nqs = nq // q_blocks
    per_head_scratch = [pltpu.VMEM((blk, blk), F32),
                        pltpu.VMEM((blk, blk), F32),
                        pltpu.VMEM((1, blk), F32),
                        pltpu.VMEM((VT_ROWS, blk), F32)]
    return pl.pallas_call(
        functools.partial(_attn_kernel, blk=blk, heads=heads, q_blocks=q_blocks),
        grid=(b, MLA_HEADS // heads, nqs),
        in_specs=[pl.BlockSpec((q_blocks, heads * HEAD_QK, blk), lambda bi, g, i: (bi * nqs + i, g, 0)),
                  pl.BlockSpec((SEQ, heads * HEAD_QK), lambda bi, g, i: (bi, g)),
                  pl.BlockSpec((nq, heads * VT_ROWS, blk), lambda bi, g, i: (bi, g, 0))],
        out_specs=pl.BlockSpec((q_blocks * blk, heads * V_HEAD_DIM), lambda bi, g, i: (bi * nqs + i, g)),
        out_shape=jax.ShapeDtypeStruct((t, MLA_HEADS * V_HEAD_DIM), BF16),
        scratch_shapes=per_head_scratch * heads,
        compiler_params=_params("arbitrary", "arbitrary", "arbitrary"),
        name="attention",
    )(qT, k, vT)


def _outproj_kernel(a_ref, gb_ref, gc_ref, ci_ref, gch_ref, cih_ref, cw_ref, cb_ref,
                    x_ref, gt_ref, g_ref, wa_ref, wc_ref, o_ref, *, per_b):
    first = (pl.program_id(0) % per_b) == 0
    g = gc_ref[...].astype(F32) * ci_ref[...].astype(F32)
    gh = gch_ref[...].astype(F32) * cih_ref[...].astype(F32)
    halo = jnp.where(first, 0.0, gh[8:16, :])
    conv = (gb_ref[...].astype(F32) * _causal_conv3(g, halo, cw_ref, cb_ref)).astype(BF16)
    mix = (jnp.dot(a_ref[...], wa_ref[...], preferred_element_type=F32)
           + jnp.dot(conv, wc_ref[...], preferred_element_type=F32))
    o_ref[...] = x_ref[...] + _rms(mix, gt_ref[...] * g_ref[...])


def _out_proj(attn, proj, conv_w, conv_b, x2, mod3, g_post, w_o_bf, tm):
    t, d = x2.shape
    ka = attn.shape[1]
    c = CONV_WIDTH
    hb = 16
    per_b = SEQ // tm
    halo_idx = lambda col: (lambda i: (jnp.maximum(i * (tm // hb) - 1, 0), col))
    return pl.pallas_call(
        functools.partial(_outproj_kernel, per_b=per_b),
        grid=(t // tm,),
        in_specs=[pl.BlockSpec((tm, ka), lambda i: (i, 0)),
                  pl.BlockSpec((tm, c), lambda i: (i, OFF_GATE_B // c)),
                  pl.BlockSpec((tm, c), lambda i: (i, OFF_GATE_C // c)),
                  pl.BlockSpec((tm, c), lambda i: (i, OFF_CONV_IN // c)),
                  pl.BlockSpec((hb, c), halo_idx(OFF_GATE_C // c)),
                  pl.BlockSpec((hb, c), halo_idx(OFF_CONV_IN // c)),
                  pl.BlockSpec((3, c), lambda i: (0, 0)),
                  pl.BlockSpec((1, c), lambda i: (0, 0)),
                  pl.BlockSpec((tm, d), lambda i: (i, 0)),
                  pl.BlockSpec((None, 1, d), lambda i: ((i // per_b) * N_MOD + 2, 0, 0)),
                  pl.BlockSpec((1, d), lambda i: (0, 0)),
                  pl.BlockSpec((ka, d), lambda i: (0, 0)),
                  pl.BlockSpec((c, d), lambda i: (ka // c, 0))],
        out_specs=pl.BlockSpec((tm, d), lambda i: (i, 0)),
        out_shape=jax.ShapeDtypeStruct((t, d), F32),
        compiler_params=_params("arbitrary"),
        name="out_proj",
    )(attn, proj, proj, proj, proj, proj, conv_w, conv_b, x2, mod3, g_post, w_o_bf, w_o_bf)


def _ffn_kernel(x_ref, sc_ref, sh_ref, gt_ref, gpre_ref, gpost_ref, wa_ref, wg_ref,
                cwa_ref, cwg_ref, cba_ref, cbg_ref, wd_ref, o_ref,
                h_scr, halo_a, halo_g, y0_scr, y1_scr, *, per_b, nf, sub):
    i = pl.program_id(0)
    f = pl.program_id(1)
    tm = x_ref.shape[0]
    tf = wa_ref.shape[1]
    first = (i % per_b) == 0

    def branch(w_ref, cw_ref, cb_ref, halo_ref, cols):
        w = w_ref[:, cols]
        u = jnp.concatenate([jnp.dot(h_scr[r:r + UP_DOT_ROWS, :], w, preferred_element_type=F32)
                             for r in range(0, tm, UP_DOT_ROWS)], axis=0)
        halo = jnp.where(first, 0.0, halo_ref[f, :, cols])
        halo_ref[f, :, cols] = u[tm - 8:, :]
        return _causal_conv3(u, halo, cw_ref.at[:, cols], cb_ref.at[:, cols])

    def up(y_out):
        for s in range(0, tf, sub):
            cols = slice(s, s + sub)
            a = branch(wa_ref, cwa_ref, cba_ref, halo_a, cols)
            g = branch(wg_ref, cwg_ref, cbg_ref, halo_g, cols)
            hg = 0.5 * g
            y_out[:, cols] = ((hg + hg * jnp.tanh(hg)) * a).astype(BF16)

    def down(y_in):
        o_ref[...] += jnp.dot(y_in[...], wd_ref[...], preferred_element_type=F32)

    @pl.when(f == 0)
    def _():
        h = _rms(x_ref[...], gpre_ref[...] * (1.0 + sc_ref[...])) + sh_ref[...]
        h_scr[...] = h.astype(BF16)
        o_ref[...] = jnp.zeros(o_ref.shape, F32)
        up(y0_scr)

    interior = jnp.logical_and(f > 0, f < nf)

    @pl.when(jnp.logical_and(interior, f % 2 == 1))
    def _():
        up(y1_scr)
        down(y0_scr)

    @pl.when(jnp.logical_and(interior, f % 2 == 0))
    def _():
        up(y0_scr)
        down(y1_scr)

    @pl.when(f == nf)
    def _():
        down(y1_scr if (nf - 1) % 2 == 1 else y0_scr)
        o_ref[...] = x_ref[...] + _rms(o_ref[...], gt_ref[...] * gpost_ref[...])


def _conv_ffn(x1, mod3, g_pre, g_post, w_up_bf, conv_w, conv_b, w_down_bf, tm):
    t, d = x1.shape
    tf = FFN_TF
    nf = D_FF // tf
    per_b = SEQ // tm
    modspec = lambda k: pl.BlockSpec((None, 1, d), lambda i, f: ((i // per_b) * N_MOD + k, 0, 0))
    up_f = lambda f: jnp.minimum(f, nf - 1)
    dn_f = lambda f: jnp.maximum(f - 1, 0)
    return pl.pallas_call(
        functools.partial(_ffn_kernel, per_b=per_b, nf=nf, sub=UP_SUB),
        grid=(t // tm, nf + 1),
        in_specs=[pl.BlockSpec((tm, d), lambda i, f: (i, 0)),
                  modspec(4), modspec(3), modspec(5),
                  pl.BlockSpec((1, d), lambda i, f: (0, 0)),
                  pl.BlockSpec((1, d), lambda i, f: (0, 0)),
                  pl.BlockSpec((d, tf), lambda i, f: (0, up_f(f))),
                  pl.BlockSpec((d, tf), lambda i, f: (0, nf + up_f(f))),
                  pl.BlockSpec((3, tf), lambda i, f: (0, up_f(f))),
                  pl.BlockSpec((3, tf), lambda i, f: (0, nf + up_f(f))),
                  pl.BlockSpec((1, tf), lambda i, f: (0, up_f(f))),
                  pl.BlockSpec((1, tf), lambda i, f: (0, nf + up_f(f))),
                  pl.BlockSpec((tf, d), lambda i, f: (dn_f(f), 0))],
        out_specs=pl.BlockSpec((tm, d), lambda i, f: (i, 0)),
        out_shape=jax.ShapeDtypeStruct((t, d), F32),
        scratch_shapes=[pltpu.VMEM((tm, d), BF16),
                        pltpu.VMEM((nf, 8, tf), F32),
                        pltpu.VMEM((nf, 8, tf), F32),
                        pltpu.VMEM((tm, tf), BF16),
                        pltpu.VMEM((tm, tf), BF16)],
        compiler_params=_params("arbitrary", "arbitrary"),
        name="conv_ffn",
    )(x1, mod3, mod3, mod3, g_pre, g_post, w_up_bf, w_up_bf, conv_w, conv_w,
      conv_b, conv_b, w_down_bf)


def _rot_half_cols(w):
    half = w.shape[-1] // 2
    return jnp.concatenate([-w[..., half:], w[..., :half]], axis=-1)


def _regroup_w_in(w):
    d = w.shape[0]
    c0 = Q_LORA_RANK
    c1 = c0 + KV_LORA_RANK
    c2 = c1 + QK_ROPE_DIM
    c3 = c2 + CONV_WIDTH
    c4 = c3 + CONV_WIDTH
    q_lat, kv_lat, k_rope = w[:, :c0], w[:, c0:c1], w[:, c1:c2]
    gate_b, gate_c, conv_in = w[:, c2:c3], w[:, c3:c4], w[:, c4:]
    pad = jnp.zeros((d, OFF_KV_LAT - OFF_K_ROPE - 2 * QK_ROPE_DIM), w.dtype)
    return jnp.concatenate([gate_b, gate_c, conv_in, q_lat, k_rope, _rot_half_cols(k_rope),
                            pad, kv_lat], axis=1)


def _block_cols(w, tn):
    k, n = w.shape
    return w.reshape(k, n // tn, tn).transpose(1, 0, 2)


def _regroup_w_uq(w):
    r = w.shape[0]
    w = w.reshape(r, MLA_HEADS, QK_NOPE_DIM + QK_ROPE_DIM)
    nope, rope = w[..., :QK_NOPE_DIM], w[..., QK_NOPE_DIM:]
    return jnp.concatenate([nope, rope, _rot_half_cols(rope)], axis=-1).reshape(r, MLA_HEADS * HEAD_QK)


def _regroup_w_ukv(w):
    r = w.shape[0]
    w = w.reshape(r, MLA_HEADS, QK_NOPE_DIM + V_HEAD_DIM)
    k = w[..., :QK_NOPE_DIM].reshape(r, MLA_HEADS * QK_NOPE_DIM)
    v = w[..., QK_NOPE_DIM:].reshape(r, MLA_HEADS * V_HEAD_DIM)
    return jnp.concatenate([k, v], axis=1)


def kernel(x, c, positions, w_ada, b_ada, g_pre_mix, g_post_mix, w_in, g_q, w_uq, g_kv, w_ukv,
           conv_w_mix, conv_b_mix, w_o, g_pre_ffn, g_post_ffn, w_up, conv_w_ffn, conv_b_ffn, w_down):
    b, s, d = x.shape
    assert (s, d) == (SEQ, D_MODEL) and w_ada.shape[0] == 1
    t = b * s
    x2 = x.reshape(t, d)
    pos3 = positions.reshape(t // ATT_BLK, 1, ATT_BLK)
    half = jnp.arange(0, QK_ROPE_DIM, 2, dtype=F32) / QK_ROPE_DIM
    invf = (1.0 / (ROPE_THETA ** half)).reshape(QK_ROPE_DIM // 2, 1)

    rows = 8
    c_pad = jnp.zeros((rows, d), F32).at[:b].set(c)
    x_cur = x2
    for l in range(w_ada.shape[0]):
        mod = _adaln_mod(c_pad, w_ada[l], b_ada[l].reshape(1, -1))
        mod3 = mod[:b].reshape(b * N_MOD, 1, d)

        w_in_blk = _block_cols(_regroup_w_in(w_in[l].astype(BF16)), IN_PROJ_TN)
        wq_p = _regroup_w_uq(w_uq[l]).astype(BF16)
        wkv_p = _regroup_w_ukv(w_ukv[l]).astype(BF16)

        proj = _in_proj(x_cur, mod3, g_pre_mix[l].reshape(1, d), w_in_blk, tm=1024)
        qT, k, vT = _latent_qkv(proj, pos3, invf, g_q[l].reshape(1, -1), g_kv[l].reshape(1, -1),
                                wq_p, wkv_p, tm=ATT_BLK)
        attn = _attention(qT, k, vT, blk=ATT_BLK, heads=ATT_HEADS_PER_STEP, q_blocks=ATT_Q_BLOCKS_PER_STEP)
        x_cur = _out_proj(attn, proj, conv_w_mix[l], conv_b_mix[l].reshape(1, -1), x_cur, mod3,
                          g_post_mix[l].reshape(1, d), w_o[l].astype(BF16), tm=512)
        x_cur = _conv_ffn(x_cur, mod3, g_pre_ffn[l].reshape(1, d), g_post_ffn[l].reshape(1, d),
                          w_up[l].astype(BF16), conv_w_ffn[l],
                          conv_b_ffn[l].reshape(1, -1), w_down[l].astype(BF16), tm=1024)
    return x_cur.reshape(b, s, d)
```

```python
import functools
import math

import jax
import jax.numpy as jnp
from jax import lax
from jax.experimental import pallas as pl
from jax.experimental.pallas import tpu as pltpu

D_MODEL = 2048
SEQ = 8192
CONV_WIDTH = 1024
MLA_HEADS = 8
QK_NOPE_DIM = 128
QK_ROPE_DIM = 64
V_HEAD_DIM = 128
Q_LORA_RANK = 768
KV_LORA_RANK = 512
ROPE_THETA = 10000.0
D_FF = 5632
RMS_EPS = 1e-6
N_MOD = 6

LANES = 128
HEAD_QK = 2 * LANES
PROJ_COLS = 4608
VMEM_LIMIT = 56 * 1024 * 1024
ATT_BLK = 512
ATT_HEADS_PER_STEP = 2
ATT_Q_BLOCKS_PER_STEP = 8
VT_ROWS = V_HEAD_DIM + 16
LOG2_E = 1.4426950408889634
IN_PROJ_TN = 1536
UP_DOT_ROWS = 512
UP_SUB = 512
FFN_TF = 512

OFF_GATE_B = 0
OFF_GATE_C = 1024
OFF_CONV_IN = 2048
OFF_Q_LAT = 3072
OFF_K_ROPE = 3840
OFF_KV_LAT = 4096

BF16 = jnp.bfloat16
F32 = jnp.float32


def _params(*sem):
    return pltpu.CompilerParams(dimension_semantics=sem, vmem_limit_bytes=VMEM_LIMIT)


def _rms(x, g):
    return x * lax.rsqrt(jnp.mean(x * x, axis=-1, keepdims=True) + RMS_EPS) * g


def _mod_kernel(c_ref, w_ref, b_ref, o_ref):
    c = c_ref[...]
    c_act = c * (1.0 / (1.0 + jnp.exp(-c)))
    o_ref[...] = jnp.dot(c_act.astype(BF16), w_ref[...].astype(BF16),
                         preferred_element_type=F32) + b_ref[...]


def _adaln_mod(c_pad, w_ada, b_ada):
    rows, d = c_pad.shape
    n = w_ada.shape[1]
    tn = 1536
    return pl.pallas_call(
        _mod_kernel,
        grid=(n // tn,),
        in_specs=[pl.BlockSpec((rows, d), lambda j: (0, 0)),
                  pl.BlockSpec((d, tn), lambda j: (0, j)),
                  pl.BlockSpec((1, tn), lambda j: (0, j))],
        out_specs=pl.BlockSpec((rows, tn), lambda j: (0, j)),
        out_shape=jax.ShapeDtypeStruct((rows, n), F32),
        compiler_params=_params("arbitrary"),
        name="adaln_mod",
    )(c_pad, w_ada, b_ada)


def _inproj_kernel(x_ref, sc_ref, sh_ref, g_ref, w_ref, o_ref, h_scr):
    tm = x_ref.shape[0]

    def project(rows):
        for r in range(0, tm, rows):
            o_ref[r:r + rows, :] = jnp.dot(h_scr[r:r + rows, :], w_ref[...],
                                           preferred_element_type=F32).astype(o_ref.dtype)

    @pl.when(pl.program_id(1) == 0)
    def _():
        rows = tm // 4
        for r in range(0, tm, rows):
            h = _rms(x_ref[r:r + rows, :], g_ref[...] * (1.0 + sc_ref[...])) + sh_ref[...]
            h_scr[r:r + rows, :] = h.astype(BF16)
        project(rows)

    @pl.when(pl.program_id(1) != 0)
    def _():
        project(tm)


def _in_proj(x2, mod3, g_pre, w_blk, tm):
    t, d = x2.shape
    nb, _, tn = w_blk.shape
    n = nb * tn
    per_b = SEQ // tm
    return pl.pallas_call(
        _inproj_kernel,
        grid=(t // tm, n // tn),
        in_specs=[pl.BlockSpec((tm, d), lambda i, j: (i, 0)),
                  pl.BlockSpec((None, 1, d), lambda i, j: ((i // per_b) * N_MOD + 1, 0, 0)),
                  pl.BlockSpec((None, 1, d), lambda i, j: ((i // per_b) * N_MOD + 0, 0, 0)),
                  pl.BlockSpec((1, d), lambda i, j: (0, 0)),
                  pl.BlockSpec((None, d, tn), lambda i, j: (j, 0, 0))],
        out_specs=pl.BlockSpec((tm, tn), lambda i, j: (i, j)),
        out_shape=jax.ShapeDtypeStruct((t, n), BF16),
        scratch_shapes=[pltpu.VMEM((tm, d), BF16)],
        compiler_params=_params("arbitrary", "arbitrary"),
        name="in_proj",
    )(x2, mod3, mod3, g_pre, w_blk)


def _qkv_kernel(ql_ref, kr_ref, kvl_ref, pos_ref, invf_ref, gq_ref, gkv_ref,
                wq_ref, wkv_ref, qT_ref, k_ref, vT_ref):
    tm = ql_ref.shape[0]
    ang = invf_ref[...] * pos_ref[...].astype(F32)
    cos = jnp.cos(ang)
    sin = jnp.sin(ang)
    cos2 = jnp.concatenate([cos, cos], axis=0)
    sin2 = jnp.concatenate([sin, sin], axis=0)

    def rope_t(a):
        at = a.T
        return at[:QK_ROPE_DIM, :] * cos2 + at[QK_ROPE_DIM:, :] * sin2

    qn = _rms(ql_ref[...].astype(F32), gq_ref[...]).astype(BF16)
    q = jnp.dot(qn, wq_ref[...], preferred_element_type=F32)
    scale = LOG2_E / math.sqrt(QK_NOPE_DIM + QK_ROPE_DIM)
    kvn = _rms(kvl_ref[...].astype(F32), gkv_ref[...]).astype(BF16)
    kv = jnp.dot(kvn, wkv_ref[...], preferred_element_type=F32)
    zeros_t = jnp.zeros((LANES - QK_ROPE_DIM, tm), F32)
    k_rope = jnp.concatenate([rope_t(kr_ref[...].astype(F32)), zeros_t], axis=0).T.astype(BF16)
    v_off = MLA_HEADS * QK_NOPE_DIM
    ones = jnp.ones((VT_ROWS - V_HEAD_DIM, tm), BF16)
    for h in range(MLA_HEADS):
        lo = h * HEAD_QK
        qT_ref[lo:lo + LANES, :] = (q[:, lo:lo + LANES] * scale).T.astype(BF16)
        qT_ref[lo + LANES:lo + LANES + QK_ROPE_DIM, :] = (rope_t(q[:, lo + LANES:lo + HEAD_QK]) * scale).astype(BF16)
        qT_ref[lo + LANES + QK_ROPE_DIM:lo + HEAD_QK, :] = zeros_t.astype(BF16)
        k_ref[:, lo:lo + LANES] = kv[:, h * LANES:(h + 1) * LANES].astype(BF16)
        k_ref[:, lo + LANES:lo + HEAD_QK] = k_rope
        vlo = h * VT_ROWS
        vT_ref[vlo:vlo + V_HEAD_DIM, :] = kv[:, v_off + h * LANES:v_off + (h + 1) * LANES].T.astype(BF16)
        vT_ref[vlo + V_HEAD_DIM:vlo + VT_ROWS, :] = ones


def _latent_qkv(proj, pos3, invf, g_q, g_kv, wq_p, wkv_p, tm):
    t = proj.shape[0]
    hq = MLA_HEADS * HEAD_QK
    hv = MLA_HEADS * V_HEAD_DIM
    hvt = MLA_HEADS * VT_ROWS
    const = lambda i: (0, 0)
    return pl.pallas_call(
        _qkv_kernel,
        grid=(t // tm,),
        in_specs=[pl.BlockSpec((tm, Q_LORA_RANK), lambda i: (i, OFF_Q_LAT // Q_LORA_RANK)),
                  pl.BlockSpec((tm, LANES), lambda i: (i, OFF_K_ROPE // LANES)),
                  pl.BlockSpec((tm, KV_LORA_RANK), lambda i: (i, OFF_KV_LAT // KV_LORA_RANK)),
                  pl.BlockSpec((None, 1, tm), lambda i: (i, 0, 0)),
                  pl.BlockSpec((QK_ROPE_DIM // 2, 1), const),
                  pl.BlockSpec((1, Q_LORA_RANK), const),
                  pl.BlockSpec((1, KV_LORA_RANK), const),
                  pl.BlockSpec((Q_LORA_RANK, hq), const),
                  pl.BlockSpec((KV_LORA_RANK, 2 * hv), const)],
        out_specs=[pl.BlockSpec((None, hq, tm), lambda i: (i, 0, 0)),
                   pl.BlockSpec((tm, hq), lambda i: (i, 0)),
                   pl.BlockSpec((None, hvt, tm), lambda i: (i, 0, 0))],
        out_shape=[jax.ShapeDtypeStruct((t // tm, hq, tm), BF16),
                   jax.ShapeDtypeStruct((t, hq), BF16),
                   jax.ShapeDtypeStruct((t // tm, hvt, tm), BF16)],
        compiler_params=_params("arbitrary"),
        name="latent_qkv",
    )(proj, proj, proj, pos3, invf, g_q, g_kv, wq_p, wkv_p)


def _causal_conv3(u, halo, w_ref, b_ref):
    w0, w1, w2, b = w_ref[0:1, :], w_ref[1:2, :], w_ref[2:3, :], b_ref[...]

    def taps(p2, p1, p0):
        return p2 * w0 + p1 * w1 + p0 * w2 + b

    body = taps(pltpu.roll(u, 2, 0), pltpu.roll(u, 1, 0), u)
    head = jnp.concatenate([halo, u[0:8, :]], axis=0)
    first = taps(pltpu.roll(head, 2, 0), pltpu.roll(head, 1, 0), head)[8:16, :]
    return jnp.concatenate([first, body[8:, :]], axis=0)


def _attn_kernel(qT_ref, k_ref, vT_ref, o_ref, *scratch, blk, heads, q_blocks):
    per_head = 4
    state = [scratch[per_head * hh:per_head * (hh + 1)] for hh in range(heads)]

    def q_rows(hh):
        return slice(hh * HEAD_QK, (hh + 1) * HEAD_QK)

    def one_q_block(qb):
        qi = pl.program_id(2) * q_blocks + qb

        def scores(hh, j):
            start = pl.multiple_of(j * blk, blk)
            return jnp.dot(k_ref[pl.ds(start, blk), q_rows(hh)], qT_ref[qb, q_rows(hh), :],
                           preferred_element_type=F32)

        def consume(hh, sT, j):
            _, _, m_scr, acc_scr = state[hh]
            m_old = m_scr[...]
            m_new = jnp.maximum(m_old, jnp.max(sT, axis=0, keepdims=True))
            alpha = jnp.exp2(m_old - m_new)
            pT = jnp.exp2(sT - m_new).astype(BF16)
            vT = vT_ref[j, hh * VT_ROWS:(hh + 1) * VT_ROWS, :]
            acc_scr[...] = alpha * acc_scr[...] + jnp.dot(vT, pT, preferred_element_type=F32)
            m_scr[...] = m_new

        def consume_diag(hh, s_ref):
            sT = s_ref[...]
            r = lax.broadcasted_iota(jnp.int32, sT.shape, 0)
            c = lax.broadcasted_iota(jnp.int32, sT.shape, 1)
            consume(hh, jnp.where(r <= c, sT, -1e30), qi)

        def pair(j):
            for hh in range(heads):
                sa, sb, _, _ = state[hh]
                sb[...] = scores(hh, j + 1)
                consume(hh, sa[...], j)
            for hh in range(heads):
                sa, sb, _, _ = state[hh]
                sa[...] = scores(hh, j + 2)
                consume(hh, sb[...], j + 1)

        for hh in range(heads):
            sa, _, m_scr, acc_scr = state[hh]
            m_scr[...] = jnp.full(m_scr.shape, -1e30, F32)
            acc_scr[...] = jnp.zeros(acc_scr.shape, F32)
            sa[...] = scores(hh, 0)

        def body(t, carry):
            pair(4 * t)
            pair(4 * t + 2)
            return carry

        lax.fori_loop(0, qi // 4, body, 0)
        rem = qi % 4

        @pl.when(rem >= 2)
        def _():
            pair(qi - rem)

        @pl.when(rem % 2 == 0)
        def _():
            for hh in range(heads):
                consume_diag(hh, state[hh][0])

        @pl.when(rem % 2 == 1)
        def _():
            for hh in range(heads):
                sa, sb, _, _ = state[hh]
                sb[...] = scores(hh, qi)
                consume(hh, sa[...], qi - 1)
            for hh in range(heads):
                consume_diag(hh, state[hh][1])

        for hh in range(heads):
            acc = state[hh][3][...]
            o_ref[qb * blk:(qb + 1) * blk, hh * V_HEAD_DIM:(hh + 1) * V_HEAD_DIM] = (
                acc[:V_HEAD_DIM] / acc[V_HEAD_DIM:V_HEAD_DIM + 1]).T.astype(o_ref.dtype)

    for qb in range(q_blocks):
        one_q_block(qb)


def _attention(qT, k, vT, blk, heads, q_blocks):
    t = k.shape[0]
    b = t // SEQ
    nq = SEQ // blk
    nqs = nq // q_blocks
    per_head_scratch = [pltpu.VMEM((blk, blk), F32),
                        pltpu.VMEM((blk, blk), F32),
                        pltpu.VMEM((1, blk), F32),
                        pltpu.VMEM((VT_ROWS, blk), F32)]
    return pl.pallas_call(
        functools.partial(_attn_kernel, blk=blk, heads=heads, q_blocks=q_blocks),
        grid=(b, MLA_HEADS // heads, nqs),
        in_specs=[pl.BlockSpec((q_blocks, heads * HEAD_QK, blk), lambda bi, g, i: (bi * nqs + i, g, 0)),
                  pl.BlockSpec((SEQ, heads * HEAD_QK), lambda bi, g, i: (bi, g)),
                  pl.BlockSpec((nq, heads * VT_ROWS, blk), lambda bi, g, i: (bi, g, 0))],
        out_specs=pl.BlockSpec((q_blocks * blk, heads * V_HEAD_DIM), lambda bi, g, i: (bi * nqs + i, g)),
        out_shape=jax.ShapeDtypeStruct((t, MLA_HEADS * V_HEAD_DIM), BF16),
        scratch_shapes=per_head_scratch * heads,
        compiler_params=_params("arbitrary", "arbitrary", "arbitrary"),
        name="attention",
    )(qT, k, vT)


def _outproj_kernel(a_ref, gb_ref, gc_ref, ci_ref, gch_ref, cih_ref, cw_ref, cb_ref,
                    x_ref, gt_ref, g_ref, wa_ref, wc_ref, o_ref, *, per_b):
    first = (pl.program_id(0) % per_b) == 0
    g = gc_ref[...].astype(F32) * ci_ref[...].astype(F32)
    gh = gch_ref[...].astype(F32) * cih_ref[...].astype(F32)
    halo = jnp.where(first, 0.0, gh[8:16, :])
    conv = (gb_ref[...].astype(F32) * _causal_conv3(g, halo, cw_ref, cb_ref)).astype(BF16)
    mix = (jnp.dot(a_ref[...], wa_ref[...], preferred_element_type=F32)
           + jnp.dot(conv, wc_ref[...], preferred_element_type=F32))
    o_ref[...] = x_ref[...] + _rms(mix, gt_ref[...] * g_ref[...])


def _out_proj(attn, proj, conv_w, conv_b, x2, mod3, g_post, w_o_bf, tm):
    t, d = x2.shape
    ka = attn.shape[1]
    c = CONV_WIDTH
    hb = 16
    per_b = SEQ // tm
    halo_idx = lambda col: (lambda i: (jnp.maximum(i * (tm // hb) - 1, 0), col))
    return pl.pallas_call(
        functools.partial(_outproj_kernel, per_b=per_b),
        grid=(t // tm,),
        in_specs=[pl.BlockSpec((tm, ka), lambda i: (i, 0)),
                  pl.BlockSpec((tm, c), lambda i: (i, OFF_GATE_B // c)),
                  pl.BlockSpec((tm, c), lambda i: (i, OFF_GATE_C // c)),
                  pl.BlockSpec((tm, c), lambda i: (i, OFF_CONV_IN // c)),
                  pl.BlockSpec((hb, c), halo_idx(OFF_GATE_C // c)),
                  pl.BlockSpec((hb, c), halo_idx(OFF_CONV_IN // c)),
                  pl.BlockSpec((3, c), lambda i: (0, 0)),
                  pl.BlockSpec((1, c), lambda i: (0, 0)),
                  pl.BlockSpec((tm, d), lambda i: (i, 0)),
                  pl.BlockSpec((None, 1, d), lambda i: ((i // per_b) * N_MOD + 2, 0, 0)),
                  pl.BlockSpec((1, d), lambda i: (0, 0)),
                  pl.BlockSpec((ka, d), lambda i: (0, 0)),
                  pl.BlockSpec((c, d), lambda i: (ka // c, 0))],
        out_specs=pl.BlockSpec((tm, d), lambda i: (i, 0)),
        out_shape=jax.ShapeDtypeStruct((t, d), F32),
        compiler_params=_params("arbitrary"),
        name="out_proj",
    )(attn, proj, proj, proj, proj, proj, conv_w, conv_b, x2, mod3, g_post, w_o_bf, w_o_bf)


def _ffn_kernel(x_ref, sc_ref, sh_ref, gt_ref, gpre_ref, gpost_ref, wa_ref, wg_ref,
                cwa_ref, cwg_ref, cba_ref, cbg_ref, wd_ref, o_ref,
                h_scr, halo_a, halo_g, y0_scr, y1_scr, *, per_b, nf, sub):
    i = pl.program_id(0)
    f = pl.program_id(1)
    tm = x_ref.shape[0]
    tf = wa_ref.shape[1]
    first = (i % per_b) == 0

    def branch(w_ref, cw_ref, cb_ref, halo_ref, cols):
        w = w_ref[:, cols]
        u = jnp.concatenate([jnp.dot(h_scr[r:r + UP_DOT_ROWS, :], w, preferred_element_type=F32)
                             for r in range(0, tm, UP_DOT_ROWS)], axis=0)
        halo = jnp.where(first, 0.0, halo_ref[f, :, cols])
        halo_ref[f, :, cols] = u[tm - 8:, :]
        return _causal_conv3(u, halo, cw_ref.at[:, cols], cb_ref.at[:, cols])

    def up(y_out):
        for s in range(0, tf, sub):
            cols = slice(s, s + sub)
            a = branch(wa_ref, cwa_ref, cba_ref, halo_a, cols)
            g = branch(wg_ref, cwg_ref, cbg_ref, halo_g, cols)
            hg = 0.5 * g
            y_out[:, cols] = ((hg + hg * jnp.tanh(hg)) * a).astype(BF16)

    def down(y_in):
        o_ref[...] += jnp.dot(y_in[...], wd_ref[...], preferred_element_type=F32)

    @pl.when(f == 0)
    def _():
        h = _rms(x_ref[...], gpre_ref[...] * (1.0 + sc_ref[...])) + sh_ref[...]
        h_scr[...] = h.astype(BF16)
        o_ref[...] = jnp.zeros(o_ref.shape, F32)
        up(y0_scr)

    interior = jnp.logical_and(f > 0, f < nf)

    @pl.when(jnp.logical_and(interior, f % 2 == 1))
    def _():
        up(y1_scr)
        down(y0_scr)

    @pl.when(jnp.logical_and(interior, f % 2 == 0))
    def _():
        up(y0_scr)
        down(y1_scr)

    @pl.when(f == nf)
    def _():
        down(y1_scr if (nf - 1) % 2 == 1 else y0_scr)
        o_ref[...] = x_ref[...] + _rms(o_ref[...], gt_ref[...] * gpost_ref[...])


def _conv_ffn(x1, mod3, g_pre, g_post, w_up_bf, conv_w, conv_b, w_down_bf, tm):
    t, d = x1.shape
    tf = FFN_TF
    nf = D_FF // tf
    per_b = SEQ // tm
    modspec = lambda k: pl.BlockSpec((None, 1, d), lambda i, f: ((i // per_b) * N_MOD + k, 0, 0))
    up_f = lambda f: jnp.minimum(f, nf - 1)
    dn_f = lambda f: jnp.maximum(f - 1, 0)
    return pl.pallas_call(
        functools.partial(_ffn_kernel, per_b=per_b, nf=nf, sub=UP_SUB),
        grid=(t // tm, nf + 1),
        in_specs=[pl.BlockSpec((tm, d), lambda i, f: (i, 0)),
                  modspec(4), modspec(3), modspec(5),
                  pl.BlockSpec((1, d), lambda i, f: (0, 0)),
                  pl.BlockSpec((1, d), lambda i, f: (0, 0)),
                  pl.BlockSpec((d, tf), lambda i, f: (0, up_f(f))),
                  pl.BlockSpec((d, tf), lambda i, f: (0, nf + up_f(f))),
                  pl.BlockSpec((3, tf), lambda i, f: (0, up_f(f))),
                  pl.BlockSpec((3, tf), lambda i, f: (0, nf + up_f(f))),
                  pl.BlockSpec((1, tf), lambda i, f: (0, up_f(f))),
                  pl.BlockSpec((1, tf), lambda i, f: (0, nf + up_f(f))),
                  pl.BlockSpec((tf, d), lambda i, f: (dn_f(f), 0))],
        out_specs=pl.BlockSpec((tm, d), lambda i, f: (i, 0)),
        out_shape=jax.ShapeDtypeStruct((t, d), F32),
        scratch_shapes=[pltpu.VMEM((tm, d), BF16),
                        pltpu.VMEM((nf, 8, tf), F32),
                        pltpu.VMEM((nf, 8, tf), F32),
                        pltpu.VMEM((tm, tf), BF16),
                        pltpu.VMEM((tm, tf), BF16)],
        compiler_params=_params("arbitrary", "arbitrary"),
        name="conv_ffn",
    )(x1, mod3, mod3, mod3, g_pre, g_post, w_up_bf, w_up_bf, conv_w, conv_w,
      conv_b, conv_b, w_down_bf)


def _rot_half_cols(w):
    half = w.shape[-1] // 2
    return jnp.concatenate([-w[..., half:], w[..., :half]], axis=-1)


def _regroup_w_in(w):
    d = w.shape[0]
    c0 = Q_LORA_RANK
    c1 = c0 + KV_LORA_RANK
    c2 = c1 + QK_ROPE_DIM
    c3 = c2 + CONV_WIDTH
    c4 = c3 + CONV_WIDTH
    q_lat, kv_lat, k_rope = w[:, :c0], w[:, c0:c1], w[:, c1:c2]
    gate_b, gate_c, conv_in = w[:, c2:c3], w[:, c3:c4], w[:, c4:]
    pad = jnp.zeros((d, OFF_KV_LAT - OFF_K_ROPE - 2 * QK_ROPE_DIM), w.dtype)
    return jnp.concatenate([gate_b, gate_c, conv_in, q_lat, k_rope, _rot_half_cols(k_rope),
                            pad, kv_lat], axis=1)


def _block_cols(w, tn):
    k, n = w.shape
    return w.reshape(k, n // tn, tn).transpose(1, 0, 2)


def _regroup_w_uq(w):
    r = w.shape[0]
    w = w.reshape(r, MLA_HEADS, QK_NOPE_DIM + QK_ROPE_DIM)
    nope, rope = w[..., :QK_NOPE_DIM], w[..., QK_NOPE_DIM:]
    return jnp.concatenate([nope, rope, _rot_half_cols(rope)], axis=-1).reshape(r, MLA_HEADS * HEAD_QK)


def _regroup_w_ukv(w):
    r = w.shape[0]
    w = w.reshape(r, MLA_HEADS, QK_NOPE_DIM + V_HEAD_DIM)
    k = w[..., :QK_NOPE_DIM].reshape(r, MLA_HEADS * QK_NOPE_DIM)
    v = w[..., QK_NOPE_DIM:].reshape(r, MLA_HEADS * V_HEAD_DIM)
    return jnp.concatenate([k, v], axis=1)


def kernel(x, c, positions, w_ada, b_ada, g_pre_mix, g_post_mix, w_in, g_q, w_uq, g_kv, w_ukv,
           conv_w_mix, conv_b_mix, w_o, g_pre_ffn, g_post_ffn, w_up, conv_w_ffn, conv_b_ffn, w_down):
    b, s, d = x.shape
    assert (s, d) == (SEQ, D_MODEL) and w_ada.shape[0] == 1
    t = b * s
    x2 = x.reshape(t, d)
    pos3 = positions.reshape(t // ATT_BLK, 1, ATT_BLK)
    half = jnp.arange(0, QK_ROPE_DIM, 2, dtype=F32) / QK_ROPE_DIM
    invf = (1.0 / (ROPE_THETA ** half)).reshape(QK_ROPE_DIM // 2, 1)

    rows = 8
    c_pad = jnp.zeros((rows, d), F32).at[:b].set(c)
    x_cur = x2
    for l in range(w_ada.shape[0]):
        mod = _adaln_mod(c_pad, w_ada[l], b_ada[l].reshape(1, -1))
        mod3 = mod[:b].reshape(b * N_MOD, 1, d)

        w_in_blk = _block_cols(_regroup_w_in(w_in[l].astype(BF16)), IN_PROJ_TN)
        wq_p = _regroup_w_uq(w_uq[l]).astype(BF16)
        wkv_p = _regroup_w_ukv(w_ukv[l]).astype(BF16)

        proj = _in_proj(x_cur, mod3, g_pre_mix[l].reshape(1, d), w_in_blk, tm=1024)
        qT, k, vT = _latent_qkv(proj, pos3, invf, g_q[l].reshape(1, -1), g_kv[l].reshape(1, -1),
                                wq_p, wkv_p, tm=ATT_BLK)
        attn = _attention(qT, k, vT, blk=ATT_BLK, heads=ATT_HEADS_PER_STEP, q_blocks=ATT_Q_BLOCKS_PER_STEP)
        x_cur = _out_proj(attn, proj, conv_w_mix[l], conv_b_mix[l].reshape(1, -1), x_cur, mod3,
                          g_post_mix[l].reshape(1, d), w_o[l].astype(BF16), tm=512)
        x_cur = _conv_ffn(x_cur, mod3, g_pre_ffn[l].reshape(1, d), g_post_ffn[l].reshape(1, d),
                          w_up[l].astype(BF16), conv_w_ffn[l],
                          conv_b_ffn[l].reshape(1, -1), w_down[l].astype(BF16), tm=1024)
    return x_cur.reshape(b, s, d)
```
